```python
import math
import jax, jax.numpy as jnp
from jax import lax
import numpy as np

D_MODEL = 1024
BATCH = 2
SEQ = 8192
DEPTH = 2

HG_HEADS = 4
HG_DK = 128
HG_DV = 128
HG_WIDTH = HG_HEADS * HG_DK
HG_CHUNK = 64
DA_HEADS = 4
DA_DQK = 64
DA_DV = 2 * DA_DQK
DA_WIDTH = DA_HEADS * DA_DV
DA_QBLOCK = 128
ALIBI_MAX_BIAS = 8.0
D_FF = ((8 * D_MODEL // 3 + 255) // 256) * 256
N_BRANCH = 2
EPS = 1e-6
IN_SIZES = (HG_HEADS * HG_DK, HG_HEADS * HG_DK, HG_HEADS * HG_DV, HG_HEADS * HG_DV,
            DA_HEADS * 2 * DA_DQK, DA_HEADS * 2 * DA_DQK, DA_HEADS * DA_DV,
            N_BRANCH * D_MODEL)
D_IN = sum(IN_SIZES)

kernel_name = "hybrid_hgrn2_diffattn_gated_block"


def _split_points():
    return tuple(int(v) for v in np.cumsum(np.array(IN_SIZES))[:-1])


def rmsnorm(x, gain):
    xf = x.astype(jnp.float32)
    y = xf * lax.rsqrt(jnp.mean(xf * xf, axis=-1, keepdims=True) + EPS)
    return (y * gain.astype(jnp.float32)).astype(x.dtype)


def hgrn2_mix(q_raw, f_raw, i_raw, g_raw, lb, out_gain):
    B, S, _ = q_raw.shape
    f32 = jnp.float32
    lb = lb.astype(f32)
    z = f_raw.astype(f32)
    log_f = jnp.logaddexp(jnp.log(lb), jnp.log1p(-lb) + jax.nn.log_sigmoid(z))
    k = (1.0 - lb) * jax.nn.sigmoid(-z)
    q = jax.nn.silu(q_raw.astype(f32))
    v = i_raw.astype(f32)
    n = S // HG_CHUNK

    def to_chunks(t, d):
        return t.reshape(B, n, HG_CHUNK, HG_HEADS, d).transpose(1, 0, 3, 2, 4)

    causal = jnp.tril(jnp.ones((HG_CHUNK, HG_CHUNK), dtype=bool))

    def step(state, xs):
        qc, kc, vc, lfc = xs
        b = jnp.cumsum(lfc, axis=-2)
        rel = b[..., :, None, :] - b[..., None, :, :]
        decay = jnp.exp(jnp.where(causal[:, :, None], rel, -jnp.inf))
        scores = jnp.einsum('bhtd,bhtsd,bhsd->bhts', qc, decay, kc)
        o = jnp.einsum('bhts,bhse->bhte', scores, vc) + jnp.einsum('bhtd,bhde->bhte', qc * jnp.exp(b), state)
        b_last = b[..., -1:, :]
        new_state = jnp.exp(b_last[..., 0, :])[..., None] * state + jnp.einsum(
            'bhsd,bhse->bhde', kc * jnp.exp(b_last - b), vc)
        return new_state, o

    s0 = jnp.zeros((B, HG_HEADS, HG_DK, HG_DV), f32)
    _, o = lax.scan(step, s0, (to_chunks(q, HG_DK), to_chunks(k, HG_DK),
                               to_chunks(v, HG_DV), to_chunks(log_f, HG_DK)))
    o = o.transpose(1, 0, 3, 2, 4).reshape(B, S, HG_HEADS, HG_DV)
    g = jax.nn.silu(g_raw.astype(f32)).reshape(B, S, HG_HEADS, HG_DV)
    o = rmsnorm(o, out_gain) * g
    return o.reshape(B, S, HG_HEADS * HG_DV).astype(q_raw.dtype)


def diff_attention(q_raw, k_raw, v_raw, lam, lam_init, subln_gain):
    B, S, _ = q_raw.shape
    f32 = jnp.float32
    q = q_raw.reshape(B, S, DA_HEADS, 2, DA_DQK).transpose(0, 2, 3, 1, 4)
    k = k_raw.reshape(B, S, DA_HEADS, 2, DA_DQK).transpose(0, 2, 3, 1, 4)
    v = v_raw.reshape(B, S, DA_HEADS, DA_DV).transpose(0, 2, 1, 3)
    nb = S // DA_QBLOCK
    qb = q.reshape(B, DA_HEADS, 2, nb, DA_QBLOCK, DA_DQK).transpose(3, 0, 1, 2, 4, 5)
    slopes = jnp.exp2(-ALIBI_MAX_BIAS / DA_HEADS * jnp.arange(1, DA_HEADS + 1, dtype=f32))
    key_pos = jnp.arange(S)
    scale = 1.0 / math.sqrt(DA_DQK)

    def block(args):
        qblk, idx = args
        s = jnp.einsum('bhcqd,bhckd->bhcqk', qblk, k).astype(f32) * scale
        dist = (idx * DA_QBLOCK + jnp.arange(DA_QBLOCK))[:, None] - key_pos[None, :]
        alibi = slopes[:, None, None] * dist.astype(f32)[None]
        s = jnp.where(dist >= 0, s - alibi[None, :, None], -jnp.inf)
        p = jax.nn.softmax(s, axis=-1)
        p = p[:, :, 0] - lam * p[:, :, 1]
        return jnp.einsum('bhqk,bhke->bhqe', p.astype(v.dtype), v)

    o = lax.map(block, (qb, jnp.arange(nb)))
    o = o.transpose(1, 0, 3, 2, 4).reshape(B, S, DA_HEADS, DA_DV)
    o = rmsnorm(o, subln_gain) * (1.0 - lam_init)
    return o.reshape(B, S, DA_WIDTH).astype(q_raw.dtype)


def setup_inputs(seed: int = 0) -> dict:
    key = jax.random.key(seed)
    ks = jax.random.split(key, 20)
    nrm = jax.random.normal

    def w(k, shape, fan_in):
        return nrm(k, shape, jnp.float32) * fan_in ** -0.5

    def gain(k, shape):
        return 1.0 + 0.02 * nrm(k, shape, jnp.float32)

    return {
        "x": nrm(ks[0], (BATCH, SEQ, D_MODEL), jnp.float32),
        "lower_bounds": 0.1 * nrm(ks[1], (DEPTH, HG_WIDTH), jnp.float32),
        "norm_mix_pre": gain(ks[2], (DEPTH, D_MODEL)),
        "norm_mix_post": gain(ks[3], (DEPTH, D_MODEL)),
        "norm_ffn_pre": gain(ks[4], (DEPTH, D_MODEL)),
        "norm_ffn_post": gain(ks[5], (DEPTH, D_MODEL)),
        "w_in": w(ks[6], (DEPTH, D_MODEL, D_IN), D_MODEL),
        "hg_out_norm": gain(ks[7], (DEPTH, HG_DV)),
        "da_subln": gain(ks[8], (DEPTH, DA_DV)),
        "lambda_q1": 0.1 * nrm(ks[9], (DEPTH, DA_DQK), jnp.float32),
        "lambda_k1": 0.1 * nrm(ks[10], (DEPTH, DA_DQK), jnp.float32),
        "lambda_q2": 0.1 * nrm(ks[11], (DEPTH, DA_DQK), jnp.float32),
        "lambda_k2": 0.1 * nrm(ks[12], (DEPTH, DA_DQK), jnp.float32),
        "w_up_a": w(ks[13], (DEPTH, HG_WIDTH, D_MODEL), HG_WIDTH),
        "w_up_b": w(ks[14], (DEPTH, DA_WIDTH, D_MODEL), DA_WIDTH),
        "w_out": w(ks[15], (DEPTH, D_MODEL, D_MODEL), D_MODEL),
        "w_ffn_gate": w(ks[16], (DEPTH, D_MODEL, D_FF), D_MODEL),
        "w_ffn_up": w(ks[17], (DEPTH, D_MODEL, D_FF), D_MODEL),
        "w_ffn_down": w(ks[18], (DEPTH, D_FF, D_MODEL), D_FF),
    }


def reference(x, lower_bounds, norm_mix_pre, norm_mix_post, norm_ffn_pre, norm_ffn_post,
              w_in, hg_out_norm, da_subln, lambda_q1, lambda_k1, lambda_q2, lambda_k2,
              w_up_a, w_up_b, w_out, w_ffn_gate, w_ffn_up, w_ffn_down):
    f32 = jnp.float32
    lb_all = jnp.cumsum(jax.nn.softmax(lower_bounds.astype(f32), axis=0), axis=0)
    lb_all = lb_all - lb_all[0:1]
    split_pts = _split_points()
    for l in range(DEPTH):
        h = rmsnorm(x, norm_mix_pre[l])
        proj = jnp.einsum('bsd,de->bse', h, w_in[l])
        q_a, f_a, i_a, g_a, q_b, k_b, v_b, gate_raw = jnp.split(proj, split_pts, axis=-1)
        y_a = hgrn2_mix(q_a, f_a, i_a, g_a, lb_all[l], hg_out_norm[l])
        lam_init = 0.8 - 0.6 * math.exp(-0.3 * l)
        lam = (jnp.exp(jnp.sum(lambda_q1[l].astype(f32) * lambda_k1[l].astype(f32)))
               - jnp.exp(jnp.sum(lambda_q2[l].astype(f32) * lambda_k2[l].astype(f32))) + lam_init)
        y_b = diff_attention(q_b, k_b, v_b, lam, lam_init, da_subln[l])
        gate_a, gate_b = jnp.split(jax.nn.sigmoid(gate_raw), N_BRANCH, axis=-1)
        merged = (gate_a * jnp.einsum('bsc,cd->bsd', y_a, w_up_a[l])
                  + gate_b * jnp.einsum('bsc,cd->bsd', y_b, w_up_b[l]))
        mix = jnp.einsum('bsd,de->bse', merged, w_out[l])
        x = x + rmsnorm(mix, norm_mix_post[l])
        h = rmsnorm(x, norm_ffn_pre[l])
        ff = jax.nn.silu(jnp.einsum('bsd,df->bsf', h, w_ffn_gate[l])) * jnp.einsum('bsd,df->bsf', h, w_ffn_up[l])
        ff = jnp.einsum('bsf,fd->bsd', ff, w_ffn_down[l])
        x = x + rmsnorm(ff, norm_ffn_post[l])
    return x
```

```python
import functools
import math

import jax
import jax.numpy as jnp
from jax import lax
from jax.experimental import pallas as pl
from jax.experimental.pallas import tpu as pltpu

F32 = jnp.float32
BF16 = jnp.bfloat16

D_MODEL = 1024
HEADS = 4
HEAD_W = 128
MIX_W = HEADS * HEAD_W
DQK = 64
D_FF = 2816
D_IN = 7 * MIX_W + 2 * D_MODEL
EPS = 1e-6
ALIBI_MAX_BIAS = 8.0
NEG_BIG = -1e30

IN_TM, IN_TN = 1024, 512
HG_T, HG_C = 512, 64
ATT_T = 256
MO_TM = 512
FF_TM, FF_TF = 512, 256
VMEM_LIMIT = 56 * 1024 * 1024


def _sigmoid(x):
    return 1.0 / (1.0 + jnp.exp(-x))


def _rms(x, gain):
    return x * lax.rsqrt(jnp.mean(x * x, axis=-1, keepdims=True) + EPS) * gain


def _prenorm_kernel(x_ref, g_ref, h_ref):
    h_ref[...] = _rms(x_ref[...], g_ref[...]).astype(BF16)


def _prenorm(x2d, gain):
    m = x2d.shape[0]
    tm = 1024
    return pl.pallas_call(
        _prenorm_kernel,
        grid=(m // tm,),
        in_specs=[pl.BlockSpec((tm, D_MODEL), lambda i: (i, 0)),
                  pl.BlockSpec((1, D_MODEL), lambda i: (0, 0))],
        out_specs=pl.BlockSpec((tm, D_MODEL), lambda i: (i, 0)),
        out_shape=jax.ShapeDtypeStruct((m, D_MODEL), BF16),
        compiler_params=pltpu.CompilerParams(dimension_semantics=("parallel",)),
        name="prenorm",
    )(x2d, gain)


def _in_proj_kernel(h_ref, w_ref, lb_ref, o_ref, lf_ref):
    j = pl.program_id(1)
    acc = jnp.dot(h_ref[...], w_ref[...], preferred_element_type=F32)

    @pl.when((j == 0) | (j == 3))
    def _():
        o_ref[...] = (acc * _sigmoid(acc)).astype(BF16)

    @pl.when(j == 1)
    def _():
        log_lb, log_1m_lb, one_m_lb = lb_ref[0:1, :], lb_ref[1:2, :], lb_ref[2:3, :]
        e = jnp.exp(-jnp.abs(acc))
        log_sig = jnp.minimum(acc, 0.0) - jnp.log(1.0 + e)
        c = log_1m_lb + log_sig
        hi = jnp.maximum(log_lb, c)
        lf_ref[...] = hi + jnp.log(1.0 + jnp.exp(-jnp.abs(log_lb - c)))
        sig_neg = jnp.where(acc >= 0.0, e, 1.0) / (1.0 + e)
        o_ref[...] = (one_m_lb * sig_neg).astype(BF16)

    @pl.when((j == 2) | (j == 5) | (j == 6))
    def _():
        o_ref[...] = acc.astype(BF16)

    @pl.when(j == 4)
    def _():
        o_ref[...] = (acc * (1.0 / math.sqrt(DQK))).astype(BF16)

    @pl.when(j >= 7)
    def _():
        o_ref[...] = _sigmoid(acc).astype(BF16)


def _in_proj(h, w, lbp):
    m = h.shape[0]
    return pl.pallas_call(
        _in_proj_kernel,
        grid=(m // IN_TM, D_IN // IN_TN),
        in_specs=[pl.BlockSpec((IN_TM, D_MODEL), lambda i, j: (i, 0)),
                  pl.BlockSpec((D_MODEL, IN_TN), lambda i, j: (0, j)),
                  pl.BlockSpec((8, MIX_W), lambda i, j: (0, 0))],
        out_specs=[pl.BlockSpec((IN_TM, IN_TN), lambda i, j: (i, j)),
                   pl.BlockSpec((IN_TM, MIX_W), lambda i, j: (i, 0))],
        out_shape=[jax.ShapeDtypeStruct((m, D_IN), BF16),
                   jax.ShapeDtypeStruct((m, MIX_W), F32)],
        compiler_params=pltpu.CompilerParams(dimension_semantics=("parallel", "arbitrary"),
                                             vmem_limit_bytes=VMEM_LIMIT),
        name="in_proj",
    )(h, w, lbp)


def _row_bcast(b, rows, w):
    return jnp.concatenate([jnp.broadcast_to(b[r:r + 1, :], (w, HEAD_W)) for r in rows], axis=0)


def _hgrn_kernel(q_ref, lf_ref, k_ref, v_ref, g_ref, gain_ref, o_ref, st_ref):
    @pl.when(pl.program_id(1) == 0)
    def _():
        st_ref[...] = jnp.zeros_like(st_ref)

    c = HG_C
    ri = lax.broadcasted_iota(jnp.int32, (c, c), 0)
    ci = lax.broadcasted_iota(jnp.int32, (c, c), 1)
    tri = (ci <= ri).astype(BF16)
    level_masks = []
    for w in (32, 16, 8):
        tb, sb = ri // w, ci // w
        level_masks.append((tb == sb + 1) & (tb % 2 == 1))
    sub_t = lax.broadcasted_iota(jnp.int32, (c // 8, 8, HEAD_W), 1)
    gain = gain_ref[...]

    def chunk(ic, carry):
        r0 = pl.multiple_of(ic * c, c)
        for h in range(HEADS):
            cs = slice(h * HEAD_W, (h + 1) * HEAD_W)
            lf = lf_ref[pl.ds(r0, c), cs]
            q = q_ref[pl.ds(r0, c), cs].astype(F32)
            k = k_ref[pl.ds(r0, c), cs].astype(F32)
            v_bf = v_ref[pl.ds(r0, c), cs]
            v = v_bf.astype(F32)
            lf_hi = lf.astype(BF16)
            lf_lo = (lf - lf_hi.astype(F32)).astype(BF16)
            b = (jnp.dot(tri, lf_hi, preferred_element_type=F32)
                 + jnp.dot(tri, lf_lo, preferred_element_type=F32))
            b_last = b[c - 1:c, :]

            a = jnp.zeros((c, c), F32)
            for w, msk in zip((32, 16, 8), level_masks):
                nb = c // w
                start = _row_bcast(b, [j * w for j in range(nb)], w)
                nxt = _row_bcast(b, [(j + 1) * w for j in range(nb - 1)] + [c - 1], w)
                qh = (q * jnp.exp(b - start)).astype(BF16)
                kh = (k * jnp.exp(nxt - b)).astype(BF16)
                aw = lax.dot_general(qh, kh, (((1,), (1,)), ((), ())), preferred_element_type=F32)
                a = jnp.where(msk, aw, a)
            o = jnp.dot(a.astype(BF16), v_bf, preferred_element_type=F32)

            b3 = b.reshape(c // 8, 8, HEAD_W)
            q3 = q.reshape(c // 8, 8, HEAD_W)
            k3 = k.reshape(c // 8, 8, HEAD_W)
            v3 = v.reshape(c // 8, 8, HEAD_W)
            od = jnp.zeros((c // 8, 8, HEAD_W), F32)
            for s in range(8):
                keep = sub_t >= s
                dec = jnp.exp(jnp.where(keep, b3 - b3[:, s:s + 1, :], NEG_BIG))
                col = jnp.sum(q3 * dec * k3[:, s:s + 1, :], axis=-1, keepdims=True)
                od = od + col * v3[:, s:s + 1, :]
            o = o + od.reshape(c, HEAD_W)

            st = st_ref[h]
            qi = (q * jnp.exp(b)).astype(BF16)
            o = o + lax.dot_general(qi, st.astype(BF16), (((1,), (1,)), ((), ())),
                                    preferred_element_type=F32)
            ks = (k * jnp.exp(b_last - b)).astype(BF16)
            upd = lax.dot_general(v_bf, ks, (((0,), (0,)), ((), ())), preferred_element_type=F32)
            st_ref[h] = st * jnp.exp(b_last) + upd

            g = g_ref[pl.ds(r0, c), cs].astype(F32)
            o_ref[pl.ds(r0, c), cs] = (_rms(o, gain) * g).astype(BF16)
        return carry

    lax.fori_loop(0, HG_T // c, chunk, 0)


def _hgrn(proj, lf, gain, batch, seq):
    m = proj.shape[0]
    nt = seq // HG_T

    def spec(col):
        return pl.BlockSpec((HG_T, MIX_W), lambda b, t: (b * nt + t, col))

    return pl.pallas_call(
        _hgrn_kernel,
        grid=(batch, nt),
        in_specs=[spec(0), spec(0), spec(1), spec(2), spec(3),
                  pl.BlockSpec((1, HEAD_W), lambda b, t: (0, 0))],
        out_specs=spec(0),
        out_shape=jax.ShapeDtypeStruct((m, MIX_W), BF16),
        scratch_shapes=[pltpu.VMEM((HEADS, HEAD_W, HEAD_W), F32)],
        compiler_params=pltpu.CompilerParams(dimension_semantics=("parallel", "arbitrary"),
                                             vmem_limit_bytes=VMEM_LIMIT),
        name="hgrn2",
    )(proj, lf, proj, proj, proj, gain)


def _attn_kernel(lam_ref, slope_ref, q_ref, k_ref, v_ref, gain_ref, o_ref, q2_ref, m_ref, l_ref, acc_ref):
    t = ATT_T
    h = pl.program_id(1)
    i = pl.program_id(2)
    slope = slope_ref[h]

    q = q_ref[...]
    lane = lax.broadcasted_iota(jnp.int32, (t, HEAD_W), 1)
    q2_ref[0:t, :] = jnp.where(lane < DQK, q, jnp.zeros_like(q))
    q2_ref[t:2 * t, :] = jnp.where(lane >= DQK, q, jnp.zeros_like(q))
    m_ref[...] = jnp.full_like(m_ref, NEG_BIG)
    l_ref[...] = jnp.zeros_like(l_ref)
    acc_ref[...] = jnp.zeros_like(acc_ref)
    col = lax.broadcasted_iota(jnp.int32, (1, t), 1)

    def step(j, masked):
        r0 = pl.multiple_of(j * t, t)
        kt = k_ref[pl.ds(r0, t), :]
        vt = v_ref[pl.ds(r0, t), :]
        s = lax.dot_general(q2_ref[...], kt, (((1,), (1,)), ((), ())), preferred_element_type=F32)
        s = s + slope * (col + j * t).astype(F32)
        if masked:
            rr = lax.broadcasted_iota(jnp.int32, (2 * t, t), 0)
            cc = lax.broadcasted_iota(jnp.int32, (2 * t, t), 1)
            s = jnp.where(cc <= jnp.where(rr >= t, rr - t, rr), s, NEG_BIG)
        m_old = m_ref[...]
        m_new = jnp.maximum(m_old, jnp.max(s, axis=-1, keepdims=True))
        alpha = jnp.exp(m_old - m_new)
        p = jnp.exp(s - m_new)
        l_ref[...] = alpha * l_ref[...] + jnp.sum(p, axis=-1, keepdims=True)
        acc_ref[...] = alpha * acc_ref[...] + jnp.dot(p.astype(BF16), vt, preferred_element_type=F32)
        m_ref[...] = m_new

    def body(j, carry):
        step(j, False)
        return carry

    lax.fori_loop(0, i, body, 0)
    step(i, True)

    lam = lam_ref[0]
    inv_l = 1.0 / l_ref[...]
    on = acc_ref[...] * inv_l
    o = on[0:t, :] - lam * on[t:2 * t, :]
    o_ref[...] = _rms(o, gain_ref[...]).astype(BF16)


def _attn(lam, proj, gain, batch, seq):
    m = proj.shape[0]
    slopes = jnp.exp2(-(ALIBI_MAX_BIAS / HEADS) * jnp.arange(1, HEADS + 1, dtype=F32))
    nq = seq // ATT_T
    qcol, kcol, vcol = 4 * HEADS, 5 * HEADS, 6 * HEADS
    return pl.pallas_call(
        _attn_kernel,
        grid=(batch, HEADS, nq),
        in_specs=[pl.BlockSpec(memory_space=pltpu.SMEM),
                  pl.BlockSpec(memory_space=pltpu.SMEM),
                  pl.BlockSpec((ATT_T, HEAD_W), lambda b, h, i: (b * nq + i, qcol + h)),
                  pl.BlockSpec((seq, HEAD_W), lambda b, h, i: (b, kcol + h)),
                  pl.BlockSpec((seq, HEAD_W), lambda b, h, i: (b, vcol + h)),
                  pl.BlockSpec((1, HEAD_W), lambda b, h, i: (0, 0))],
        out_specs=pl.BlockSpec((ATT_T, HEAD_W), lambda b, h, i: (b * nq + i, h)),
        out_shape=jax.ShapeDtypeStruct((m, MIX_W), BF16),
        scratch_shapes=[pltpu.VMEM((2 * ATT_T, HEAD_W), BF16),
                        pltpu.VMEM((2 * ATT_T, 1), F32),
                        pltpu.VMEM((2 * ATT_T, 1), F32),
                        pltpu.VMEM((2 * ATT_T, HEAD_W), F32)],
        compiler_params=pltpu.CompilerParams(dimension_semantics=("parallel", "parallel", "arbitrary"),
                                             vmem_limit_bytes=VMEM_LIMIT),
        name="diffattn",
    )(lam, slopes, proj, proj, proj, gain)


def _merge_out_kernel(ya_ref, yb_ref, ga0_ref, ga1_ref, gb0_ref, gb1_ref, x_ref,
                      wua_ref, wub_ref, wo_ref, gpost_ref, gffn_ref, x1_ref, h2_ref):
    ua = jnp.dot(ya_ref[...], wua_ref[...], preferred_element_type=F32)
    ub = jnp.dot(yb_ref[...], wub_ref[...], preferred_element_type=F32)
    ga = jnp.concatenate([ga0_ref[...], ga1_ref[...]], axis=1).astype(F32)
    gb = jnp.concatenate([gb0_ref[...], gb1_ref[...]], axis=1).astype(F32)
    merged = (ga * ua + gb * ub).astype(BF16)
    mix = jnp.dot(merged, wo_ref[...], preferred_element_type=F32)
    x1 = x_ref[...] + _rms(mix, gpost_ref[...])
    x1_ref[...] = x1
    h2_ref[...] = _rms(x1, gffn_ref[...]).astype(BF16)


def _resident(shape):
    return pl.BlockSpec(shape, lambda i: (0,) * len(shape), pipeline_mode=pl.Buffered(1))


def _merge_out(ya, yb, proj, x2d, wua, wub, wo, gpost, gffn):
    m = ya.shape[0]
    tm = MO_TM

    def row(width, col=0):
        return pl.BlockSpec((tm, width), lambda i: (i, col))

    return pl.pallas_call(
        _merge_out_kernel,
        grid=(m // tm,),
        in_specs=[row(MIX_W), row(MIX_W), row(MIX_W, 7), row(MIX_W, 8), row(MIX_W, 9), row(MIX_W, 10),
                  row(D_MODEL),
                  _resident((MIX_W, D_MODEL)), _resident((MIX_W, D_MODEL)), _resident((D_MODEL, D_MODEL)),
                  _resident((1, D_MODEL)), _resident((1, D_MODEL))],
        out_specs=[row(D_MODEL), row(D_MODEL)],
        out_shape=[jax.ShapeDtypeStruct((m, D_MODEL), F32),
                   jax.ShapeDtypeStruct((m, D_MODEL), BF16)],
        compiler_params=pltpu.CompilerParams(dimension_semantics=("parallel",),
                                             vmem_limit_bytes=VMEM_LIMIT),
        name="merge_out",
    )(ya, yb, proj, proj, proj, proj, x2d, wua, wub, wo, gpost, gffn)


def _ffn_kernel(h_ref, x_ref, wg_ref, wu_ref, wd_ref, gpost_ref, gnext_ref, x2_ref, hn_ref):
    h = h_ref[...]
    acc = jnp.zeros((FF_TM, D_MODEL), F32)
    for c in range(D_FF // FF_TF):
        cs = slice(c * FF_TF, (c + 1) * FF_TF)
        g = jnp.dot(h, wg_ref[:, cs], preferred_element_type=F32)
        u = jnp.dot(h, wu_ref[:, cs], preferred_element_type=F32)
        ff = (g * _sigmoid(g) * u).astype(BF16)
        acc = acc + jnp.dot(ff, wd_ref[cs, :], preferred_element_type=F32)
    x2 = x_ref[...] + _rms(acc, gpost_ref[...])
    x2_ref[...] = x2
    hn_ref[...] = _rms(x2, gnext_ref[...]).astype(BF16)


def _ffn(h2, x1, wg, wu, wd, gpost, gnext):
    m = h2.shape[0]
    tm = FF_TM

    def row(width):
        return pl.BlockSpec((tm, width), lambda i: (i, 0))

    return pl.pallas_call(
        _ffn_kernel,
        grid=(m // tm,),
        in_specs=[row(D_MODEL), row(D_MODEL),
                  _resident((D_MODEL, D_FF)), _resident((D_MODEL, D_FF)), _resident((D_FF, D_MODEL)),
                  _resident((1, D_MODEL)), _resident((1, D_MODEL))],
        out_specs=[row(D_MODEL), row(D_MODEL)],
        out_shape=[jax.ShapeDtypeStruct((m, D_MODEL), F32),
                   jax.ShapeDtypeStruct((m, D_MODEL), BF16)],
        compiler_params=pltpu.CompilerParams(dimension_semantics=("parallel",),
                                             vmem_limit_bytes=VMEM_LIMIT),
        name="ffn",
    )(h2, x1, wg, wu, wd, gpost, gnext)


def kernel(x, lower_bounds, norm_mix_pre, norm_mix_post, norm_ffn_pre, norm_ffn_post, w_in, hg_out_norm,
           da_subln, lambda_q1, lambda_k1, lambda_q2, lambda_k2, w_up_a, w_up_b, w_out, w_ffn_gate,
           w_ffn_up, w_ffn_down):
    batch, seq, _ = x.shape
    depth = w_in.shape[0]
    m = batch * seq
    x2d = x.reshape(m, D_MODEL).astype(F32)

    lb_all = jnp.cumsum(jax.nn.softmax(lower_bounds.astype(F32), axis=0), axis=0)
    lb_all = lb_all - lb_all[0:1]

    h = _prenorm(x2d, norm_mix_pre[0].reshape(1, D_MODEL))
    for l in range(depth):
        lb = lb_all[l]
        lbp = jnp.zeros((8, MIX_W), F32)
        lbp = lbp.at[0].set(jnp.maximum(jnp.log(lb), NEG_BIG)).at[1].set(jnp.log1p(-lb)).at[2].set(1.0 - lb)
        proj, lf = _in_proj(h, w_in[l].astype(BF16), lbp)

        ya = _hgrn(proj, lf, hg_out_norm[l].reshape(1, HEAD_W), batch, seq)

        lam_init = 0.8 - 0.6 * math.exp(-0.3 * l)
        lam = (jnp.exp(jnp.sum(lambda_q1[l].astype(F32) * lambda_k1[l].astype(F32)))
               - jnp.exp(jnp.sum(lambda_q2[l].astype(F32) * lambda_k2[l].astype(F32))) + lam_init)
        sub_gain = (da_subln[l].astype(F32) * (1.0 - lam_init)).reshape(1, HEAD_W)
        yb = _attn(lam.reshape(1), proj, sub_gain, batch, seq)

        x1, h2 = _merge_out(ya, yb, proj, x2d, w_up_a[l].astype(BF16), w_up_b[l].astype(BF16),
                            w_out[l].astype(BF16), norm_mix_post[l].reshape(1, D_MODEL),
                            norm_ffn_pre[l].reshape(1, D_MODEL))
        g_next = norm_mix_pre[(l + 1) % depth].reshape(1, D_MODEL)
        x2d, h = _ffn(h2, x1, w_ffn_gate[l].astype(BF16), w_ffn_up[l].astype(BF16),
                      w_ffn_down[l].astype(BF16), norm_ffn_post[l].reshape(1, D_MODEL), g_next)
    return x2d.reshape(batch, seq, D_MODEL)
```

```python
import functools
import math

import jax
import jax.numpy as jnp
from jax import lax
from jax.experimental import pallas as pl
from jax.experimental.pallas import tpu as pltpu

F32 = jnp.float32
BF16 = jnp.bfloat16

D_MODEL = 1024
HEADS = 4
HEAD_W = 128
MIX_W = HEADS * HEAD_W
DQK = 64
D_FF = 2816
D_IN = 7 * MIX_W + 2 * D_MODEL
EPS = 1e-6
ALIBI_MAX_BIAS = 8.0
NEG_BIG = -1e30

IN_TM, IN_TN = 1024, 512
HG_T, HG_C = 512, 64
ATT_TQ, ATT_TK = 256, 1024
MO_TM = 512
FF_TM, FF_TF = 512, 256
VMEM_LIMIT = 56 * 1024 * 1024


def _sigmoid(x):
    return 1.0 / (1.0 + jnp.exp(-x))


def _rms(x, gain):
    return x * lax.rsqrt(jnp.mean(x * x, axis=-1, keepdims=True) + EPS) * gain


def _prenorm_kernel(x_ref, g_ref, h_ref):
    h_ref[...] = _rms(x_ref[...], g_ref[...]).astype(BF16)


def _prenorm(x2d, gain):
    m = x2d.shape[0]
    tm = 1024
    return pl.pallas_call(
        _prenorm_kernel,
        grid=(m // tm,),
        in_specs=[pl.BlockSpec((tm, D_MODEL), lambda i: (i, 0)),
                  pl.BlockSpec((1, D_MODEL), lambda i: (0, 0))],
        out_specs=pl.BlockSpec((tm, D_MODEL), lambda i: (i, 0)),
        out_shape=jax.ShapeDtypeStruct((m, D_MODEL), BF16),
        compiler_params=pltpu.CompilerParams(dimension_semantics=("parallel",)),
        name="prenorm",
    )(x2d, gain)


def _in_proj_kernel(h_ref, w_ref, lb_ref, o_ref, lf_ref):
    j = pl.program_id(1)
    acc = jnp.dot(h_ref[...], w_ref[...], preferred_element_type=F32)

    @pl.when((j == 0) | (j == 3))
    def _():
        o_ref[...] = (acc * _sigmoid(acc)).astype(BF16)

    @pl.when(j == 1)
    def _():
        log_lb, log_1m_lb, one_m_lb = lb_ref[0:1, :], lb_ref[1:2, :], lb_ref[2:3, :]
        e = jnp.exp(-jnp.abs(acc))
        log_sig = jnp.minimum(acc, 0.0) - jnp.log(1.0 + e)
        c = log_1m_lb + log_sig
        hi = jnp.maximum(log_lb, c)
        lf_ref[...] = hi + jnp.log(1.0 + jnp.exp(-jnp.abs(log_lb - c)))
        sig_neg = jnp.where(acc >= 0.0, e, 1.0) / (1.0 + e)
        o_ref[...] = (one_m_lb * sig_neg).astype(BF16)

    @pl.when((j == 2) | (j == 5) | (j == 6))
    def _():
        o_ref[...] = acc.astype(BF16)

    @pl.when(j == 4)
    def _():
        o_ref[...] = (acc * (1.0 / math.sqrt(DQK))).astype(BF16)

    @pl.when(j >= 7)
    def _():
        o_ref[...] = _sigmoid(acc).astype(BF16)


def _in_proj(h, w, lbp):
    m = h.shape[0]
    return pl.pallas_call(
        _in_proj_kernel,
        grid=(m // IN_TM, D_IN // IN_TN),
        in_specs=[pl.BlockSpec((IN_TM, D_MODEL), lambda i, j: (i, 0)),
                  pl.BlockSpec((D_MODEL, IN_TN), lambda i, j: (0, j)),
                  pl.BlockSpec((8, MIX_W), lambda i, j: (0, 0))],
        out_specs=[pl.BlockSpec((IN_TM, IN_TN), lambda i, j: (i, j)),
                   pl.BlockSpec((IN_TM, MIX_W), lambda i, j: (i, 0))],
        out_shape=[jax.ShapeDtypeStruct((m, D_IN), BF16),
                   jax.ShapeDtypeStruct((m, MIX_W), F32)],
        compiler_params=pltpu.CompilerParams(dimension_semantics=("parallel", "arbitrary"),
                                             vmem_limit_bytes=VMEM_LIMIT),
        name="in_proj",
    )(h, w, lbp)


def _row_bcast(b, rows, w):
    return jnp.concatenate([jnp.broadcast_to(b[r:r + 1, :], (w, HEAD_W)) for r in rows], axis=0)


def _hgrn_kernel(q_ref, lf_ref, k_ref, v_ref, g_ref, gain_ref, o_ref, st_ref):
    @pl.when(pl.program_id(1) == 0)
    def _():
        st_ref[...] = jnp.zeros_like(st_ref)

    c = HG_C
    ri = lax.broadcasted_iota(jnp.int32, (c, c), 0)
    ci = lax.broadcasted_iota(jnp.int32, (c, c), 1)
    tri = (ci <= ri).astype(BF16)
    level_masks = []
    for w in (32, 16, 8):
        tb, sb = ri // w, ci // w
        level_masks.append((tb == sb + 1) & (tb % 2 == 1))
    sub_t = lax.broadcasted_iota(jnp.int32, (c // 8, 8, HEAD_W), 1)
    gain = gain_ref[...]

    def chunk(ic, carry):
        r0 = pl.multiple_of(ic * c, c)
        for h in range(HEADS):
            cs = slice(h * HEAD_W, (h + 1) * HEAD_W)
            lf = lf_ref[pl.ds(r0, c), cs]
            q = q_ref[pl.ds(r0, c), cs].astype(F32)
            k = k_ref[pl.ds(r0, c), cs].astype(F32)
            v_bf = v_ref[pl.ds(r0, c), cs]
            v = v_bf.astype(F32)
            lf_hi = lf.astype(BF16)
            lf_lo = (lf - lf_hi.astype(F32)).astype(BF16)
            b = (jnp.dot(tri, lf_hi, preferred_element_type=F32)
                 + jnp.dot(tri, lf_lo, preferred_element_type=F32))
            b_last = b[c - 1:c, :]

            a = jnp.zeros((c, c), F32)
            for w, msk in zip((32, 16, 8), level_masks):
                nb = c // w
                start = _row_bcast(b, [j * w for j in range(nb)], w)
                nxt = _row_bcast(b, [(j + 1) * w for j in range(nb - 1)] + [c - 1], w)
                qh = (q * jnp.exp(b - start)).astype(BF16)
                kh = (k * jnp.exp(nxt - b)).astype(BF16)
                aw = lax.dot_general(qh, kh, (((1,), (1,)), ((), ())), preferred_element_type=F32)
                a = jnp.where(msk, aw, a)
            o = jnp.dot(a.astype(BF16), v_bf, preferred_element_type=F32)

            b3 = b.reshape(c // 8, 8, HEAD_W)
            q3 = q.reshape(c // 8, 8, HEAD_W)
            k3 = k.reshape(c // 8, 8, HEAD_W)
            v3 = v.reshape(c // 8, 8, HEAD_W)
            od = jnp.zeros((c // 8, 8, HEAD_W), F32)
            for s in range(8):
                keep = sub_t >= s
                dec = jnp.exp(jnp.where(keep, b3 - b3[:, s:s + 1, :], NEG_BIG))
                col = jnp.sum(q3 * dec * k3[:, s:s + 1, :], axis=-1, keepdims=True)
                od = od + col * v3[:, s:s + 1, :]
            o = o + od.reshape(c, HEAD_W)

            st = st_ref[h]
            qi = (q * jnp.exp(b)).astype(BF16)
            o = o + lax.dot_general(qi, st.astype(BF16), (((1,), (1,)), ((), ())),
                                    preferred_element_type=F32)
            ks = (k * jnp.exp(b_last - b)).astype(BF16)
            upd = lax.dot_general(v_bf, ks, (((0,), (0,)), ((), ())), preferred_element_type=F32)
            st_ref[h] = st * jnp.exp(b_last) + upd

            g = g_ref[pl.ds(r0, c), cs].astype(F32)
            o_ref[pl.ds(r0, c), cs] = (_rms(o, gain) * g).astype(BF16)
        return carry

    lax.fori_loop(0, HG_T // c, chunk, 0)


def _hgrn(proj, lf, gain, batch, seq):
    m = proj.shape[0]
    nt = seq // HG_T

    def spec(col):
        return pl.BlockSpec((HG_T, MIX_W), lambda b, t: (b * nt + t, col))

    return pl.pallas_call(
        _hgrn_kernel,
        grid=(batch, nt),
        in_specs=[spec(0), spec(0), spec(1), spec(2), spec(3),
                  pl.BlockSpec((1, HEAD_W), lambda b, t: (0, 0))],
        out_specs=spec(0),
        out_shape=jax.ShapeDtypeStruct((m, MIX_W), BF16),
        scratch_shapes=[pltpu.VMEM((HEADS, HEAD_W, HEAD_W), F32)],
        compiler_params=pltpu.CompilerParams(dimension_semantics=("parallel", "arbitrary"),
                                             vmem_limit_bytes=VMEM_LIMIT),
        name="hgrn2",
    )(proj, lf, proj, proj, proj, gain)


def _attn_kernel(lam_ref, slope_ref, q_ref, k_ref, v_ref, gain_ref, o_ref, kaug_ref, vt_ref, q2_ref, acc_ref):
    tq, r, tk = ATT_TQ, 2 * ATT_TQ, ATT_TK
    h = pl.program_id(1)
    i = pl.program_id(2)
    slope = slope_ref[h]
    nblk = k_ref.shape[0] // tk

    @pl.when(i == 0)
    def _():
        def fill(jb, carry):
            r0 = pl.multiple_of(jb * tk, tk)
            kaug_ref[pl.ds(r0, tk), 0:HEAD_W] = k_ref[pl.ds(r0, tk), :]
            row = lax.broadcasted_iota(jnp.int32, (tk, HEAD_W), 0) + jb * tk
            lane = lax.broadcasted_iota(jnp.int32, (tk, HEAD_W), 1)
            hi = (row - (row & 127)).astype(F32)
            lo = (row & 127).astype(F32)
            bias = jnp.where(lane == 0, slope * hi, jnp.where(lane == 1, slope * lo, 0.0))
            kaug_ref[pl.ds(r0, tk), HEAD_W:2 * HEAD_W] = bias.astype(BF16)
            vt_ref[jb] = v_ref[pl.ds(r0, tk), :].astype(F32).T.astype(BF16)
            return carry
        lax.fori_loop(0, nblk, fill, 0)

    q = q_ref[...]
    lane = lax.broadcasted_iota(jnp.int32, (tq, HEAD_W), 1)
    zero = jnp.zeros_like(q)
    ones2 = jnp.where(lane < 2, 1.0, 0.0).astype(BF16)
    q2_ref[0:tq, 0:HEAD_W] = jnp.where(lane < DQK, q, zero)
    q2_ref[tq:r, 0:HEAD_W] = jnp.where(lane >= DQK, q, zero)
    q2_ref[0:tq, HEAD_W:2 * HEAD_W] = ones2
    q2_ref[tq:r, HEAD_W:2 * HEAD_W] = ones2
    acc_ref[...] = jnp.zeros_like(acc_ref)

    def step(j, m_old, l_old, masked):
        r0 = pl.multiple_of(j * tk, tk)
        s = lax.dot_general(kaug_ref[pl.ds(r0, tk), :], q2_ref[...], (((1,), (1,)), ((), ())),
                            preferred_element_type=F32)
        if masked:
            kv_pos = lax.broadcasted_iota(jnp.int32, (tk, r), 0) + j * tk
            ql = lax.broadcasted_iota(jnp.int32, (tk, r), 1)
            q_pos = i * tq + jnp.where(ql >= tq, ql - tq, ql)
            s = jnp.where(kv_pos <= q_pos, s, NEG_BIG)
        m_new = jnp.maximum(m_old, jnp.max(s, axis=0, keepdims=True))
        alpha = jnp.exp(m_old - m_new)
        p = jnp.exp(s - m_new)
        l_new = alpha * l_old + jnp.sum(p, axis=0, keepdims=True)
        pv = jnp.dot(vt_ref[j], p.astype(BF16), preferred_element_type=F32)
        acc_ref[...] = alpha * acc_ref[...] + pv
        return m_new, l_new

    def body(j, carry):
        return step(j, carry[0], carry[1], False)

    n_full = (i * tq) // tk
    m0 = jnp.full((1, r), NEG_BIG, F32)
    l0 = jnp.zeros((1, r), F32)
    m1, l1 = lax.fori_loop(0, n_full, body, (m0, l0))
    _, l2 = step(n_full, m1, l1, True)

    lam = lam_ref[0]
    on = acc_ref[...] * (1.0 / l2)
    ot = on[:, 0:tq] - lam * on[:, tq:r]
    ot = ot * lax.rsqrt(jnp.mean(ot * ot, axis=0, keepdims=True) + EPS)
    o_ref[...] = (ot.T * gain_ref[...]).astype(BF16)


def _attn(lam, proj, gain, batch, seq):
    m = proj.shape[0]
    slopes = jnp.exp2(-(ALIBI_MAX_BIAS / HEADS) * jnp.arange(1, HEADS + 1, dtype=F32))
    nq = seq // ATT_TQ
    qcol, kcol, vcol = 4 * HEADS, 5 * HEADS, 6 * HEADS
    return pl.pallas_call(
        _attn_kernel,
        grid=(batch, HEADS, nq),
        in_specs=[pl.BlockSpec(memory_space=pltpu.SMEM),
                  pl.BlockSpec(memory_space=pltpu.SMEM),
                  pl.BlockSpec((ATT_TQ, HEAD_W), lambda b, h, i: (b * nq + i, qcol + h)),
                  pl.BlockSpec((seq, HEAD_W), lambda b, h, i: (b, kcol + h)),
                  pl.BlockSpec((seq, HEAD_W), lambda b, h, i: (b, vcol + h)),
                  pl.BlockSpec((1, HEAD_W), lambda b, h, i: (0, 0))],
        out_specs=pl.BlockSpec((ATT_TQ, HEAD_W), lambda b, h, i: (b * nq + i, h)),
        out_shape=jax.ShapeDtypeStruct((m, MIX_W), BF16),
        scratch_shapes=[pltpu.VMEM((seq, 2 * HEAD_W), BF16),
                        pltpu.VMEM((seq // ATT_TK, HEAD_W, ATT_TK), BF16),
                        pltpu.VMEM((2 * ATT_TQ, 2 * HEAD_W), BF16),
                        pltpu.VMEM((HEAD_W, 2 * ATT_TQ), F32)],
        compiler_params=pltpu.CompilerParams(dimension_semantics=("parallel", "parallel", "arbitrary"),
                                             vmem_limit_bytes=VMEM_LIMIT),
        name="diffattn",
    )(lam, slopes, proj, proj, proj, gain)


def _merge_out_kernel(ya_ref, yb_ref, ga0_ref, ga1_ref, gb0_ref, gb1_ref, x_ref,
                      wua_ref, wub_ref, wo_ref, gpost_ref, gffn_ref, x1_ref, h2_ref):
    ua = jnp.dot(ya_ref[...], wua_ref[...], preferred_element_type=F32)
    ub = jnp.dot(yb_ref[...], wub_ref[...], preferred_element_type=F32)
    ga = jnp.concatenate([ga0_ref[...], ga1_ref[...]], axis=1).astype(F32)
    gb = jnp.concatenate([gb0_ref[...], gb1_ref[...]], axis=1).astype(F32)
    merged = (ga * ua + gb * ub).astype(BF16)
    mix = jnp.dot(merged, wo_ref[...], preferred_element_type=F32)
    x1 = x_ref[...] + _rms(mix, gpost_ref[...])
    x1_ref[...] = x1
    h2_ref[...] = _rms(x1, gffn_ref[...]).astype(BF16)


def _resident(shape):
    return pl.BlockSpec(shape, lambda i: (0,) * len(shape), pipeline_mode=pl.Buffered(1))


def _merge_out(ya, yb, proj, x2d, wua, wub, wo, gpost, gffn):
    m = ya.shape[0]
    tm = MO_TM

    def row(width, col=0):
        return pl.BlockSpec((tm, width), lambda i: (i, col))

    return pl.pallas_call(
        _merge_out_kernel,
        grid=(m // tm,),
        in_specs=[row(MIX_W), row(MIX_W), row(MIX_W, 7), row(MIX_W, 8), row(MIX_W, 9), row(MIX_W, 10),
                  row(D_MODEL),
                  _resident((MIX_W, D_MODEL)), _resident((MIX_W, D_MODEL)), _resident((D_MODEL, D_MODEL)),
                  _resident((1, D_MODEL)), _resident((1, D_MODEL))],
        out_specs=[row(D_MODEL), row(D_MODEL)],
        out_shape=[jax.ShapeDtypeStruct((m, D_MODEL), F32),
                   jax.ShapeDtypeStruct((m, D_MODEL), BF16)],
        compiler_params=pltpu.CompilerParams(dimension_semantics=("parallel",),
                                             vmem_limit_bytes=VMEM_LIMIT),
        name="merge_out",
    )(ya, yb, proj, proj, proj, proj, x2d, wua, wub, wo, gpost, gffn)


def _ffn_kernel(h_ref, x_ref, wg_ref, wu_ref, wd_ref, gpost_ref, gnext_ref, x2_ref, hn_ref):
    h = h_ref[...]
    acc = jnp.zeros((FF_TM, D_MODEL), F32)
    for c in range(D_FF // FF_TF):
        cs = slice(c * FF_TF, (c + 1) * FF_TF)
        g = jnp.dot(h, wg_ref[:, cs], preferred_element_type=F32)
        u = jnp.dot(h, wu_ref[:, cs], preferred_element_type=F32)
        ff = (g * _sigmoid(g) * u).astype(BF16)
        acc = acc + jnp.dot(ff, wd_ref[cs, :], preferred_element_type=F32)
    x2 = x_ref[...] + _rms(acc, gpost_ref[...])
    x2_ref[...] = x2
    hn_ref[...] = _rms(x2, gnext_ref[...]).astype(BF16)


def _ffn(h2, x1, wg, wu, wd, gpost, gnext):
    m = h2.shape[0]
    tm = FF_TM

    def row(width):
        return pl.BlockSpec((tm, width), lambda i: (i, 0))

    return pl.pallas_call(
        _ffn_kernel,
        grid=(m // tm,),
        in_specs=[row(D_MODEL), row(D_MODEL),
                  _resident((D_MODEL, D_FF)), _resident((D_MODEL, D_FF)), _resident((D_FF, D_MODEL)),
                  _resident((1, D_MODEL)), _resident((1, D_MODEL))],
        out_specs=[row(D_MODEL), row(D_MODEL)],
        out_shape=[jax.ShapeDtypeStruct((m, D_MODEL), F32),
                   jax.ShapeDtypeStruct((m, D_MODEL), BF16)],
        compiler_params=pltpu.CompilerParams(dimension_semantics=("parallel",),
                                             vmem_limit_bytes=VMEM_LIMIT),
        name="ffn",
    )(h2, x1, wg, wu, wd, gpost, gnext)


def kernel(x, lower_bounds, norm_mix_pre, norm_mix_post, norm_ffn_pre, norm_ffn_post, w_in, hg_out_norm,
           da_subln, lambda_q1, lambda_k1, lambda_q2, lambda_k2, w_up_a, w_up_b, w_out, w_ffn_gate,
           w_ffn_up, w_ffn_down):
    batch, seq, _ = x.shape
    depth = w_in.shape[0]
    m = batch * seq
    x2d = x.reshape(m, D_MODEL).astype(F32)

    lb_all = jnp.cumsum(jax.nn.softmax(lower_bounds.astype(F32), axis=0), axis=0)
    lb_all = lb_all - lb_all[0:1]

    h = _prenorm(x2d, norm_mix_pre[0].reshape(1, D_MODEL))
    for l in range(depth):
        lb = lb_all[l]
        lbp = jnp.zeros((8, MIX_W), F32)
        lbp = lbp.at[0].set(jnp.maximum(jnp.log(lb), NEG_BIG)).at[1].set(jnp.log1p(-lb)).at[2].set(1.0 - lb)
        proj, lf = _in_proj(h, w_in[l].astype(BF16), lbp)

        ya = _hgrn(proj, lf, hg_out_norm[l].reshape(1, HEAD_W), batch, seq)

        lam_init = 0.8 - 0.6 * math.exp(-0.3 * l)
        lam = (jnp.exp(jnp.sum(lambda_q1[l].astype(F32) * lambda_k1[l].astype(F32)))
               - jnp.exp(jnp.sum(lambda_q2[l].astype(F32) * lambda_k2[l].astype(F32))) + lam_init)
        sub_gain = (da_subln[l].astype(F32) * (1.0 - lam_init)).reshape(1, HEAD_W)
        yb = _attn(lam.reshape(1), proj, sub_gain, batch, seq)

        x1, h2 = _merge_out(ya, yb, proj, x2d, w_up_a[l].astype(BF16), w_up_b[l].astype(BF16),
                            w_out[l].astype(BF16), norm_mix_post[l].reshape(1, D_MODEL),
                            norm_ffn_pre[l].reshape(1, D_MODEL))
        g_next = norm_mix_pre[(l + 1) % depth].reshape(1, D_MODEL)
        x2d, h = _ffn(h2, x1, w_ffn_gate[l].astype(BF16), w_ffn_up[l].astype(BF16),
                      w_ffn_down[l].astype(BF16), norm_ffn_post[l].reshape(1, D_MODEL), g_next)
    return x2d.reshape(batch, seq, D_MODEL)
```

```python
import functools
import math

import jax
import jax.numpy as jnp
from jax import lax
from jax.experimental import pallas as pl
from jax.experimental.pallas import tpu as pltpu

F32 = jnp.float32
BF16 = jnp.bfloat16

D_MODEL = 1024
HEADS = 4
HEAD_W = 128
MIX_W = HEADS * HEAD_W
DQK = 64
D_FF = 2816
D_IN = 7 * MIX_W + 2 * D_MODEL
EPS = 1e-6
ALIBI_MAX_BIAS = 8.0
NEG_BIG = -1e30
LOG2E = math.log2(math.e)

IN_TM, IN_TN, IN_SUB = 1024, 512, 256
HG_T, HG_C = 512, 64
ATT_TQ, ATT_TK = 512, 512
MO_TM = 512
FF_TM, FF_TF = 512, 256
VMEM_LIMIT = 56 * 1024 * 1024


def _sigmoid(x):
    return 1.0 / (1.0 + jnp.exp(-x))


def _rms(x, gain):
    return x * lax.rsqrt(jnp.mean(x * x, axis=-1, keepdims=True) + EPS) * gain


def _prenorm_kernel(x_ref, g_ref, h_ref):
    h_ref[...] = _rms(x_ref[...], g_ref[...]).astype(BF16)


def _prenorm(x2d, gain):
    m = x2d.shape[0]
    tm = 1024
    return pl.pallas_call(
        _prenorm_kernel,
        grid=(m // tm,),
        in_specs=[pl.BlockSpec((tm, D_MODEL), lambda i: (i, 0)),
                  pl.BlockSpec((1, D_MODEL), lambda i: (0, 0))],
        out_specs=pl.BlockSpec((tm, D_MODEL), lambda i: (i, 0)),
        out_shape=jax.ShapeDtypeStruct((m, D_MODEL), BF16),
        compiler_params=pltpu.CompilerParams(dimension_semantics=("parallel",)),
        name="prenorm",
    )(x2d, gain)


def _in_proj_kernel(h_ref, w_ref, lb_ref, o_ref, lf_ref):
    j = pl.program_id(1)

    def run(epilogue):
        for r0 in range(0, IN_TM, IN_SUB):
            rs = slice(r0, r0 + IN_SUB)
            epilogue(rs, jnp.dot(h_ref[rs, :], w_ref[...], preferred_element_type=F32))

    def silu(rs, acc):
        o_ref[rs, :] = (acc * _sigmoid(acc)).astype(BF16)

    def forget_gate(rs, acc):
        log_lb, log_1m_lb, one_m_lb = lb_ref[0:1, :], lb_ref[1:2, :], lb_ref[2:3, :]
        e = jnp.exp(-jnp.abs(acc))
        log_sig = jnp.minimum(acc, 0.0) - jnp.log(1.0 + e)
        c = log_1m_lb + log_sig
        hi = jnp.maximum(log_lb, c)
        lf_ref[rs, :] = hi + jnp.log(1.0 + jnp.exp(-jnp.abs(log_lb - c)))
        sig_neg = jnp.where(acc >= 0.0, e, 1.0) / (1.0 + e)
        o_ref[rs, :] = (one_m_lb * sig_neg).astype(BF16)

    def identity(rs, acc):
        o_ref[rs, :] = acc.astype(BF16)

    def query_scale(rs, acc):
        o_ref[rs, :] = (acc * (LOG2E / math.sqrt(DQK))).astype(BF16)

    def gate(rs, acc):
        o_ref[rs, :] = _sigmoid(acc).astype(BF16)

    pl.when((j == 0) | (j == 3))(lambda: run(silu))
    pl.when(j == 1)(lambda: run(forget_gate))
    pl.when((j == 2) | (j == 5) | (j == 6))(lambda: run(identity))
    pl.when(j == 4)(lambda: run(query_scale))
    pl.when(j >= 7)(lambda: run(gate))


def _in_proj(h, w, lbp):
    m = h.shape[0]
    return pl.pallas_call(
        _in_proj_kernel,
        grid=(m // IN_TM, D_IN // IN_TN),
        in_specs=[pl.BlockSpec((IN_TM, D_MODEL), lambda i, j: (i, 0)),
                  pl.BlockSpec((D_MODEL, IN_TN), lambda i, j: (0, j)),
                  pl.BlockSpec((8, MIX_W), lambda i, j: (0, 0))],
        out_specs=[pl.BlockSpec((IN_TM, IN_TN), lambda i, j: (i, j)),
                   pl.BlockSpec((IN_TM, MIX_W), lambda i, j: (i, 0))],
        out_shape=[jax.ShapeDtypeStruct((m, D_IN), BF16),
                   jax.ShapeDtypeStruct((m, MIX_W), F32)],
        compiler_params=pltpu.CompilerParams(dimension_semantics=("parallel", "arbitrary"),
                                             vmem_limit_bytes=VMEM_LIMIT),
        name="in_proj",
    )(h, w, lbp)


def _row_bcast(b, rows, w):
    return jnp.concatenate([jnp.broadcast_to(b[r:r + 1, :], (w, HEAD_W)) for r in rows], axis=0)


def _hgrn_kernel(q_ref, lf_ref, k_ref, v_ref, g_ref, gain_ref, o_ref, st_ref):
    @pl.when(pl.program_id(1) == 0)
    def _():
        st_ref[...] = jnp.zeros_like(st_ref)

    c = HG_C
    ri = lax.broadcasted_iota(jnp.int32, (c, c), 0)
    ci = lax.broadcasted_iota(jnp.int32, (c, c), 1)
    tri = (ci <= ri).astype(BF16)
    level_masks = []
    for w in (32, 16, 8):
        tb, sb = ri // w, ci // w
        level_masks.append((tb == sb + 1) & (tb % 2 == 1))
    sub_t = lax.broadcasted_iota(jnp.int32, (c // 8, 8, HEAD_W), 1)
    gain = gain_ref[...]

    def chunk(ic, carry):
        r0 = pl.multiple_of(ic * c, c)
        for h in range(HEADS):
            cs = slice(h * HEAD_W, (h + 1) * HEAD_W)
            lf = lf_ref[pl.ds(r0, c), cs]
            q = q_ref[pl.ds(r0, c), cs].astype(F32)
            k = k_ref[pl.ds(r0, c), cs].astype(F32)
            v_bf = v_ref[pl.ds(r0, c), cs]
            v = v_bf.astype(F32)
            lf_hi = lf.astype(BF16)
            lf_lo = (lf - lf_hi.astype(F32)).astype(BF16)
            b = (jnp.dot(tri, lf_hi, preferred_element_type=F32)
                 + jnp.dot(tri, lf_lo, preferred_element_type=F32))
            b_last = b[c - 1:c, :]

            a = jnp.zeros((c, c), F32)
            for w, msk in zip((32, 16, 8), level_masks):
                nb = c // w
                start = _row_bcast(b, [j * w for j in range(nb)], w)
                nxt = _row_bcast(b, [(j + 1) * w for j in range(nb - 1)] + [c - 1], w)
                qh = (q * jnp.exp(b - start)).astype(BF16)
                kh = (k * jnp.exp(nxt - b)).astype(BF16)
                aw = lax.dot_general(qh, kh, (((1,), (1,)), ((), ())), preferred_element_type=F32)
                a = jnp.where(msk, aw, a)
            o = jnp.dot(a.astype(BF16), v_bf, preferred_element_type=F32)

            b3 = b.reshape(c // 8, 8, HEAD_W)
            q3 = q.reshape(c // 8, 8, HEAD_W)
            k3 = k.reshape(c // 8, 8, HEAD_W)
            v3 = v.reshape(c // 8, 8, HEAD_W)
            od = jnp.zeros((c // 8, 8, HEAD_W), F32)
            for s in range(8):
                keep = sub_t >= s
                dec = jnp.exp(jnp.where(keep, b3 - b3[:, s:s + 1, :], NEG_BIG))
                col = jnp.sum(q3 * dec * k3[:, s:s + 1, :], axis=-1, keepdims=True)
                od = od + col * v3[:, s:s + 1, :]
            o = o + od.reshape(c, HEAD_W)

            st = st_ref[h]
            qi = (q * jnp.exp(b)).astype(BF16)
            o = o + lax.dot_general(qi, st.astype(BF16), (((1,), (1,)), ((), ())),
                                    preferred_element_type=F32)
            ks = (k * jnp.exp(b_last - b)).astype(BF16)
            upd = lax.dot_general(v_bf, ks, (((0,), (0,)), ((), ())), preferred_element_type=F32)
            st_ref[h] = st * jnp.exp(b_last) + upd

            g = g_ref[pl.ds(r0, c), cs].astype(F32)
            o_ref[pl.ds(r0, c), cs] = (_rms(o, gain) * g).astype(BF16)
        return carry

    lax.fori_loop(0, HG_T // c, chunk, 0)


def _hgrn(proj, lf, gain, batch, seq):
    m = proj.shape[0]
    nt = seq // HG_T

    def spec(col):
        return pl.BlockSpec((HG_T, MIX_W), lambda b, t: (b * nt + t, col))

    return pl.pallas_call(
        _hgrn_kernel,
        grid=(batch, nt),
        in_specs=[spec(0), spec(0), spec(1), spec(2), spec(3),
                  pl.BlockSpec((1, HEAD_W), lambda b, t: (0, 0))],
        out_specs=spec(0),
        out_shape=jax.ShapeDtypeStruct((m, MIX_W), BF16),
        scratch_shapes=[pltpu.VMEM((HEADS, HEAD_W, HEAD_W), F32)],
        compiler_params=pltpu.CompilerParams(dimension_semantics=("parallel", "arbitrary"),
                                             vmem_limit_bytes=VMEM_LIMIT),
        name="hgrn2",
    )(proj, lf, proj, proj, proj, gain)


def _attn_kernel(lam_ref, slope_ref, q_ref, k_ref, v_ref, gain_ref, o_ref,
                 kaug_ref, vt_ref, q2t_ref, acc_ref, sa_ref, sb_ref, pa_ref, pb_ref):
    tq, r, tk = ATT_TQ, 2 * ATT_TQ, ATT_TK
    h = pl.program_id(1)
    i = pl.program_id(2)
    slope = slope_ref[h] * LOG2E
    nblk = k_ref.shape[0] // tk

    @pl.when(i == 0)
    def _():
        def fill(jb, carry):
            r0 = pl.multiple_of(jb * tk, tk)
            kaug_ref[pl.ds(r0, tk), 0:HEAD_W] = k_ref[pl.ds(r0, tk), :]
            row = lax.broadcasted_iota(jnp.int32, (tk, HEAD_W), 0) + jb * tk
            lane = lax.broadcasted_iota(jnp.int32, (tk, HEAD_W), 1)
            c = slope * row.astype(F32)
            c1 = c.astype(BF16).astype(F32)
            c2 = (c - c1).astype(BF16).astype(F32)
            c3 = c - c1 - c2
            cols = jnp.where(lane == 0, c1, jnp.where(lane == 1, c2, jnp.where(lane == 2, c3, 0.0)))
            kaug_ref[pl.ds(r0, tk), HEAD_W:2 * HEAD_W] = cols.astype(BF16)
            vt_ref[jb] = v_ref[pl.ds(r0, tk), :].astype(F32).T.astype(BF16)
            return carry
        lax.fori_loop(0, nblk, fill, 0)

    qt = q_ref[...].astype(F32).T
    sub = lax.broadcasted_iota(jnp.int32, (HEAD_W, tq), 0)
    q2t_ref[0:HEAD_W, 0:tq] = jnp.where(sub < DQK, qt, 0.0).astype(BF16)
    q2t_ref[0:HEAD_W, tq:r] = jnp.where(sub >= DQK, qt, 0.0).astype(BF16)
    sub_r = lax.broadcasted_iota(jnp.int32, (HEAD_W, r), 0)
    q2t_ref[HEAD_W:2 * HEAD_W, :] = jnp.where(sub_r < 3, 1.0, 0.0).astype(BF16)
    acc_ref[...] = jnp.zeros_like(acc_ref)
    pb_ref[...] = jnp.zeros_like(pb_ref)

    def scores(j):
        r0 = pl.multiple_of(j * tk, tk)
        return jnp.dot(kaug_ref[pl.ds(r0, tk), :], q2t_ref[...], preferred_element_type=F32)

    def softmax(s, m_old, l_old):
        m_new = jnp.maximum(m_old, jnp.max(s, axis=0, keepdims=True))
        alpha = jnp.exp2(m_old - m_new)
        p = jnp.exp2(s - m_new)
        l_new = alpha * l_old + jnp.sum(p, axis=0, keepdims=True)
        return p.astype(BF16), alpha, m_new, l_new

    def add_values(jv, p, alpha):
        acc_ref[...] = alpha * acc_ref[...] + jnp.dot(vt_ref[jv], p, preferred_element_type=F32)

    def stage(j, s_in, s_out, p_in, p_out, m_old, l_old, alpha_prev):
        s_out[...] = scores(j + 1)
        p, alpha, m_new, l_new = softmax(s_in[...], m_old, l_old)
        add_values(jnp.maximum(j - 1, 0), p_in[...], alpha_prev)
        p_out[...] = p
        return m_new, l_new, alpha

    def pair(t, carry):
        m_a, l_a, al_a = stage(2 * t, sa_ref, sb_ref, pb_ref, pa_ref, *carry)
        return stage(2 * t + 1, sb_ref, sa_ref, pa_ref, pb_ref, m_a, l_a, al_a)

    def finish(j_last, s_in, p_in, m_old, l_old, alpha_prev):
        kv_pos = lax.broadcasted_iota(jnp.int32, (tk, r), 0) + j_last * tk
        ql = lax.broadcasted_iota(jnp.int32, (tk, r), 1)
        q_pos = i * tq + jnp.where(ql >= tq, ql - tq, ql)
        s_last = jnp.where(kv_pos <= q_pos, s_in[...], NEG_BIG)
        p, alpha, _, l_fin = softmax(s_last, m_old, l_old)
        add_values(jnp.maximum(j_last - 1, 0), p_in[...], alpha_prev)
        add_values(j_last, p, alpha)
        on = acc_ref[...] * (1.0 / l_fin)
        ot = on[:, 0:tq] - lam_ref[0] * on[:, tq:r]
        ot = ot * lax.rsqrt(jnp.mean(ot * ot, axis=0, keepdims=True) + EPS)
        o_ref[...] = (ot.T * gain_ref[...]).astype(BF16)

    n_full = (i * tq) // tk
    sa_ref[...] = scores(0)
    init = (jnp.full((1, r), NEG_BIG, F32), jnp.zeros((1, r), F32), jnp.ones((1, r), F32))
    m1, l1, al1 = lax.fori_loop(0, n_full // 2, pair, init)

    @pl.when(n_full % 2 == 0)
    def _():
        finish(n_full, sa_ref, pb_ref, m1, l1, al1)

    @pl.when(n_full % 2 == 1)
    def _():
        m2, l2, al2 = stage(n_full - 1, sa_ref, sb_ref, pb_ref, pa_ref, m1, l1, al1)
        finish(n_full, sb_ref, pa_ref, m2, l2, al2)


def _attn(lam, proj, gain, batch, seq):
    m = proj.shape[0]
    slopes = jnp.exp2(-(ALIBI_MAX_BIAS / HEADS) * jnp.arange(1, HEADS + 1, dtype=F32))
    nq = seq // ATT_TQ
    qcol, kcol, vcol = 4 * HEADS, 5 * HEADS, 6 * HEADS
    return pl.pallas_call(
        _attn_kernel,
        grid=(batch, HEADS, nq),
        in_specs=[pl.BlockSpec(memory_space=pltpu.SMEM),
                  pl.BlockSpec(memory_space=pltpu.SMEM),
                  pl.BlockSpec((ATT_TQ, HEAD_W), lambda b, h, i: (b * nq + i, qcol + h)),
                  pl.BlockSpec((seq, HEAD_W), lambda b, h, i: (b, kcol + h)),
                  pl.BlockSpec((seq, HEAD_W), lambda b, h, i: (b, vcol + h)),
                  pl.BlockSpec((1, HEAD_W), lambda b, h, i: (0, 0))],
        out_specs=pl.BlockSpec((ATT_TQ, HEAD_W), lambda b, h, i: (b * nq + i, h)),
        out_shape=jax.ShapeDtypeStruct((m, MIX_W), BF16),
        scratch_shapes=[pltpu.VMEM((seq, 2 * HEAD_W), BF16),
                        pltpu.VMEM((seq // ATT_TK, HEAD_W, ATT_TK), BF16),
                        pltpu.VMEM((2 * HEAD_W, 2 * ATT_TQ), BF16),
                        pltpu.VMEM((HEAD_W, 2 * ATT_TQ), F32),
                        pltpu.VMEM((ATT_TK, 2 * ATT_TQ), F32),
                        pltpu.VMEM((ATT_TK, 2 * ATT_TQ), F32),
                        pltpu.VMEM((ATT_TK, 2 * ATT_TQ), BF16),
                        pltpu.VMEM((ATT_TK, 2 * ATT_TQ), BF16)],
        compiler_params=pltpu.CompilerParams(dimension_semantics=("parallel", "parallel", "arbitrary"),
                                             vmem_limit_bytes=VMEM_LIMIT),
        name="diffattn",
    )(lam, slopes, proj, proj, proj, gain)


def _merge_out_kernel(ya_ref, yb_ref, ga0_ref, ga1_ref, gb0_ref, gb1_ref, x_ref,
                      wua_ref, wub_ref, wo_ref, gpost_ref, gffn_ref, x1_ref, h2_ref):
    ua = jnp.dot(ya_ref[...], wua_ref[...], preferred_element_type=F32)
    ub = jnp.dot(yb_ref[...], wub_ref[...], preferred_element_type=F32)
    ga = jnp.concatenate([ga0_ref[...], ga1_ref[...]], axis=1).astype(F32)
    gb = jnp.concatenate([gb0_ref[...], gb1_ref[...]], axis=1).astype(F32)
    merged = (ga * ua + gb * ub).astype(BF16)
    mix = jnp.dot(merged, wo_ref[...], preferred_element_type=F32)
    x1 = x_ref[...] + _rms(mix, gpost_ref[...])
    x1_ref[...] = x1
    h2_ref[...] = _rms(x1, gffn_ref[...]).astype(BF16)


def _resident(shape):
    return pl.BlockSpec(shape, lambda i: (0,) * len(shape), pipeline_mode=pl.Buffered(1))


def _merge_out(ya, yb, proj, x2d, wua, wub, wo, gpost, gffn):
    m = ya.shape[0]
    tm = MO_TM

    def row(width, col=0):
        return pl.BlockSpec((tm, width), lambda i: (i, col))

    return pl.pallas_call(
        _merge_out_kernel,
        grid=(m // tm,),
        in_specs=[row(MIX_W), row(MIX_W), row(MIX_W, 7), row(MIX_W, 8), row(MIX_W, 9), row(MIX_W, 10),
                  row(D_MODEL),
                  _resident((MIX_W, D_MODEL)), _resident((MIX_W, D_MODEL)), _resident((D_MODEL, D_MODEL)),
                  _resident((1, D_MODEL)), _resident((1, D_MODEL))],
        out_specs=[row(D_MODEL), row(D_MODEL)],
        out_shape=[jax.ShapeDtypeStruct((m, D_MODEL), F32),
                   jax.ShapeDtypeStruct((m, D_MODEL), BF16)],
        compiler_params=pltpu.CompilerParams(dimension_semantics=("parallel",),
                                             vmem_limit_bytes=VMEM_LIMIT),
        name="merge_out",
    )(ya, yb, proj, proj, proj, proj, x2d, wua, wub, wo, gpost, gffn)


def _ffn_kernel(h_ref, x_ref, wg_ref, wu_ref, wd_ref, gpost_ref, gnext_ref, x2_ref, hn_ref):
    h = h_ref[...]
    acc = jnp.zeros((FF_TM, D_MODEL), F32)
    for c in range(D_FF // FF_TF):
        cs = slice(c * FF_TF, (c + 1) * FF_TF)
        g = jnp.dot(h, wg_ref[:, cs], preferred_element_type=F32)
        u = jnp.dot(h, wu_ref[:, cs], preferred_element_type=F32)
        ff = (g * _sigmoid(g) * u).astype(BF16)
        acc = acc + jnp.dot(ff, wd_ref[cs, :], preferred_element_type=F32)
    x2 = x_ref[...] + _rms(acc, gpost_ref[...])
    x2_ref[...] = x2
    hn_ref[...] = _rms(x2, gnext_ref[...]).astype(BF16)


def _ffn(h2, x1, wg, wu, wd, gpost, gnext):
    m = h2.shape[0]
    tm = FF_TM

    def row(width):
        return pl.BlockSpec((tm, width), lambda i: (i, 0))

    return pl.pallas_call(
        _ffn_kernel,
        grid=(m // tm,),
        in_specs=[row(D_MODEL), row(D_MODEL),
                  _resident((D_MODEL, D_FF)), _resident((D_MODEL, D_FF)), _resident((D_FF, D_MODEL)),
                  _resident((1, D_MODEL)), _resident((1, D_MODEL))],
        out_specs=[row(D_MODEL), row(D_MODEL)],
        out_shape=[jax.ShapeDtypeStruct((m, D_MODEL), F32),
                   jax.ShapeDtypeStruct((m, D_MODEL), BF16)],
        compiler_params=pltpu.CompilerParams(dimension_semantics=("parallel",),
                                             vmem_limit_bytes=VMEM_LIMIT),
        name="ffn",
    )(h2, x1, wg, wu, wd, gpost, gnext)


def kernel(x, lower_bounds, norm_mix_pre, norm_mix_post, norm_ffn_pre, norm_ffn_post, w_in, hg_out_norm,
           da_subln, lambda_q1, lambda_k1, lambda_q2, lambda_k2, w_up_a, w_up_b, w_out, w_ffn_gate,
           w_ffn_up, w_ffn_down):
    batch, seq, _ = x.shape
    depth = w_in.shape[0]
    m = batch * seq
    x2d = x.reshape(m, D_MODEL).astype(F32)

    lb_all = jnp.cumsum(jax.nn.softmax(lower_bounds.astype(F32), axis=0), axis=0)
    lb_all = lb_all - lb_all[0:1]

    h = _prenorm(x2d, norm_mix_pre[0].reshape(1, D_MODEL))
    for l in range(depth):
        lb = lb_all[l]
        lbp = jnp.zeros((8, MIX_W), F32)
        lbp = lbp.at[0].set(jnp.maximum(jnp.log(lb), NEG_BIG)).at[1].set(jnp.log1p(-lb)).at[2].set(1.0 - lb)
        proj, lf = _in_proj(h, w_in[l].astype(BF16), lbp)

        ya = _hgrn(proj, lf, hg_out_norm[l].reshape(1, HEAD_W), batch, seq)

        lam_init = 0.8 - 0.6 * math.exp(-0.3 * l)
        lam = (jnp.exp(jnp.sum(lambda_q1[l].astype(F32) * lambda_k1[l].astype(F32)))
               - jnp.exp(jnp.sum(lambda_q2[l].astype(F32) * lambda_k2[l].astype(F32))) + lam_init)
        sub_gain = (da_subln[l].astype(F32) * (1.0 - lam_init)).reshape(1, HEAD_W)
        yb = _attn(lam.reshape(1), proj, sub_gain, batch, seq)

        x1, h2 = _merge_out(ya, yb, proj, x2d, w_up_a[l].astype(BF16), w_up_b[l].astype(BF16),
                            w_out[l].astype(BF16), norm_mix_post[l].reshape(1, D_MODEL),
                            norm_ffn_pre[l].reshape(1, D_MODEL))
        g_next = norm_mix_pre[(l + 1) % depth].reshape(1, D_MODEL)
        x2d, h = _ffn(h2, x1, w_ffn_gate[l].astype(BF16), w_ffn_up[l].astype(BF16),
                      w_ffn_down[l].astype(BF16), norm_ffn_post[l].reshape(1, D_MODEL), g_next)
    return x2d.reshape(batch, seq, D_MODEL)
```

```python
import functools
import math

import jax
import jax.numpy as jnp
from jax import lax
from jax.experimental import pallas as pl
from jax.experimental.pallas import tpu as pltpu

F32 = jnp.float32
BF16 = jnp.bfloat16

D_MODEL = 1024
HEADS = 4
HEAD_W = 128
MIX_W = HEADS * HEAD_W
DQK = 64
D_FF = 2816
D_IN = 7 * MIX_W + 2 * D_MODEL
EPS = 1e-6
ALIBI_MAX_BIAS = 8.0
NEG_BIG = -1e30
LOG2E = math.log2(math.e)

IN_TM, IN_TN = 512, 512
HG_T, HG_C = 512, 64
ATT_TQ, ATT_TK = 512, 512
MO_TM = 512
FF_TM, FF_TF = 512, 256
VMEM_LIMIT = 56 * 1024 * 1024


def _sigmoid(x):
    return 1.0 / (1.0 + jnp.exp(-x))


def _rms(x, gain):
    return x * lax.rsqrt(jnp.mean(x * x, axis=-1, keepdims=True) + EPS) * gain


def _resident(shape):
    return pl.BlockSpec(shape, lambda *_: (0,) * len(shape), pipeline_mode=pl.Buffered(1))


def _in_proj_kernel(x_ref, g_ref, w_ref, lb_ref, o_ref, lf_ref, h_ref, *, prenorm):
    if prenorm:
        h_ref[...] = _rms(x_ref[...], g_ref[...]).astype(BF16)
    else:
        h_ref[...] = x_ref[...]

    def silu(acc):
        return acc * _sigmoid(acc)

    def forget_gate(acc):
        log_lb, log_1m_lb, one_m_lb = lb_ref[0:1, :], lb_ref[1:2, :], lb_ref[2:3, :]
        e = jnp.exp(-jnp.abs(acc))
        log_sig = jnp.minimum(acc, 0.0) - jnp.log(1.0 + e)
        c = log_1m_lb + log_sig
        hi = jnp.maximum(log_lb, c)
        lf_ref[...] = hi + jnp.log(1.0 + jnp.exp(-jnp.abs(log_lb - c)))
        return one_m_lb * (jnp.where(acc >= 0.0, e, 1.0) / (1.0 + e))

    def query_scale(acc):
        return acc * (LOG2E / math.sqrt(DQK))

    epilogues = [silu, forget_gate, None, silu, query_scale, None, None] + [_sigmoid] * 4
    for jt, epilogue in enumerate(epilogues):
        cs = slice(jt * IN_TN, (jt + 1) * IN_TN)
        acc = jnp.dot(h_ref[...], w_ref[:, cs], preferred_element_type=F32)
        o_ref[:, cs] = (acc if epilogue is None else epilogue(acc)).astype(BF16)


def _in_proj(x_or_h, gain, w, lbp, prenorm):
    m = x_or_h.shape[0]
    return pl.pallas_call(
        functools.partial(_in_proj_kernel, prenorm=prenorm),
        grid=(m // IN_TM,),
        in_specs=[pl.BlockSpec((IN_TM, D_MODEL), lambda i: (i, 0)),
                  _resident((1, D_MODEL)), _resident((D_MODEL, D_IN)), _resident((8, MIX_W))],
        out_specs=[pl.BlockSpec((IN_TM, D_IN), lambda i: (i, 0)),
                   pl.BlockSpec((IN_TM, MIX_W), lambda i: (i, 0))],
        out_shape=[jax.ShapeDtypeStruct((m, D_IN), BF16),
                   jax.ShapeDtypeStruct((m, MIX_W), F32)],
        scratch_shapes=[pltpu.VMEM((IN_TM, D_MODEL), BF16)],
        compiler_params=pltpu.CompilerParams(dimension_semantics=("parallel",),
                                             vmem_limit_bytes=VMEM_LIMIT),
        name="in_proj",
    )(x_or_h, gain, w, lbp)


def _row_bcast(b, rows, w):
    return jnp.concatenate([jnp.broadcast_to(b[r:r + 1, :], (w, HEAD_W)) for r in rows], axis=0)


def _hgrn_kernel(q_ref, lf_ref, k_ref, v_ref, g_ref, gain_ref, o_ref, st_ref):
    @pl.when(pl.program_id(1) == 0)
    def _():
        st_ref[...] = jnp.zeros_like(st_ref)

    c = HG_C
    ri = lax.broadcasted_iota(jnp.int32, (c, c), 0)
    ci = lax.broadcasted_iota(jnp.int32, (c, c), 1)
    tri = (ci <= ri).astype(BF16)
    level_masks = []
    for w in (32, 16, 8):
        tb, sb = ri // w, ci // w
        level_masks.append((tb == sb + 1) & (tb % 2 == 1))
    diag_masks = [(ci == (ri // 8) * 8 + s) & (ri % 8 >= s) for s in range(8)]
    ones_sq = jnp.ones((HEAD_W, HEAD_W), BF16)
    gain = gain_ref[...]

    def chunk(ic, carry):
        r0 = pl.multiple_of(ic * c, c)
        heads = [slice(h * HEAD_W, (h + 1) * HEAD_W) for h in range(HEADS)]
        nt_dims = (((1,), (1,)), ((), ()))
        q = [q_ref[pl.ds(r0, c), cs].astype(F32) for cs in heads]
        k = [k_ref[pl.ds(r0, c), cs].astype(F32) for cs in heads]
        v_bf = [v_ref[pl.ds(r0, c), cs] for cs in heads]

        b = []
        for cs in heads:
            lf = lf_ref[pl.ds(r0, c), cs]
            lf_hi = lf.astype(BF16)
            lf_lo = (lf - lf_hi.astype(F32)).astype(BF16)
            b.append(jnp.dot(tri, lf_hi, preferred_element_type=F32)
                     + jnp.dot(tri, lf_lo, preferred_element_type=F32))
        b_last = [bh[c - 1:c, :] for bh in b]

        upd = []
        for h in range(HEADS):
            ks = (k[h] * jnp.exp(b_last[h] - b[h])).astype(BF16)
            upd.append(lax.dot_general(v_bf[h], ks, (((0,), (0,)), ((), ())), preferred_element_type=F32))
        a = [jnp.zeros((c, c), F32) for _ in heads]
        for w, msk in zip((32, 16, 8), level_masks):
            nb = c // w
            for h in range(HEADS):
                start = _row_bcast(b[h], [j * w for j in range(nb)], w)
                nxt = _row_bcast(b[h], [(j + 1) * w for j in range(nb - 1)] + [c - 1], w)
                qh = (q[h] * jnp.exp(b[h] - start)).astype(BF16)
                kh = (k[h] * jnp.exp(nxt - b[h])).astype(BF16)
                aw = lax.dot_general(qh, kh, nt_dims, preferred_element_type=F32)
                a[h] = jnp.where(msk, aw, a[h])

        sums = []
        for h in range(HEADS):
            b3 = b[h].reshape(c // 8, 8, HEAD_W)
            q3 = q[h].reshape(c // 8, 8, HEAD_W)
            k3 = k[h].reshape(c // 8, 8, HEAD_W)
            terms = []
            for s in range(8):
                dec = jnp.exp(jnp.minimum(b3 - b3[:, s:s + 1, :], 0.0))
                terms.append((q3 * dec * k3[:, s:s + 1, :]).reshape(c, HEAD_W).astype(BF16))
            sums.append(jnp.dot(jnp.concatenate(terms, axis=0), ones_sq, preferred_element_type=F32))

        o = []
        for h in range(HEADS):
            st = st_ref[h]
            qi = (q[h] * jnp.exp(b[h])).astype(BF16)
            o.append(lax.dot_general(qi, st.astype(BF16), nt_dims, preferred_element_type=F32))
            st_ref[h] = st * jnp.exp(b_last[h]) + upd[h]
        for h, cs in enumerate(heads):
            for s in range(8):
                a[h] = jnp.where(diag_masks[s], sums[h][s * c:(s + 1) * c, 0:c], a[h])
            oh = o[h] + jnp.dot(a[h].astype(BF16), v_bf[h], preferred_element_type=F32)
            g = g_ref[pl.ds(r0, c), cs].astype(F32)
            o_ref[pl.ds(r0, c), cs] = (_rms(oh, gain) * g).astype(BF16)
        return carry

    lax.fori_loop(0, HG_T // c, chunk, 0, unroll=4)


def _hgrn(proj, lf, gain, batch, seq):
    m = proj.shape[0]
    nt = seq // HG_T

    def spec(col):
        return pl.BlockSpec((HG_T, MIX_W), lambda b, t: (b * nt + t, col))

    return pl.pallas_call(
        _hgrn_kernel,
        grid=(batch, nt),
        in_specs=[spec(0), spec(0), spec(1), spec(2), spec(3),
                  pl.BlockSpec((1, HEAD_W), lambda b, t: (0, 0))],
        out_specs=spec(0),
        out_shape=jax.ShapeDtypeStruct((m, MIX_W), BF16),
        scratch_shapes=[pltpu.VMEM((HEADS, HEAD_W, HEAD_W), F32)],
        compiler_params=pltpu.CompilerParams(dimension_semantics=("parallel", "arbitrary"),
                                             vmem_limit_bytes=VMEM_LIMIT),
        name="hgrn2",
    )(proj, lf, proj, proj, proj, gain)


def _attn_kernel(lam_ref, slope_ref, q_ref, k_ref, v_ref, gain_ref, o_ref,
                 kaug_ref, vt_ref, q2t_ref, acc_ref, sa_ref, sb_ref, pa_ref, pb_ref):
    tq, r, tk = ATT_TQ, 2 * ATT_TQ, ATT_TK
    h = pl.program_id(1)
    i = pl.program_id(2)
    slope = slope_ref[h] * LOG2E
    nblk = k_ref.shape[0] // tk

    @pl.when(i == 0)
    def _():
        def fill(jb, carry):
            r0 = pl.multiple_of(jb * tk, tk)
            kaug_ref[pl.ds(r0, tk), 0:HEAD_W] = k_ref[pl.ds(r0, tk), :]
            row = lax.broadcasted_iota(jnp.int32, (tk, HEAD_W), 0) + jb * tk
            lane = lax.broadcasted_iota(jnp.int32, (tk, HEAD_W), 1)
            c = slope * row.astype(F32)
            c1 = c.astype(BF16).astype(F32)
            c2 = (c - c1).astype(BF16).astype(F32)
            c3 = c - c1 - c2
            cols = jnp.where(lane == 0, c1, jnp.where(lane == 1, c2, jnp.where(lane == 2, c3, 0.0)))
            kaug_ref[pl.ds(r0, tk), HEAD_W:2 * HEAD_W] = cols.astype(BF16)
            vt_ref[jb] = v_ref[pl.ds(r0, tk), :].astype(F32).T.astype(BF16)
            return carry
        lax.fori_loop(0, nblk, fill, 0)

    qt = q_ref[...].astype(F32).T
    sub = lax.broadcasted_iota(jnp.int32, (HEAD_W, tq), 0)
    q2t_ref[0:HEAD_W, 0:tq] = jnp.where(sub < DQK, qt, 0.0).astype(BF16)
    q2t_ref[0:HEAD_W, tq:r] = jnp.where(sub >= DQK, qt, 0.0).astype(BF16)
    sub_r = lax.broadcasted_iota(jnp.int32, (HEAD_W, r), 0)
    q2t_ref[HEAD_W:2 * HEAD_W, :] = jnp.where(sub_r < 3, 1.0, 0.0).astype(BF16)
    acc_ref[...] = jnp.zeros_like(acc_ref)
    pb_ref[...] = jnp.zeros_like(pb_ref)

    def scores(j):
        r0 = pl.multiple_of(j * tk, tk)
        return jnp.dot(kaug_ref[pl.ds(r0, tk), :], q2t_ref[...], preferred_element_type=F32)

    def softmax(s, bmax, m_old, l_old):
        m_new = jnp.maximum(m_old, bmax)
        alpha = jnp.exp2(m_old - m_new)
        p = jnp.exp2(s - m_new)
        l_new = alpha * l_old + jnp.sum(p, axis=0, keepdims=True)
        return p.astype(BF16), alpha, m_new, l_new

    def add_values(jv, p, alpha):
        acc_ref[...] = alpha * acc_ref[...] + jnp.dot(vt_ref[jv], p, preferred_element_type=F32)

    def stage(j, s_in, s_out, p_in, p_out, m_old, l_old, alpha_prev, bmax):
        s_next = scores(j + 1)
        s_out[...] = s_next
        bmax_next = jnp.max(s_next, axis=0, keepdims=True)
        p, alpha, m_new, l_new = softmax(s_in[...], bmax, m_old, l_old)
        add_values(jnp.maximum(j - 1, 0), p_in[...], alpha_prev)
        p_out[...] = p
        return m_new, l_new, alpha, bmax_next

    def pair(t, carry):
        mid = stage(2 * t, sa_ref, sb_ref, pb_ref, pa_ref, *carry)
        return stage(2 * t + 1, sb_ref, sa_ref, pa_ref, pb_ref, *mid)

    def finish(j_last, s_in, p_in, m_old, l_old, alpha_prev):
        kv_pos = lax.broadcasted_iota(jnp.int32, (tk, r), 0) + j_last * tk
        ql = lax.broadcasted_iota(jnp.int32, (tk, r), 1)
        q_pos = i * tq + jnp.where(ql >= tq, ql - tq, ql)
        s_last = jnp.where(kv_pos <= q_pos, s_in[...], NEG_BIG)
        p, alpha, _, l_fin = softmax(s_last, jnp.max(s_last, axis=0, keepdims=True), m_old, l_old)
        add_values(jnp.maximum(j_last - 1, 0), p_in[...], alpha_prev)
        add_values(j_last, p, alpha)
        on = acc_ref[...] * (1.0 / l_fin)
        ot = on[:, 0:tq] - lam_ref[0] * on[:, tq:r]
        ot = ot * lax.rsqrt(jnp.mean(ot * ot, axis=0, keepdims=True) + EPS)
        o_ref[...] = (ot.T * gain_ref[...]).astype(BF16)

    n_full = (i * tq) // tk
    s0 = scores(0)
    sa_ref[...] = s0
    init = (jnp.full((1, r), NEG_BIG, F32), jnp.zeros((1, r), F32), jnp.ones((1, r), F32),
            jnp.max(s0, axis=0, keepdims=True))
    m1, l1, al1, bmax1 = lax.fori_loop(0, n_full // 2, pair, init)

    @pl.when(n_full % 2 == 0)
    def _():
        finish(n_full, sa_ref, pb_ref, m1, l1, al1)

    @pl.when(n_full % 2 == 1)
    def _():
        m2, l2, al2, _ = stage(n_full - 1, sa_ref, sb_ref, pb_ref, pa_ref, m1, l1, al1, bmax1)
        finish(n_full, sb_ref, pa_ref, m2, l2, al2)


def _attn(lam, proj, gain, batch, seq):
    m = proj.shape[0]
    slopes = jnp.exp2(-(ALIBI_MAX_BIAS / HEADS) * jnp.arange(1, HEADS + 1, dtype=F32))
    nq = seq // ATT_TQ
    qcol, kcol, vcol = 4 * HEADS, 5 * HEADS, 6 * HEADS
    return pl.pallas_call(
        _attn_kernel,
        grid=(batch, HEADS, nq),
        in_specs=[pl.BlockSpec(memory_space=pltpu.SMEM),
                  pl.BlockSpec(memory_space=pltpu.SMEM),
                  pl.BlockSpec((ATT_TQ, HEAD_W), lambda b, h, i: (b * nq + i, qcol + h)),
                  pl.BlockSpec((seq, HEAD_W), lambda b, h, i: (b, kcol + h)),
                  pl.BlockSpec((seq, HEAD_W), lambda b, h, i: (b, vcol + h)),
                  pl.BlockSpec((1, HEAD_W), lambda b, h, i: (0, 0))],
        out_specs=pl.BlockSpec((ATT_TQ, HEAD_W), lambda b, h, i: (b * nq + i, h)),
        out_shape=jax.ShapeDtypeStruct((m, MIX_W), BF16),
        scratch_shapes=[pltpu.VMEM((seq, 2 * HEAD_W), BF16),
                        pltpu.VMEM((seq // ATT_TK, HEAD_W, ATT_TK), BF16),
                        pltpu.VMEM((2 * HEAD_W, 2 * ATT_TQ), BF16),
                        pltpu.VMEM((HEAD_W, 2 * ATT_TQ), F32),
                        pltpu.VMEM((ATT_TK, 2 * ATT_TQ), F32),
                        pltpu.VMEM((ATT_TK, 2 * ATT_TQ), F32),
                        pltpu.VMEM((ATT_TK, 2 * ATT_TQ), BF16),
                        pltpu.VMEM((ATT_TK, 2 * ATT_TQ), BF16)],
        compiler_params=pltpu.CompilerParams(dimension_semantics=("parallel", "parallel", "arbitrary"),
                                             vmem_limit_bytes=VMEM_LIMIT),
        name="diffattn",
    )(lam, slopes, proj, proj, proj, gain)


def _merge_out_kernel(ya_ref, yb_ref, ga0_ref, ga1_ref, gb0_ref, gb1_ref, x_ref,
                      wua_ref, wub_ref, wo_ref, gpost_ref, gffn_ref, x1_ref, h2_ref):
    ua = jnp.dot(ya_ref[...], wua_ref[...], preferred_element_type=F32)
    ub = jnp.dot(yb_ref[...], wub_ref[...], preferred_element_type=F32)
    ga = jnp.concatenate([ga0_ref[...], ga1_ref[...]], axis=1).astype(F32)
    gb = jnp.concatenate([gb0_ref[...], gb1_ref[...]], axis=1).astype(F32)
    merged = (ga * ua + gb * ub).astype(BF16)
    mix = jnp.dot(merged, wo_ref[...], preferred_element_type=F32)
    x1 = x_ref[...] + _rms(mix, gpost_ref[...])
    x1_ref[...] = x1
    h2_ref[...] = _rms(x1, gffn_ref[...]).astype(BF16)


def _merge_out(ya, yb, proj, x2d, wua, wub, wo, gpost, gffn):
    m = ya.shape[0]
    tm = MO_TM

    def row(width, col=0):
        return pl.BlockSpec((tm, width), lambda i: (i, col))

    return pl.pallas_call(
        _merge_out_kernel,
        grid=(m // tm,),
        in_specs=[row(MIX_W), row(MIX_W), row(MIX_W, 7), row(MIX_W, 8), row(MIX_W, 9), row(MIX_W, 10),
                  row(D_MODEL),
                  _resident((MIX_W, D_MODEL)), _resident((MIX_W, D_MODEL)), _resident((D_MODEL, D_MODEL)),
                  _resident((1, D_MODEL)), _resident((1, D_MODEL))],
        out_specs=[row(D_MODEL), row(D_MODEL)],
        out_shape=[jax.ShapeDtypeStruct((m, D_MODEL), F32),
                   jax.ShapeDtypeStruct((m, D_MODEL), BF16)],
        compiler_params=pltpu.CompilerParams(dimension_semantics=("parallel",),
                                             vmem_limit_bytes=VMEM_LIMIT),
        name="merge_out",
    )(ya, yb, proj, proj, proj, proj, x2d, wua, wub, wo, gpost, gffn)


def _ffn_kernel(h_ref, x_ref, wg_ref, wu_ref, wd_ref, gpost_ref, gnext_ref, x2_ref, *maybe_hn_ref):
    h = h_ref[...]
    acc = jnp.zeros((FF_TM, D_MODEL), F32)
    for c in range(D_FF // FF_TF):
        cs = slice(c * FF_TF, (c + 1) * FF_TF)
        g = jnp.dot(h, wg_ref[:, cs], preferred_element_type=F32)
        u = jnp.dot(h, wu_ref[:, cs], preferred_element_type=F32)
        ff = (g * _sigmoid(g) * u).astype(BF16)
        acc = acc + jnp.dot(ff, wd_ref[cs, :], preferred_element_type=F32)
    x2 = x_ref[...] + _rms(acc, gpost_ref[...])
    x2_ref[...] = x2
    for hn_ref in maybe_hn_ref:
        hn_ref[...] = _rms(x2, gnext_ref[...]).astype(BF16)


def _ffn(h2, x1, wg, wu, wd, gpost, gnext, emit_next):
    m = h2.shape[0]
    tm = FF_TM

    def row(width):
        return pl.BlockSpec((tm, width), lambda i: (i, 0))

    out_specs = [row(D_MODEL)] + ([row(D_MODEL)] if emit_next else [])
    out_shape = ([jax.ShapeDtypeStruct((m, D_MODEL), F32)]
                 + ([jax.ShapeDtypeStruct((m, D_MODEL), BF16)] if emit_next else []))
    return pl.pallas_call(
        _ffn_kernel,
        grid=(m // tm,),
        in_specs=[row(D_MODEL), row(D_MODEL),
                  _resident((D_MODEL, D_FF)), _resident((D_MODEL, D_FF)), _resident((D_FF, D_MODEL)),
                  _resident((1, D_MODEL)), _resident((1, D_MODEL))],
        out_specs=out_specs,
        out_shape=out_shape,
        compiler_params=pltpu.CompilerParams(dimension_semantics=("parallel",),
                                             vmem_limit_bytes=VMEM_LIMIT),
        name="ffn",
    )(h2, x1, wg, wu, wd, gpost, gnext)


def kernel(x, lower_bounds, norm_mix_pre, norm_mix_post, norm_ffn_pre, norm_ffn_post, w_in, hg_out_norm,
           da_subln, lambda_q1, lambda_k1, lambda_q2, lambda_k2, w_up_a, w_up_b, w_out, w_ffn_gate,
           w_ffn_up, w_ffn_down):
    batch, seq, _ = x.shape
    depth = w_in.shape[0]
    m = batch * seq
    x2d = x.reshape(m, D_MODEL).astype(F32)

    lb_all = jnp.cumsum(jax.nn.softmax(lower_bounds.astype(F32), axis=0), axis=0)
    lb_all = lb_all - lb_all[0:1]

    h = None
    for l in range(depth):
        lb = lb_all[l]
        lbp = jnp.zeros((8, MIX_W), F32)
        lbp = lbp.at[0].set(jnp.maximum(jnp.log(lb), NEG_BIG)).at[1].set(jnp.log1p(-lb)).at[2].set(1.0 - lb)
        proj, lf = _in_proj(x2d if h is None else h, norm_mix_pre[l].reshape(1, D_MODEL),
                            w_in[l].astype(BF16), lbp, prenorm=h is None)

        ya = _hgrn(proj, lf, hg_out_norm[l].reshape(1, HEAD_W), batch, seq)

        lam_init = 0.8 - 0.6 * math.exp(-0.3 * l)
        lam = (jnp.exp(jnp.sum(lambda_q1[l].astype(F32) * lambda_k1[l].astype(F32)))
               - jnp.exp(jnp.sum(lambda_q2[l].astype(F32) * lambda_k2[l].astype(F32))) + lam_init)
        sub_gain = (da_subln[l].astype(F32) * (1.0 - lam_init)).reshape(1, HEAD_W)
        yb = _attn(lam.reshape(1), proj, sub_gain, batch, seq)

        x1, h2 = _merge_out(ya, yb, proj, x2d, w_up_a[l].astype(BF16), w_up_b[l].astype(BF16),
                            w_out[l].astype(BF16), norm_mix_post[l].reshape(1, D_MODEL),
                            norm_ffn_pre[l].reshape(1, D_MODEL))
        last = l == depth - 1
        g_next = norm_mix_pre[(l + 1) % depth].reshape(1, D_MODEL)
        outs = _ffn(h2, x1, w_ffn_gate[l].astype(BF16), w_ffn_up[l].astype(BF16),
                    w_ffn_down[l].astype(BF16), norm_ffn_post[l].reshape(1, D_MODEL), g_next,
                    emit_next=not last)
        x2d, h = (outs[0], None) if last else outs
    return x2d.reshape(batch, seq, D_MODEL)
```

```python
import functools
import math

import jax
import jax.numpy as jnp
from jax import lax
from jax.experimental import pallas as pl
from jax.experimental.pallas import tpu as pltpu

F32 = jnp.float32
BF16 = jnp.bfloat16

D_MODEL = 1024
HEADS = 4
HEAD_W = 128
MIX_W = HEADS * HEAD_W
DQK = 64
D_FF = 2816
D_IN = 7 * MIX_W + 2 * D_MODEL
EPS = 1e-6
ALIBI_MAX_BIAS = 8.0
NEG_BIG = -1e30
LOG2E = math.log2(math.e)

IN_TM, IN_TN = 512, 512
HG_T, HG_C = 512, 64
ATT_TQ, ATT_TK = 512, 512
CH_TM, FF_TF = 512, 256
VMEM_LIMIT = 56 * 1024 * 1024


def _sigmoid(x):
    return 1.0 / (1.0 + jnp.exp(-x))


def _rms(x, gain):
    return x * lax.rsqrt(jnp.mean(x * x, axis=-1, keepdims=True) + EPS) * gain


def _resident(shape):
    return pl.BlockSpec(shape, lambda *_: (0,) * len(shape), pipeline_mode=pl.Buffered(1))


PROJ_TILES = 9
PROJ_W = PROJ_TILES * 512
KB_TILE, GATE_TILE = 4, 5


def _in_proj_kernel(x_ref, g_ref, w_ref, lb_ref, o_ref, lf_ref, qt_ref, vt_ref, h_ref, *, prenorm):
    if prenorm:
        h_ref[...] = _rms(x_ref[...], g_ref[...]).astype(BF16)
    else:
        h_ref[...] = x_ref[...]

    def silu(acc):
        return acc * _sigmoid(acc)

    def forget_gate(acc):
        log_lb, log_1m_lb, one_m_lb = lb_ref[0:1, :], lb_ref[1:2, :], lb_ref[2:3, :]
        e = jnp.exp(-jnp.abs(acc))
        log_sig = jnp.minimum(acc, 0.0) - jnp.log(1.0 + e)
        c = log_1m_lb + log_sig
        hi = jnp.maximum(log_lb, c)
        lf_ref[...] = hi + jnp.log(1.0 + jnp.exp(-jnp.abs(log_lb - c)))
        return one_m_lb * (jnp.where(acc >= 0.0, e, 1.0) / (1.0 + e))

    def query_scale(acc):
        return acc * (LOG2E / math.sqrt(DQK))

    def column_tile(jt):
        return jnp.dot(h_ref[...], w_ref[:, jt * IN_TN:(jt + 1) * IN_TN], preferred_element_type=F32)

    plan = [(0, silu, 0), (1, forget_gate, 1), (2, None, 2), (3, silu, 3), (5, None, KB_TILE)]
    plan += [(7 + g, _sigmoid, GATE_TILE + g) for g in range(4)]
    for jt, epilogue, out_tile in plan[:4]:
        acc = column_tile(jt)
        o_ref[:, out_tile * IN_TN:(out_tile + 1) * IN_TN] = (
            acc if epilogue is None else epilogue(acc)).astype(BF16)
    qt_ref[0] = query_scale(column_tile(4)).T.astype(BF16)
    vt_ref[0] = column_tile(6).T.astype(BF16)
    for jt, epilogue, out_tile in plan[4:]:
        acc = column_tile(jt)
        o_ref[:, out_tile * IN_TN:(out_tile + 1) * IN_TN] = (
            acc if epilogue is None else epilogue(acc)).astype(BF16)


def _in_proj(x_or_h, gain, w, lbp, prenorm):
    m = x_or_h.shape[0]
    nt = m // IN_TM
    return pl.pallas_call(
        functools.partial(_in_proj_kernel, prenorm=prenorm),
        grid=(nt,),
        in_specs=[pl.BlockSpec((IN_TM, D_MODEL), lambda i: (i, 0)),
                  _resident((1, D_MODEL)), _resident((D_MODEL, D_IN)), _resident((8, MIX_W))],
        out_specs=[pl.BlockSpec((IN_TM, PROJ_W), lambda i: (i, 0)),
                   pl.BlockSpec((IN_TM, MIX_W), lambda i: (i, 0)),
                   pl.BlockSpec((1, MIX_W, IN_TM), lambda i: (i, 0, 0)),
                   pl.BlockSpec((1, MIX_W, IN_TM), lambda i: (i, 0, 0))],
        out_shape=[jax.ShapeDtypeStruct((m, PROJ_W), BF16),
                   jax.ShapeDtypeStruct((m, MIX_W), F32),
                   jax.ShapeDtypeStruct((nt, MIX_W, IN_TM), BF16),
                   jax.ShapeDtypeStruct((nt, MIX_W, IN_TM), BF16)],
        scratch_shapes=[pltpu.VMEM((IN_TM, D_MODEL), BF16)],
        compiler_params=pltpu.CompilerParams(dimension_semantics=("parallel",),
                                             vmem_limit_bytes=VMEM_LIMIT),
        name="in_proj",
    )(x_or_h, gain, w, lbp)


def _row_bcast(b, rows, w):
    return jnp.concatenate([jnp.broadcast_to(b[r:r + 1, :], (w, HEAD_W)) for r in rows], axis=0)


def _hgrn_kernel(q_ref, lf_ref, k_ref, v_ref, g_ref, gain_ref, o_ref, st_ref):
    @pl.when(pl.program_id(1) == 0)
    def _():
        st_ref[...] = jnp.zeros_like(st_ref)

    c = HG_C
    ri = lax.broadcasted_iota(jnp.int32, (c, c), 0)
    ci = lax.broadcasted_iota(jnp.int32, (c, c), 1)
    tri = (ci <= ri).astype(BF16)
    level_masks = []
    for w in (32, 16, 8):
        tb, sb = ri // w, ci // w
        level_masks.append((tb == sb + 1) & (tb % 2 == 1))
    diag_masks = [(ci == (ri // 8) * 8 + s) & (ri % 8 >= s) for s in range(8)]
    ones_sq = jnp.ones((HEAD_W, HEAD_W), BF16)
    gain = gain_ref[...]

    def chunk(ic, carry):
        r0 = pl.multiple_of(ic * c, c)
        heads = [slice(h * HEAD_W, (h + 1) * HEAD_W) for h in range(HEADS)]
        nt_dims = (((1,), (1,)), ((), ()))
        q = [q_ref[pl.ds(r0, c), cs].astype(F32) for cs in heads]
        k = [k_ref[pl.ds(r0, c), cs].astype(F32) for cs in heads]
        v_bf = [v_ref[pl.ds(r0, c), cs] for cs in heads]

        b = []
        for cs in heads:
            lf = lf_ref[pl.ds(r0, c), cs]
            lf_hi = lf.astype(BF16)
            lf_lo = (lf - lf_hi.astype(F32)).astype(BF16)
            b.append(jnp.dot(tri, lf_hi, preferred_element_type=F32)
                     + jnp.dot(tri, lf_lo, preferred_element_type=F32))
        b_last = [bh[c - 1:c, :] for bh in b]

        upd = []
        for h in range(HEADS):
            ks = (k[h] * jnp.exp(b_last[h] - b[h])).astype(BF16)
            upd.append(lax.dot_general(v_bf[h], ks, (((0,), (0,)), ((), ())), preferred_element_type=F32))
        a = [jnp.zeros((c, c), F32) for _ in heads]
        for w, msk in zip((32, 16, 8), level_masks):
            nb = c // w
            for h in range(HEADS):
                start = _row_bcast(b[h], [j * w for j in range(nb)], w)
                nxt = _row_bcast(b[h], [(j + 1) * w for j in range(nb - 1)] + [c - 1], w)
                qh = (q[h] * jnp.exp(b[h] - start)).astype(BF16)
                kh = (k[h] * jnp.exp(nxt - b[h])).astype(BF16)
                aw = lax.dot_general(qh, kh, nt_dims, preferred_element_type=F32)
                a[h] = jnp.where(msk, aw, a[h])

        sums = []
        for h in range(HEADS):
            b3 = b[h].reshape(c // 8, 8, HEAD_W)
            q3 = q[h].reshape(c // 8, 8, HEAD_W)
            k3 = k[h].reshape(c // 8, 8, HEAD_W)
            terms = []
            for s in range(8):
                dec = jnp.exp(jnp.minimum(b3 - b3[:, s:s + 1, :], 0.0))
                terms.append((q3 * dec * k3[:, s:s + 1, :]).reshape(c, HEAD_W).astype(BF16))
            sums.append(jnp.dot(jnp.concatenate(terms, axis=0), ones_sq, preferred_element_type=F32))

        o = []
        for h in range(HEADS):
            st = st_ref[h]
            qi = (q[h] * jnp.exp(b[h])).astype(BF16)
            o.append(lax.dot_general(qi, st.astype(BF16), nt_dims, preferred_element_type=F32))
            st_ref[h] = st * jnp.exp(b_last[h]) + upd[h]
        for h, cs in enumerate(heads):
            for s in range(8):
                a[h] = jnp.where(diag_masks[s], sums[h][s * c:(s + 1) * c, 0:c], a[h])
            oh = o[h] + jnp.dot(a[h].astype(BF16), v_bf[h], preferred_element_type=F32)
            g = g_ref[pl.ds(r0, c), cs].astype(F32)
            o_ref[pl.ds(r0, c), cs] = (_rms(oh, gain) * g).astype(BF16)
        return carry

    lax.fori_loop(0, HG_T // c, chunk, 0, unroll=4)


def _hgrn(proj, lf, gain, batch, seq):
    m = proj.shape[0]
    nt = seq // HG_T

    def spec(col):
        return pl.BlockSpec((HG_T, MIX_W), lambda b, t: (b * nt + t, col))

    return pl.pallas_call(
        _hgrn_kernel,
        grid=(batch, nt),
        in_specs=[spec(0), spec(0), spec(1), spec(2), spec(3),
                  pl.BlockSpec((1, HEAD_W), lambda b, t: (0, 0))],
        out_specs=spec(0),
        out_shape=jax.ShapeDtypeStruct((m, MIX_W), BF16),
        scratch_shapes=[pltpu.VMEM((HEADS, HEAD_W, HEAD_W), F32)],
        compiler_params=pltpu.CompilerParams(dimension_semantics=("parallel", "arbitrary"),
                                             vmem_limit_bytes=VMEM_LIMIT),
        name="hgrn2",
    )(proj, lf, proj, proj, proj, gain)


def _attn_kernel(lam_ref, slope_ref, qt_ref, k_ref, vt_ref, gain_ref, o_ref,
                 kaug_ref, q2t_ref, acc_ref, sa_ref, sb_ref, pa_ref, pb_ref, stat_ref):
    tq, r, tk = ATT_TQ, 2 * ATT_TQ, ATT_TK
    h = pl.program_id(1)
    i = pl.program_id(2)
    slope = slope_ref[h] * LOG2E
    nblk = k_ref.shape[0] // tk

    @pl.when(i == 0)
    def _():
        def fill(jb, carry):
            r0 = pl.multiple_of(jb * tk, tk)
            kaug_ref[pl.ds(r0, tk), 0:HEAD_W] = k_ref[pl.ds(r0, tk), :]
            row = lax.broadcasted_iota(jnp.int32, (tk, HEAD_W), 0) + jb * tk
            lane = lax.broadcasted_iota(jnp.int32, (tk, HEAD_W), 1)
            c = slope * row.astype(F32)
            c1 = c.astype(BF16).astype(F32)
            c2 = (c - c1).astype(BF16).astype(F32)
            c3 = c - c1 - c2
            cols = jnp.where(lane == 0, c1, jnp.where(lane == 1, c2, jnp.where(lane == 2, c3, 0.0)))
            kaug_ref[pl.ds(r0, tk), HEAD_W:2 * HEAD_W] = cols.astype(BF16)
            return carry
        lax.fori_loop(0, nblk, fill, 0)

    qt = qt_ref[0].astype(F32)
    sub = lax.broadcasted_iota(jnp.int32, (HEAD_W, tq), 0)
    q2t_ref[0:HEAD_W, 0:tq] = jnp.where(sub < DQK, qt, 0.0).astype(BF16)
    q2t_ref[0:HEAD_W, tq:r] = jnp.where(sub >= DQK, qt, 0.0).astype(BF16)
    sub_r = lax.broadcasted_iota(jnp.int32, (HEAD_W, r), 0)
    q2t_ref[HEAD_W:2 * HEAD_W, :] = jnp.where(sub_r < 3, 1.0, 0.0).astype(BF16)
    acc_ref[...] = jnp.zeros_like(acc_ref)
    pb_ref[...] = jnp.zeros_like(pb_ref)

    def scores(j):
        r0 = pl.multiple_of(j * tk, tk)
        return jnp.dot(kaug_ref[pl.ds(r0, tk), :], q2t_ref[...], preferred_element_type=F32)

    slab = 64

    def col_reduce(x, op):
        return op(x.reshape(slab // 8, 8, r), axis=0)

    def softmax(s_of, p_out, bmax, m_old, l_old):
        m_new = jnp.maximum(m_old, bmax)
        alpha = jnp.exp2(m_old - m_new)
        part = jnp.zeros((8, r), F32)
        for a in range(0, tk, slab):
            p = jnp.exp2(s_of(slice(a, a + slab)) - m_new)
            part = part + col_reduce(p, jnp.sum)
            p_out[a:a + slab, :] = p.astype(BF16)
        l_new = alpha * l_old + jnp.sum(part, axis=0, keepdims=True)
        return alpha, m_new, l_new

    def store_scores(s, s_out):
        part = jnp.full((8, r), NEG_BIG, F32)
        for a in range(0, tk, slab):
            sl = s[a:a + slab, :]
            s_out[a:a + slab, :] = sl
            part = jnp.maximum(part, col_reduce(sl, jnp.max))
        return jnp.max(part, axis=0, keepdims=True)

    def add_values(jv, p, alpha):
        acc_ref[...] = alpha * acc_ref[...] + jnp.dot(vt_ref[jv], p, preferred_element_type=F32)

    M_ROW, L_ROW, ALPHA_ROW, BMAX_ROW = (slice(k, k + 1) for k in range(4))

    def stage(j, s_in, s_out, p_in, p_out):
        bmax_next = store_scores(scores(j + 1), s_out)
        add_values(jnp.maximum(j - 1, 0), p_in[...], stat_ref[ALPHA_ROW, :])
        alpha, m_new, l_new = softmax(lambda rows: s_in[rows, :], p_out, stat_ref[BMAX_ROW, :],
                                      stat_ref[M_ROW, :], stat_ref[L_ROW, :])
        stat_ref[M_ROW, :] = m_new
        stat_ref[L_ROW, :] = l_new
        stat_ref[ALPHA_ROW, :] = alpha
        stat_ref[BMAX_ROW, :] = bmax_next

    def pair(t, carry):
        stage(2 * t, sa_ref, sb_ref, pb_ref, pa_ref)
        stage(2 * t + 1, sb_ref, sa_ref, pa_ref, pb_ref)
        return carry

    def finish(j_last, s_in, p_in, p_free):
        ql = lax.broadcasted_iota(jnp.int32, (slab, r), 1)
        lead = i * tq - j_last * tk + jnp.where(ql >= tq, ql - tq, ql)
        row = lax.broadcasted_iota(jnp.int32, (slab, r), 0)

        def masked(rows):
            return jnp.where(row + rows.start <= lead, s_in[rows, :], NEG_BIG)

        part = jnp.full((8, r), NEG_BIG, F32)
        for a in range(0, tk, slab):
            part = jnp.maximum(part, col_reduce(masked(slice(a, a + slab)), jnp.max))
        add_values(jnp.maximum(j_last - 1, 0), p_in[...], stat_ref[ALPHA_ROW, :])
        alpha, _, l_fin = softmax(masked, p_free, jnp.max(part, axis=0, keepdims=True),
                                  stat_ref[M_ROW, :], stat_ref[L_ROW, :])
        add_values(j_last, p_free[...], alpha)
        on = acc_ref[...] * (1.0 / l_fin)
        ot = on[:, 0:tq] - lam_ref[0] * on[:, tq:r]
        ot = ot * lax.rsqrt(jnp.mean(ot * ot, axis=0, keepdims=True) + EPS)
        o_ref[...] = (ot.T * gain_ref[...]).astype(BF16)

    n_full = (i * tq) // tk
    stat_ref[BMAX_ROW, :] = store_scores(scores(0), sa_ref)
    stat_ref[M_ROW, :] = jnp.full((1, r), NEG_BIG, F32)
    stat_ref[L_ROW, :] = jnp.zeros((1, r), F32)
    stat_ref[ALPHA_ROW, :] = jnp.ones((1, r), F32)
    lax.fori_loop(0, n_full // 2, pair, 0)

    @pl.when(n_full % 2 == 0)
    def _():
        finish(n_full, sa_ref, pb_ref, pa_ref)

    @pl.when(n_full % 2 == 1)
    def _():
        stage(n_full - 1, sa_ref, sb_ref, pb_ref, pa_ref)
        finish(n_full, sb_ref, pa_ref, pb_ref)


def _attn(lam, proj, qt, vt, gain, batch, seq):
    m = proj.shape[0]
    slopes = jnp.exp2(-(ALIBI_MAX_BIAS / HEADS) * jnp.arange(1, HEADS + 1, dtype=F32))
    assert ATT_TQ == IN_TM and ATT_TK == IN_TM
    nq = seq // ATT_TQ
    nk = seq // ATT_TK
    kcol = KB_TILE * HEADS
    return pl.pallas_call(
        _attn_kernel,
        grid=(batch, HEADS, nq),
        in_specs=[pl.BlockSpec(memory_space=pltpu.SMEM),
                  pl.BlockSpec(memory_space=pltpu.SMEM),
                  pl.BlockSpec((1, HEAD_W, ATT_TQ), lambda b, h, i: (b * nq + i, h, 0)),
                  pl.BlockSpec((seq, HEAD_W), lambda b, h, i: (b, kcol + h)),
                  pl.BlockSpec((nk, HEAD_W, ATT_TK), lambda b, h, i: (b, h, 0)),
                  pl.BlockSpec((1, HEAD_W), lambda b, h, i: (0, 0))],
        out_specs=pl.BlockSpec((ATT_TQ, HEAD_W), lambda b, h, i: (b * nq + i, h)),
        out_shape=jax.ShapeDtypeStruct((m, MIX_W), BF16),
        scratch_shapes=[pltpu.VMEM((seq, 2 * HEAD_W), BF16),
                        pltpu.VMEM((2 * HEAD_W, 2 * ATT_TQ), BF16),
                        pltpu.VMEM((HEAD_W, 2 * ATT_TQ), F32),
                        pltpu.VMEM((ATT_TK, 2 * ATT_TQ), F32),
                        pltpu.VMEM((ATT_TK, 2 * ATT_TQ), F32),
                        pltpu.VMEM((ATT_TK, 2 * ATT_TQ), BF16),
                        pltpu.VMEM((ATT_TK, 2 * ATT_TQ), BF16),
                        pltpu.VMEM((8, 2 * ATT_TQ), F32)],
        compiler_params=pltpu.CompilerParams(dimension_semantics=("parallel", "parallel", "arbitrary"),
                                             vmem_limit_bytes=VMEM_LIMIT),
        name="diffattn",
    )(lam, slopes, qt, proj, vt, gain)


def _channel_kernel(ya_ref, yb_ref, ga0_ref, ga1_ref, gb0_ref, gb1_ref, x_ref,
                    wua_ref, wub_ref, wo_ref, wg_ref, wu_ref, wd_ref,
                    gmix_ref, gpre_ref, gpost_ref, gnext_ref, x2_ref, *maybe_hn_ref):
    ua = jnp.dot(ya_ref[...], wua_ref[...], preferred_element_type=F32)
    ub = jnp.dot(yb_ref[...], wub_ref[...], preferred_element_type=F32)
    ga = jnp.concatenate([ga0_ref[...], ga1_ref[...]], axis=1).astype(F32)
    gb = jnp.concatenate([gb0_ref[...], gb1_ref[...]], axis=1).astype(F32)
    merged = (ga * ua + gb * ub).astype(BF16)
    mix = jnp.dot(merged, wo_ref[...], preferred_element_type=F32)
    x1 = x_ref[...] + _rms(mix, gmix_ref[...])
    h = _rms(x1, gpre_ref[...]).astype(BF16)

    acc = jnp.zeros((CH_TM, D_MODEL), F32)
    for c in range(D_FF // FF_TF):
        cs = slice(c * FF_TF, (c + 1) * FF_TF)
        g = jnp.dot(h, wg_ref[:, cs], preferred_element_type=F32)
        u = jnp.dot(h, wu_ref[:, cs], preferred_element_type=F32)
        ff = (g * _sigmoid(g) * u).astype(BF16)
        acc = acc + jnp.dot(ff, wd_ref[cs, :], preferred_element_type=F32)
    x2 = x1 + _rms(acc, gpost_ref[...])
    x2_ref[...] = x2
    for hn_ref in maybe_hn_ref:
        hn_ref[...] = _rms(x2, gnext_ref[...]).astype(BF16)


def _channel(ya, yb, proj, x2d, wua, wub, wo, wg, wu, wd, gmix, gpre, gpost, gnext, emit_next):
    m = ya.shape[0]
    tm = CH_TM

    def row(width, col=0):
        return pl.BlockSpec((tm, width), lambda i: (i, col))

    out_specs = [row(D_MODEL)] + ([row(D_MODEL)] if emit_next else [])
    out_shape = ([jax.ShapeDtypeStruct((m, D_MODEL), F32)]
                 + ([jax.ShapeDtypeStruct((m, D_MODEL), BF16)] if emit_next else []))
    return pl.pallas_call(
        _channel_kernel,
        grid=(m // tm,),
        in_specs=[row(MIX_W), row(MIX_W)] + [row(MIX_W, GATE_TILE + g) for g in range(4)] + [
                  row(D_MODEL),
                  _resident((MIX_W, D_MODEL)), _resident((MIX_W, D_MODEL)), _resident((D_MODEL, D_MODEL)),
                  _resident((D_MODEL, D_FF)), _resident((D_MODEL, D_FF)), _resident((D_FF, D_MODEL)),
                  _resident((1, D_MODEL)), _resident((1, D_MODEL)), _resident((1, D_MODEL)),
                  _resident((1, D_MODEL))],
        out_specs=out_specs,
        out_shape=out_shape,
        compiler_params=pltpu.CompilerParams(dimension_semantics=("parallel",),
                                             vmem_limit_bytes=VMEM_LIMIT),
        name="channel",
    )(ya, yb, proj, proj, proj, proj, x2d, wua, wub, wo, wg, wu, wd, gmix, gpre, gpost, gnext)


def kernel(x, lower_bounds, norm_mix_pre, norm_mix_post, norm_ffn_pre, norm_ffn_post, w_in, hg_out_norm,
           da_subln, lambda_q1, lambda_k1, lambda_q2, lambda_k2, w_up_a, w_up_b, w_out, w_ffn_gate,
           w_ffn_up, w_ffn_down):
    batch, seq, _ = x.shape
    depth = w_in.shape[0]
    m = batch * seq
    x2d = x.reshape(m, D_MODEL).astype(F32)

    lb_all = jnp.cumsum(jax.nn.softmax(lower_bounds.astype(F32), axis=0), axis=0)
    lb_all = lb_all - lb_all[0:1]

    h = None
    for l in range(depth):
        lb = lb_all[l]
        lbp = jnp.zeros((8, MIX_W), F32)
        lbp = lbp.at[0].set(jnp.maximum(jnp.log(lb), NEG_BIG)).at[1].set(jnp.log1p(-lb)).at[2].set(1.0 - lb)
        proj, lf, qt, vt = _in_proj(x2d if h is None else h, norm_mix_pre[l].reshape(1, D_MODEL),
                                    w_in[l].astype(BF16), lbp, prenorm=h is None)

        ya = _hgrn(proj, lf, hg_out_norm[l].reshape(1, HEAD_W), batch, seq)

        lam_init = 0.8 - 0.6 * math.exp(-0.3 * l)
        lam = (jnp.exp(jnp.sum(lambda_q1[l].astype(F32) * lambda_k1[l].astype(F32)))
               - jnp.exp(jnp.sum(lambda_q2[l].astype(F32) * lambda_k2[l].astype(F32))) + lam_init)
        sub_gain = (da_subln[l].astype(F32) * (1.0 - lam_init)).reshape(1, HEAD_W)
        yb = _attn(lam.reshape(1), proj, qt, vt, sub_gain, batch, seq)

        last = l == depth - 1
        g_next = norm_mix_pre[(l + 1) % depth].reshape(1, D_MODEL)
        outs = _channel(ya, yb, proj, x2d, w_up_a[l].astype(BF16), w_up_b[l].astype(BF16),
                        w_out[l].astype(BF16), w_ffn_gate[l].astype(BF16), w_ffn_up[l].astype(BF16),
                        w_ffn_down[l].astype(BF16), norm_mix_post[l].reshape(1, D_MODEL),
                        norm_ffn_pre[l].reshape(1, D_MODEL), norm_ffn_post[l].reshape(1, D_MODEL),
                        g_next, emit_next=not last)
        x2d, h = (outs[0], None) if last else outs
    return x2d.reshape(batch, seq, D_MODEL)
```

```python
import functools
import math

import jax
import jax.numpy as jnp
from jax import lax
from jax.experimental import pallas as pl
from jax.experimental.pallas import tpu as pltpu

F32 = jnp.float32
BF16 = jnp.bfloat16

D_MODEL = 1024
HEADS = 4
HEAD_W = 128
MIX_W = HEADS * HEAD_W
DQK = 64
D_FF = 2816
D_IN = 7 * MIX_W + 2 * D_MODEL
EPS = 1e-6
ALIBI_MAX_BIAS = 8.0
NEG_BIG = -1e30
LOG2E = math.log2(math.e)

IN_TM, IN_TN = 512, 512
HG_T, HG_C = 512, 64
ATT_TQ, ATT_TK = 512, 512
CH_TM, FF_TF = 512, 256
VMEM_LIMIT = 56 * 1024 * 1024


def _sigmoid(x):
    return 1.0 / (1.0 + jnp.exp(-x))


def _rms(x, gain):
    return x * lax.rsqrt(jnp.mean(x * x, axis=-1, keepdims=True) + EPS) * gain


def _resident(shape):
    return pl.BlockSpec(shape, lambda *_: (0,) * len(shape), pipeline_mode=pl.Buffered(1))


PROJ_TILES = 9
PROJ_W = PROJ_TILES * 512
KB_TILE, GATE_TILE = 4, 5


def _in_proj_kernel(x_ref, g_ref, w_ref, lb_ref, o_ref, lf_ref, qt_ref, vt_ref, h_ref, *, prenorm):
    if prenorm:
        h_ref[...] = _rms(x_ref[...], g_ref[...]).astype(BF16)
    else:
        h_ref[...] = x_ref[...]

    def silu(acc):
        return acc * _sigmoid(acc)

    def forget_gate(acc):
        log_lb, log_1m_lb, one_m_lb = lb_ref[0:1, :], lb_ref[1:2, :], lb_ref[2:3, :]
        e = jnp.exp(-jnp.abs(acc))
        log_sig = jnp.minimum(acc, 0.0) - jnp.log(1.0 + e)
        c = log_1m_lb + log_sig
        hi = jnp.maximum(log_lb, c)
        lf_ref[...] = hi + jnp.log(1.0 + jnp.exp(-jnp.abs(log_lb - c)))
        return one_m_lb * (jnp.where(acc >= 0.0, e, 1.0) / (1.0 + e))

    def query_scale(acc):
        return acc * (LOG2E / math.sqrt(DQK))

    def column_tile(jt):
        return jnp.dot(h_ref[...], w_ref[:, jt * IN_TN:(jt + 1) * IN_TN], preferred_element_type=F32)

    plan = [(0, silu, 0), (1, forget_gate, 1), (2, None, 2), (3, silu, 3), (5, None, KB_TILE)]
    plan += [(7 + g, _sigmoid, GATE_TILE + g) for g in range(4)]
    for jt, epilogue, out_tile in plan[:4]:
        acc = column_tile(jt)
        o_ref[:, out_tile * IN_TN:(out_tile + 1) * IN_TN] = (
            acc if epilogue is None else epilogue(acc)).astype(BF16)
    qt_ref[0] = query_scale(column_tile(4)).T.astype(BF16)
    vt_ref[0] = column_tile(6).T.astype(BF16)
    for jt, epilogue, out_tile in plan[4:]:
        acc = column_tile(jt)
        o_ref[:, out_tile * IN_TN:(out_tile + 1) * IN_TN] = (
            acc if epilogue is None else epilogue(acc)).astype(BF16)


def _in_proj(x_or_h, gain, w, lbp, prenorm):
    m = x_or_h.shape[0]
    nt = m // IN_TM
    return pl.pallas_call(
        functools.partial(_in_proj_kernel, prenorm=prenorm),
        grid=(nt,),
        in_specs=[pl.BlockSpec((IN_TM, D_MODEL), lambda i: (i, 0)),
                  _resident((1, D_MODEL)), _resident((D_MODEL, D_IN)), _resident((8, MIX_W))],
        out_specs=[pl.BlockSpec((IN_TM, PROJ_W), lambda i: (i, 0)),
                   pl.BlockSpec((IN_TM, MIX_W), lambda i: (i, 0)),
                   pl.BlockSpec((1, MIX_W, IN_TM), lambda i: (i, 0, 0)),
                   pl.BlockSpec((1, MIX_W, IN_TM), lambda i: (i, 0, 0))],
        out_shape=[jax.ShapeDtypeStruct((m, PROJ_W), BF16),
                   jax.ShapeDtypeStruct((m, MIX_W), F32),
                   jax.ShapeDtypeStruct((nt, MIX_W, IN_TM), BF16),
                   jax.ShapeDtypeStruct((nt, MIX_W, IN_TM), BF16)],
        scratch_shapes=[pltpu.VMEM((IN_TM, D_MODEL), BF16)],
        compiler_params=pltpu.CompilerParams(dimension_semantics=("parallel",),
                                             vmem_limit_bytes=VMEM_LIMIT),
        name="in_proj",
    )(x_or_h, gain, w, lbp)


def _row_bcast(b, rows, w):
    return jnp.concatenate([jnp.broadcast_to(b[r:r + 1, :], (w, HEAD_W)) for r in rows], axis=0)


def _hgrn_kernel(q_ref, lf_ref, k_ref, v_ref, g_ref, gain_ref, o_ref, st_ref):
    @pl.when(pl.program_id(1) == 0)
    def _():
        st_ref[...] = jnp.zeros_like(st_ref)

    c = HG_C
    ri = lax.broadcasted_iota(jnp.int32, (c, c), 0)
    ci = lax.broadcasted_iota(jnp.int32, (c, c), 1)
    tri = (ci <= ri).astype(BF16)
    level_masks = []
    for w in (32, 16, 8):
        tb, sb = ri // w, ci // w
        level_masks.append((tb == sb + 1) & (tb % 2 == 1))
    diag_masks = [(ci == (ri // 8) * 8 + s) & (ri % 8 >= s) for s in range(8)]
    ones_sq = jnp.ones((HEAD_W, HEAD_W), BF16)
    gain = gain_ref[...]

    def chunk(ic, carry):
        r0 = pl.multiple_of(ic * c, c)
        heads = [slice(h * HEAD_W, (h + 1) * HEAD_W) for h in range(HEADS)]
        nt_dims = (((1,), (1,)), ((), ()))
        q = [q_ref[pl.ds(r0, c), cs].astype(F32) for cs in heads]
        k = [k_ref[pl.ds(r0, c), cs].astype(F32) for cs in heads]
        v_bf = [v_ref[pl.ds(r0, c), cs] for cs in heads]

        b = []
        for cs in heads:
            lf = lf_ref[pl.ds(r0, c), cs]
            lf_hi = lf.astype(BF16)
            lf_lo = (lf - lf_hi.astype(F32)).astype(BF16)
            b.append(jnp.dot(tri, lf_hi, preferred_element_type=F32)
                     + jnp.dot(tri, lf_lo, preferred_element_type=F32))
        b_last = [bh[c - 1:c, :] for bh in b]

        upd = []
        for h in range(HEADS):
            ks = (k[h] * jnp.exp(b_last[h] - b[h])).astype(BF16)
            upd.append(lax.dot_general(v_bf[h], ks, (((0,), (0,)), ((), ())), preferred_element_type=F32))
        a = [jnp.zeros((c, c), F32) for _ in heads]
        for w, msk in zip((32, 16, 8), level_masks):
            nb = c // w
            for h in range(HEADS):
                start = _row_bcast(b[h], [j * w for j in range(nb)], w)
                nxt = _row_bcast(b[h], [(j + 1) * w for j in range(nb - 1)] + [c - 1], w)
                qh = (q[h] * jnp.exp(b[h] - start)).astype(BF16)
                kh = (k[h] * jnp.exp(nxt - b[h])).astype(BF16)
                aw = lax.dot_general(qh, kh, nt_dims, preferred_element_type=F32)
                a[h] = jnp.where(msk, aw, a[h])

        sums = []
        for h in range(HEADS):
            b3 = b[h].reshape(c // 8, 8, HEAD_W)
            q3 = q[h].reshape(c // 8, 8, HEAD_W)
            k3 = k[h].reshape(c // 8, 8, HEAD_W)
            terms = []
            for s in range(8):
                dec = jnp.exp(jnp.minimum(b3 - b3[:, s:s + 1, :], 0.0))
                terms.append((q3 * dec * k3[:, s:s + 1, :]).reshape(c, HEAD_W).astype(BF16))
            sums.append(jnp.dot(jnp.concatenate(terms, axis=0), ones_sq, preferred_element_type=F32))

        o = []
        for h in range(HEADS):
            st = st_ref[h]
            qi = (q[h] * jnp.exp(b[h])).astype(BF16)
            o.append(lax.dot_general(qi, st.astype(BF16), nt_dims, preferred_element_type=F32))
            st_ref[h] = st * jnp.exp(b_last[h]) + upd[h]
        for h, cs in enumerate(heads):
            for s in range(8):
                a[h] = jnp.where(diag_masks[s], sums[h][s * c:(s + 1) * c, 0:c], a[h])
            oh = o[h] + jnp.dot(a[h].astype(BF16), v_bf[h], preferred_element_type=F32)
            g = g_ref[pl.ds(r0, c), cs].astype(F32)
            o_ref[pl.ds(r0, c), cs] = (_rms(oh, gain) * g).astype(BF16)
        return carry

    lax.fori_loop(0, HG_T // c, chunk, 0, unroll=4)


def _hgrn(proj, lf, gain, batch, seq):
    m = proj.shape[0]
    nt = seq // HG_T

    def spec(col):
        return pl.BlockSpec((HG_T, MIX_W), lambda b, t: (b * nt + t, col))

    return pl.pallas_call(
        _hgrn_kernel,
        grid=(batch, nt),
        in_specs=[spec(0), spec(0), spec(1), spec(2), spec(3),
                  pl.BlockSpec((1, HEAD_W), lambda b, t: (0, 0))],
        out_specs=spec(0),
        out_shape=jax.ShapeDtypeStruct((m, MIX_W), BF16),
        scratch_shapes=[pltpu.VMEM((HEADS, HEAD_W, HEAD_W), F32)],
        compiler_params=pltpu.CompilerParams(dimension_semantics=("parallel", "arbitrary"),
                                             vmem_limit_bytes=VMEM_LIMIT),
        name="hgrn2",
    )(proj, lf, proj, proj, proj, gain)


ATT_HPG = 2


def _attn_kernel(lam_ref, slope_ref, qt_ref, k_ref, vt_ref, gain_ref, o_ref,
                 kaug_ref, q2t_ref, acc_ref, sa_ref, sb_ref, pa_ref, pb_ref, stat_ref):
    tq, r, tk = ATT_TQ, 2 * ATT_TQ, ATT_TK
    grp = pl.program_id(1)
    nblk = k_ref.shape[0] // tk
    slab = 64
    M_ROW, L_ROW, ALPHA_ROW, BMAX_ROW = (slice(k, k + 1) for k in range(4))

    def col_reduce(x, op):
        return op(x.reshape(slab // 8, 8, r), axis=0)

    def head(hh, i):
        cols = slice(hh * HEAD_W, (hh + 1) * HEAD_W)
        kaug, q2t, acc, stat = kaug_ref.at[hh], q2t_ref.at[hh], acc_ref.at[hh], stat_ref.at[hh]
        sa, sb, pa, pb = sa_ref.at[hh], sb_ref.at[hh], pa_ref.at[hh], pb_ref.at[hh]
        slope = slope_ref[grp * ATT_HPG + hh] * LOG2E

        def fill(jb):
            r0 = pl.multiple_of(jb * tk, tk)
            kaug[pl.ds(r0, tk), 0:HEAD_W] = k_ref[pl.ds(r0, tk), cols]
            row = lax.broadcasted_iota(jnp.int32, (tk, HEAD_W), 0) + jb * tk
            lane = lax.broadcasted_iota(jnp.int32, (tk, HEAD_W), 1)
            c = slope * row.astype(F32)
            c1 = c.astype(BF16).astype(F32)
            c2 = (c - c1).astype(BF16).astype(F32)
            c3 = c - c1 - c2
            bias = jnp.where(lane == 0, c1, jnp.where(lane == 1, c2, jnp.where(lane == 2, c3, 0.0)))
            kaug[pl.ds(r0, tk), HEAD_W:2 * HEAD_W] = bias.astype(BF16)

        def scores(j):
            r0 = pl.multiple_of(j * tk, tk)
            return jnp.dot(kaug[pl.ds(r0, tk), :], q2t[...], preferred_element_type=F32)

        def store_scores(s, s_out):
            part = jnp.full((8, r), NEG_BIG, F32)
            for a in range(0, tk, slab):
                sl = s[a:a + slab, :]
                s_out[a:a + slab, :] = sl
                part = jnp.maximum(part, col_reduce(sl, jnp.max))
            return jnp.max(part, axis=0, keepdims=True)

        def softmax(s_of, p_out, bmax, m_old, l_old):
            m_new = jnp.maximum(m_old, bmax)
            alpha = jnp.exp2(m_old - m_new)
            part = jnp.zeros((8, r), F32)
            for a in range(0, tk, slab):
                p = jnp.exp2(s_of(slice(a, a + slab)) - m_new)
                part = part + col_reduce(p, jnp.sum)
                p_out[a:a + slab, :] = p.astype(BF16)
            l_new = alpha * l_old + jnp.sum(part, axis=0, keepdims=True)
            return alpha, m_new, l_new

        def add_values(jv, p, alpha):
            acc[...] = alpha * acc[...] + jnp.dot(vt_ref[jv, cols, :], p, preferred_element_type=F32)

        def start():
            qt = qt_ref[i, cols, :].astype(F32)
            sub = lax.broadcasted_iota(jnp.int32, (HEAD_W, tq), 0)
            q2t[0:HEAD_W, 0:tq] = jnp.where(sub < DQK, qt, 0.0).astype(BF16)
            q2t[0:HEAD_W, tq:r] = jnp.where(sub >= DQK, qt, 0.0).astype(BF16)
            sub_r = lax.broadcasted_iota(jnp.int32, (HEAD_W, r), 0)
            q2t[HEAD_W:2 * HEAD_W, :] = jnp.where(sub_r < 3, 1.0, 0.0).astype(BF16)
            acc[...] = jnp.zeros((HEAD_W, r), F32)
            pb[...] = jnp.zeros((tk, r), BF16)
            stat[BMAX_ROW, :] = store_scores(scores(0), sa)
            stat[M_ROW, :] = jnp.full((1, r), NEG_BIG, F32)
            stat[L_ROW, :] = jnp.zeros((1, r), F32)
            stat[ALPHA_ROW, :] = jnp.ones((1, r), F32)

        def stage(j, flip):
            s_in, s_out, p_in, p_out = (sb, sa, pa, pb) if flip else (sa, sb, pb, pa)
            bmax_next = store_scores(scores(j + 1), s_out)
            add_values(jnp.maximum(j - 1, 0), p_in[...], stat[ALPHA_ROW, :])
            alpha, m_new, l_new = softmax(lambda rows: s_in[rows, :], p_out, stat[BMAX_ROW, :],
                                          stat[M_ROW, :], stat[L_ROW, :])
            stat[M_ROW, :] = m_new
            stat[L_ROW, :] = l_new
            stat[ALPHA_ROW, :] = alpha
            stat[BMAX_ROW, :] = bmax_next

        def finish(j_last, flip):
            s_in, p_in, p_free = (sb, pa, pb) if flip else (sa, pb, pa)
            ql = lax.broadcasted_iota(jnp.int32, (slab, r), 1)
            lead = i * tq - j_last * tk + jnp.where(ql >= tq, ql - tq, ql)
            row = lax.broadcasted_iota(jnp.int32, (slab, r), 0)

            def masked(rows):
                return jnp.where(row + rows.start <= lead, s_in[rows, :], NEG_BIG)

            part = jnp.full((8, r), NEG_BIG, F32)
            for a in range(0, tk, slab):
                part = jnp.maximum(part, col_reduce(masked(slice(a, a + slab)), jnp.max))
            add_values(jnp.maximum(j_last - 1, 0), p_in[...], stat[ALPHA_ROW, :])
            alpha, _, l_fin = softmax(masked, p_free, jnp.max(part, axis=0, keepdims=True),
                                      stat[M_ROW, :], stat[L_ROW, :])
            add_values(j_last, p_free[...], alpha)
            on = acc[...] * (1.0 / l_fin)
            ot = on[:, 0:tq] - lam_ref[0] * on[:, tq:r]
            ot = ot * lax.rsqrt(jnp.mean(ot * ot, axis=0, keepdims=True) + EPS)
            o_ref[pl.ds(pl.multiple_of(i * tq, tq), tq), cols] = (ot.T * gain_ref[...]).astype(BF16)

        return fill, start, stage, finish

    def fill_all(jb, carry):
        for hh in range(ATT_HPG):
            head(hh, 0)[0](jb)
        return carry
    lax.fori_loop(0, nblk, fill_all, 0)

    def query_tile(i, carry):
        heads = [head(hh, i) for hh in range(ATT_HPG)]
        for _, start, _, _ in heads:
            start()

        def pair(t, carry):
            for flip in (False, True):
                for _, _, stage, _ in heads:
                    stage(2 * t + int(flip), flip)
            return carry

        n_full = (i * tq) // tk
        lax.fori_loop(0, n_full // 2, pair, 0)

        @pl.when(n_full % 2 == 0)
        def _():
            for _, _, _, finish in heads:
                finish(n_full, False)

        @pl.when(n_full % 2 == 1)
        def _():
            for _, _, stage, _ in heads:
                stage(n_full - 1, False)
            for _, _, _, finish in heads:
                finish(n_full, True)
        return carry

    lax.fori_loop(0, qt_ref.shape[0], query_tile, 0)


def _attn(lam, proj, qt, vt, gain, batch, seq):
    m = proj.shape[0]
    slopes = jnp.exp2(-(ALIBI_MAX_BIAS / HEADS) * jnp.arange(1, HEADS + 1, dtype=F32))
    assert ATT_TQ == IN_TM and ATT_TK == IN_TM
    nq = seq // ATT_TQ
    nk = seq // ATT_TK
    gw = ATT_HPG * HEAD_W
    kcol = KB_TILE * (MIX_W // gw)
    hpg, r = ATT_HPG, 2 * ATT_TQ
    return pl.pallas_call(
        _attn_kernel,
        grid=(batch, HEADS // ATT_HPG),
        in_specs=[pl.BlockSpec(memory_space=pltpu.SMEM),
                  pl.BlockSpec(memory_space=pltpu.SMEM),
                  pl.BlockSpec((nq, gw, ATT_TQ), lambda b, g: (b, g, 0), pipeline_mode=pl.Buffered(1)),
                  pl.BlockSpec((seq, gw), lambda b, g: (b, kcol + g), pipeline_mode=pl.Buffered(1)),
                  pl.BlockSpec((nk, gw, ATT_TK), lambda b, g: (b, g, 0), pipeline_mode=pl.Buffered(1)),
                  pl.BlockSpec((1, HEAD_W), lambda b, g: (0, 0))],
        out_specs=pl.BlockSpec((seq, gw), lambda b, g: (b, g)),
        out_shape=jax.ShapeDtypeStruct((m, MIX_W), BF16),
        scratch_shapes=[pltpu.VMEM((hpg, seq, 2 * HEAD_W), BF16),
                        pltpu.VMEM((hpg, 2 * HEAD_W, r), BF16),
                        pltpu.VMEM((hpg, HEAD_W, r), F32),
                        pltpu.VMEM((hpg, ATT_TK, r), F32),
                        pltpu.VMEM((hpg, ATT_TK, r), F32),
                        pltpu.VMEM((hpg, ATT_TK, r), BF16),
                        pltpu.VMEM((hpg, ATT_TK, r), BF16),
                        pltpu.VMEM((hpg, 8, r), F32)],
        compiler_params=pltpu.CompilerParams(dimension_semantics=("parallel", "parallel"),
                                             vmem_limit_bytes=VMEM_LIMIT),
        name="diffattn",
    )(lam, slopes, qt, proj, vt, gain)


def _channel_kernel(ya_ref, yb_ref, ga0_ref, ga1_ref, gb0_ref, gb1_ref, x_ref,
                    wua_ref, wub_ref, wo_ref, wg_ref, wu_ref, wd_ref,
                    gmix_ref, gpre_ref, gpost_ref, gnext_ref, x2_ref, *maybe_hn_ref):
    ua = jnp.dot(ya_ref[...], wua_ref[...], preferred_element_type=F32)
    ub = jnp.dot(yb_ref[...], wub_ref[...], preferred_element_type=F32)
    ga = jnp.concatenate([ga0_ref[...], ga1_ref[...]], axis=1).astype(F32)
    gb = jnp.concatenate([gb0_ref[...], gb1_ref[...]], axis=1).astype(F32)
    merged = (ga * ua + gb * ub).astype(BF16)
    mix = jnp.dot(merged, wo_ref[...], preferred_element_type=F32)
    x1 = x_ref[...] + _rms(mix, gmix_ref[...])
    h = _rms(x1, gpre_ref[...]).astype(BF16)

    acc = jnp.zeros((CH_TM, D_MODEL), F32)
    for c in range(D_FF // FF_TF):
        cs = slice(c * FF_TF, (c + 1) * FF_TF)
        g = jnp.dot(h, wg_ref[:, cs], preferred_element_type=F32)
        u = jnp.dot(h, wu_ref[:, cs], preferred_element_type=F32)
        ff = (g * _sigmoid(g) * u).astype(BF16)
        acc = acc + jnp.dot(ff, wd_ref[cs, :], preferred_element_type=F32)
    x2 = x1 + _rms(acc, gpost_ref[...])
    x2_ref[...] = x2
    for hn_ref in maybe_hn_ref:
        hn_ref[...] = _rms(x2, gnext_ref[...]).astype(BF16)


def _channel(ya, yb, proj, x2d, wua, wub, wo, wg, wu, wd, gmix, gpre, gpost, gnext, emit_next):
    m = ya.shape[0]
    tm = CH_TM

    def row(width, col=0):
        return pl.BlockSpec((tm, width), lambda i: (i, col))

    out_specs = [row(D_MODEL)] + ([row(D_MODEL)] if emit_next else [])
    out_shape = ([jax.ShapeDtypeStruct((m, D_MODEL), F32)]
                 + ([jax.ShapeDtypeStruct((m, D_MODEL), BF16)] if emit_next else []))
    return pl.pallas_call(
        _channel_kernel,
        grid=(m // tm,),
        in_specs=[row(MIX_W), row(MIX_W)] + [row(MIX_W, GATE_TILE + g) for g in range(4)] + [
                  row(D_MODEL),
                  _resident((MIX_W, D_MODEL)), _resident((MIX_W, D_MODEL)), _resident((D_MODEL, D_MODEL)),
                  _resident((D_MODEL, D_FF)), _resident((D_MODEL, D_FF)), _resident((D_FF, D_MODEL)),
                  _resident((1, D_MODEL)), _resident((1, D_MODEL)), _resident((1, D_MODEL)),
                  _resident((1, D_MODEL))],
        out_specs=out_specs,
        out_shape=out_shape,
        compiler_params=pltpu.CompilerParams(dimension_semantics=("parallel",),
                                             vmem_limit_bytes=VMEM_LIMIT),
        name="channel",
    )(ya, yb, proj, proj, proj, proj, x2d, wua, wub, wo, wg, wu, wd, gmix, gpre, gpost, gnext)


def kernel(x, lower_bounds, norm_mix_pre, norm_mix_post, norm_ffn_pre, norm_ffn_post, w_in, hg_out_norm,
           da_subln, lambda_q1, lambda_k1, lambda_q2, lambda_k2, w_up_a, w_up_b, w_out, w_ffn_gate,
           w_ffn_up, w_ffn_down):
    batch, seq, _ = x.shape
    depth = w_in.shape[0]
    m = batch * seq
    x2d = x.reshape(m, D_MODEL).astype(F32)

    lb_all = jnp.cumsum(jax.nn.softmax(lower_bounds.astype(F32), axis=0), axis=0)
    lb_all = lb_all - lb_all[0:1]

    h = None
    for l in range(depth):
        lb = lb_all[l]
        lbp = jnp.zeros((8, MIX_W), F32)
        lbp = lbp.at[0].set(jnp.maximum(jnp.log(lb), NEG_BIG)).at[1].set(jnp.log1p(-lb)).at[2].set(1.0 - lb)
        proj, lf, qt, vt = _in_proj(x2d if h is None else h, norm_mix_pre[l].reshape(1, D_MODEL),
                                    w_in[l].astype(BF16), lbp, prenorm=h is None)

        ya = _hgrn(proj, lf, hg_out_norm[l].reshape(1, HEAD_W), batch, seq)

        lam_init = 0.8 - 0.6 * math.exp(-0.3 * l)
        lam = (jnp.exp(jnp.sum(lambda_q1[l].astype(F32) * lambda_k1[l].astype(F32)))
               - jnp.exp(jnp.sum(lambda_q2[l].astype(F32) * lambda_k2[l].astype(F32))) + lam_init)
        sub_gain = (da_subln[l].astype(F32) * (1.0 - lam_init)).reshape(1, HEAD_W)
        yb = _attn(lam.reshape(1), proj, qt, vt, sub_gain, batch, seq)

        last = l == depth - 1
        g_next = norm_mix_pre[(l + 1) % depth].reshape(1, D_MODEL)
        outs = _channel(ya, yb, proj, x2d, w_up_a[l].astype(BF16), w_up_b[l].astype(BF16),
                        w_out[l].astype(BF16), w_ffn_gate[l].astype(BF16), w_ffn_up[l].astype(BF16),
                        w_ffn_down[l].astype(BF16), norm_mix_post[l].reshape(1, D_MODEL),
                        norm_ffn_pre[l].reshape(1, D_MODEL), norm_ffn_post[l].reshape(1, D_MODEL),
                        g_next, emit_next=not last)
        x2d, h = (outs[0], None) if last else outs
    return x2d.reshape(batch, seq, D_MODEL)
```

```python
import functools
import math

import jax
import jax.numpy as jnp
from jax import lax
from jax.experimental import pallas as pl
from jax.experimental.pallas import tpu as pltpu

F32 = jnp.float32
BF16 = jnp.bfloat16

D_MODEL = 1024
HEADS = 4
HEAD_W = 128
MIX_W = HEADS * HEAD_W
DQK = 64
D_FF = 2816
D_IN = 7 * MIX_W + 2 * D_MODEL
EPS = 1e-6
ALIBI_MAX_BIAS = 8.0
NEG_BIG = -1e30
LOG2E = math.log2(math.e)

IN_TM, IN_TN = 512, 512
HG_T, HG_C = 512, 64
HG_GROUP = 8
HG_MAX_FACTORED_DECAY = 80.0
ATT_TQ, ATT_TK = 512, 512
CH_TM, FF_TF = 512, 256
VMEM_LIMIT = 56 * 1024 * 1024


def _sigmoid(x):
    return 1.0 / (1.0 + jnp.exp(-x))


def _rms(x, gain):
    return x * lax.rsqrt(jnp.mean(x * x, axis=-1, keepdims=True) + EPS) * gain


def _resident(shape, layer=None):
    if layer is None:
        return pl.BlockSpec(shape, lambda *_: (0,) * len(shape), pipeline_mode=pl.Buffered(1))
    return pl.BlockSpec((None,) + tuple(shape), lambda *_: (layer,) + (0,) * len(shape),
                        pipeline_mode=pl.Buffered(1))


PROJ_TILES = 9
PROJ_W = PROJ_TILES * 512
KB_TILE, GATE_TILE = 4, 5


def _in_proj_kernel(x_ref, g_ref, w_ref, lb_ref, o_ref, lf_ref, lfmin_ref, qt_ref, vt_ref, h_ref, *,
                    prenorm):
    if prenorm:
        h_ref[...] = _rms(x_ref[...], g_ref[...]).astype(BF16)
    else:
        h_ref[...] = x_ref[...]

    def silu(acc):
        return acc * _sigmoid(acc)

    def forget_gate(acc):
        log_lb, log_1m_lb, one_m_lb = lb_ref[0:1, :], lb_ref[1:2, :], lb_ref[2:3, :]
        e = jnp.exp(-jnp.abs(acc))
        log_sig = jnp.minimum(acc, 0.0) - jnp.log(1.0 + e)
        c = log_1m_lb + log_sig
        hi = jnp.maximum(log_lb, c)
        lf = hi + jnp.log(1.0 + jnp.exp(-jnp.abs(log_lb - c)))
        lf_ref[...] = lf
        lfmin_ref[0] = jnp.full((8, HEAD_W), jnp.min(lf), F32)
        return one_m_lb * (jnp.where(acc >= 0.0, e, 1.0) / (1.0 + e))

    def query_scale(acc):
        return acc * (LOG2E / math.sqrt(DQK))

    def column_tile(jt):
        return jnp.dot(h_ref[...], w_ref[:, jt * IN_TN:(jt + 1) * IN_TN], preferred_element_type=F32)

    plan = [(0, silu, 0), (1, forget_gate, 1), (2, None, 2), (3, silu, 3), (5, None, KB_TILE)]
    plan += [(7 + g, _sigmoid, GATE_TILE + g) for g in range(4)]
    for jt, epilogue, out_tile in plan[:4]:
        acc = column_tile(jt)
        o_ref[:, out_tile * IN_TN:(out_tile + 1) * IN_TN] = (
            acc if epilogue is None else epilogue(acc)).astype(BF16)
    qt_ref[0] = query_scale(column_tile(4)).T.astype(BF16)
    vt_ref[0] = column_tile(6).T.astype(BF16)
    for jt, epilogue, out_tile in plan[4:]:
        acc = column_tile(jt)
        o_ref[:, out_tile * IN_TN:(out_tile + 1) * IN_TN] = (
            acc if epilogue is None else epilogue(acc)).astype(BF16)


def _in_proj(x_or_h, gain, w_all, layer, lbp, prenorm):
    m = x_or_h.shape[0]
    nt = m // IN_TM
    return pl.pallas_call(
        functools.partial(_in_proj_kernel, prenorm=prenorm),
        grid=(nt,),
        in_specs=[pl.BlockSpec((IN_TM, D_MODEL), lambda i: (i, 0)),
                  _resident((1, D_MODEL)), _resident((D_MODEL, D_IN), layer), _resident((8, MIX_W))],
        out_specs=[pl.BlockSpec((IN_TM, PROJ_W), lambda i: (i, 0)),
                   pl.BlockSpec((IN_TM, MIX_W), lambda i: (i, 0)),
                   pl.BlockSpec((1, 8, HEAD_W), lambda i: (i, 0, 0)),
                   pl.BlockSpec((1, MIX_W, IN_TM), lambda i: (i, 0, 0)),
                   pl.BlockSpec((1, MIX_W, IN_TM), lambda i: (i, 0, 0))],
        out_shape=[jax.ShapeDtypeStruct((m, PROJ_W), BF16),
                   jax.ShapeDtypeStruct((m, MIX_W), F32),
                   jax.ShapeDtypeStruct((nt, 8, HEAD_W), F32),
                   jax.ShapeDtypeStruct((nt, MIX_W, IN_TM), BF16),
                   jax.ShapeDtypeStruct((nt, MIX_W, IN_TM), BF16)],
        scratch_shapes=[pltpu.VMEM((IN_TM, D_MODEL), BF16)],
        compiler_params=pltpu.CompilerParams(dimension_semantics=("parallel",),
                                             vmem_limit_bytes=VMEM_LIMIT),
        name="in_proj",
    )(x_or_h, gain, w_all, lbp)


def _row_bcast(b, rows, w):
    return jnp.concatenate([jnp.broadcast_to(b[r:r + 1, :], (w, HEAD_W)) for r in rows], axis=0)


def _hgrn_kernel(lfmin_ref, q_ref, lf_ref, k_ref, v_ref, g_ref, gain_ref, o_ref, st_ref):
    @pl.when(pl.program_id(1) == 0)
    def _():
        st_ref[...] = jnp.zeros_like(st_ref)

    c = HG_C
    ri = lax.broadcasted_iota(jnp.int32, (c, c), 0)
    ci = lax.broadcasted_iota(jnp.int32, (c, c), 1)
    tri = (ci <= ri).astype(BF16)
    level_masks = []
    for w in (32, 16, 8):
        tb, sb = ri // w, ci // w
        level_masks.append((tb == sb + 1) & (tb % 2 == 1))
    diag_masks = [(ci == (ri // 8) * 8 + s) & (ri % 8 >= s) for s in range(8)]
    same_block_mask = (ri // 8 == ci // 8) & (ci <= ri)
    ones_sq = jnp.ones((HEAD_W, HEAD_W), BF16)
    gain = gain_ref[...]

    def chunk_group(ig, carry, factored):
        nt_dims = (((1,), (1,)), ((), ()))
        items = [(pl.multiple_of((ig * HG_GROUP + ch) * c, c), slice(h * HEAD_W, (h + 1) * HEAD_W), h)
                 for ch in range(HG_GROUP) for h in range(HEADS)]
        n = len(items)
        q = [q_ref[pl.ds(r0, c), cs].astype(F32) for r0, cs, _ in items]
        k = [k_ref[pl.ds(r0, c), cs].astype(F32) for r0, cs, _ in items]
        v_bf = [v_ref[pl.ds(r0, c), cs] for r0, cs, _ in items]

        b = []
        for r0, cs, _ in items:
            lf = lf_ref[pl.ds(r0, c), cs]
            lf_hi = lf.astype(BF16)
            lf_lo = (lf - lf_hi.astype(F32)).astype(BF16)
            b.append(jnp.dot(tri, lf_hi, preferred_element_type=F32)
                     + jnp.dot(tri, lf_lo, preferred_element_type=F32))
        b_last = [bh[c - 1:c, :] for bh in b]

        upd = []
        for it in range(n):
            ks = (k[it] * jnp.exp(b_last[it] - b[it])).astype(BF16)
            upd.append(lax.dot_general(v_bf[it], ks, (((0,), (0,)), ((), ())), preferred_element_type=F32))
        a = [jnp.zeros((c, c), F32) for _ in items]
        for w, msk in zip((32, 16, 8), level_masks):
            nb = c // w
            for it in range(n):
                start = _row_bcast(b[it], [j * w for j in range(nb)], w)
                nxt = _row_bcast(b[it], [(j + 1) * w for j in range(nb - 1)] + [c - 1], w)
                qh = (q[it] * jnp.exp(b[it] - start)).astype(BF16)
                kh = (k[it] * jnp.exp(nxt - b[it])).astype(BF16)
                aw = lax.dot_general(qh, kh, nt_dims, preferred_element_type=F32)
                a[it] = jnp.where(msk, aw, a[it])
                if factored and w == 8:
                    kb = (k[it] * jnp.exp(start - b[it])).astype(BF16)
                    aw = lax.dot_general(qh, kb, nt_dims, preferred_element_type=F32)
                    a[it] = jnp.where(same_block_mask, aw, a[it])

        if not factored:
            for it in range(n):
                b3 = b[it].reshape(c // 8, 8, HEAD_W)
                q3 = q[it].reshape(c // 8, 8, HEAD_W)
                k3 = k[it].reshape(c // 8, 8, HEAD_W)
                terms = []
                for s in range(8):
                    dec = jnp.exp(jnp.minimum(b3 - b3[:, s:s + 1, :], 0.0))
                    terms.append((q3 * dec * k3[:, s:s + 1, :]).reshape(c, HEAD_W).astype(BF16))
                sums = jnp.dot(jnp.concatenate(terms, axis=0), ones_sq, preferred_element_type=F32)
                for s in range(8):
                    a[it] = jnp.where(diag_masks[s], sums[s * c:(s + 1) * c, 0:c], a[it])

        o = [None] * n
        for h in range(HEADS):
            st = st_ref[h]
            for it in range(h, n, HEADS):
                qi = (q[it] * jnp.exp(b[it])).astype(BF16)
                o[it] = lax.dot_general(qi, st.astype(BF16), nt_dims, preferred_element_type=F32)
                st = st * jnp.exp(b_last[it]) + upd[it]
            st_ref[h] = st
        for it, (r0, cs, _) in enumerate(items):
            oh = o[it] + jnp.dot(a[it].astype(BF16), v_bf[it], preferred_element_type=F32)
            g = g_ref[pl.ds(r0, c), cs].astype(F32)
            o_ref[pl.ds(r0, c), cs] = (_rms(oh, gain) * g).astype(BF16)
        return carry

    tile = pl.program_id(0) * pl.num_programs(1) + pl.program_id(1)
    mild = lfmin_ref[tile] * -8.0 < HG_MAX_FACTORED_DECAY
    n_groups = HG_T // (c * HG_GROUP)

    @pl.when(mild)
    def _():
        lax.fori_loop(0, n_groups, functools.partial(chunk_group, factored=True), 0)

    @pl.when(jnp.logical_not(mild))
    def _():
        lax.fori_loop(0, n_groups, functools.partial(chunk_group, factored=False), 0)


def _hgrn(proj, lf, lfmin, gain, batch, seq):
    m = proj.shape[0]
    nt = seq // HG_T
    assert HG_T == IN_TM

    def spec(col):
        return pl.BlockSpec((HG_T, MIX_W), lambda b, t: (b * nt + t, col))

    return pl.pallas_call(
        _hgrn_kernel,
        grid=(batch, nt),
        in_specs=[pl.BlockSpec(memory_space=pltpu.SMEM),
                  spec(0), spec(0), spec(1), spec(2), spec(3),
                  pl.BlockSpec((1, HEAD_W), lambda b, t: (0, 0))],
        out_specs=spec(0),
        out_shape=jax.ShapeDtypeStruct((m, MIX_W), BF16),
        scratch_shapes=[pltpu.VMEM((HEADS, HEAD_W, HEAD_W), F32)],
        compiler_params=pltpu.CompilerParams(dimension_semantics=("parallel", "arbitrary"),
                                             vmem_limit_bytes=VMEM_LIMIT),
        name="hgrn2",
    )(lfmin, proj, lf, proj, proj, proj, gain)


ATT_HPG = 2


def _attn_kernel(lam_ref, slope_ref, qt_ref, k_ref, vt_ref, gain_ref, o_ref,
                 kaug_ref, q2t_ref, acc_ref, sa_ref, sb_ref, pa_ref, pb_ref, stat_ref):
    tq, r, tk = ATT_TQ, 2 * ATT_TQ, ATT_TK
    grp = pl.program_id(1)
    nblk = k_ref.shape[0] // tk
    slab = 64
    M_ROW, L_ROW, ALPHA_ROW, BMAX_ROW = (slice(k, k + 1) for k in range(4))

    def col_reduce(x, op):
        return op(x.reshape(slab // 8, 8, r), axis=0)

    def head(hh, i):
        cols = slice(hh * HEAD_W, (hh + 1) * HEAD_W)
        kaug, q2t, acc, stat = kaug_ref.at[hh], q2t_ref.at[hh], acc_ref.at[hh], stat_ref.at[hh]
        sa, sb, pa, pb = sa_ref.at[hh], sb_ref.at[hh], pa_ref.at[hh], pb_ref.at[hh]
        slope = slope_ref[grp * ATT_HPG + hh] * LOG2E

        def fill(jb):
            r0 = pl.multiple_of(jb * tk, tk)
            kaug[pl.ds(r0, tk), 0:HEAD_W] = k_ref[pl.ds(r0, tk), cols]
            row = lax.broadcasted_iota(jnp.int32, (tk, HEAD_W), 0) + jb * tk
            lane = lax.broadcasted_iota(jnp.int32, (tk, HEAD_W), 1)
            c = slope * row.astype(F32)
            c1 = c.astype(BF16).astype(F32)
            c2 = (c - c1).astype(BF16).astype(F32)
            c3 = c - c1 - c2
            bias = jnp.where(lane == 0, c1, jnp.where(lane == 1, c2, jnp.where(lane == 2, c3, 0.0)))
            kaug[pl.ds(r0, tk), HEAD_W:2 * HEAD_W] = bias.astype(BF16)

        def scores(j):
            r0 = pl.multiple_of(j * tk, tk)
            return jnp.dot(kaug[pl.ds(r0, tk), :], q2t[...], preferred_element_type=F32)

        def store_scores(s, s_out):
            part = jnp.full((8, r), NEG_BIG, F32)
            for a in range(0, tk, slab):
                sl = s[a:a + slab, :]
                s_out[a:a + slab, :] = sl
                part = jnp.maximum(part, col_reduce(sl, jnp.max))
            return jnp.max(part, axis=0, keepdims=True)

        def softmax(s_of, p_out, bmax, m_old, l_old):
            m_new = jnp.maximum(m_old, bmax)
            alpha = jnp.exp2(m_old - m_new)
            part = jnp.zeros((8, r), F32)
            for a in range(0, tk, slab):
                p = jnp.exp2(s_of(slice(a, a + slab)) - m_new)
                part = part + col_reduce(p, jnp.sum)
                p_out[a:a + slab, :] = p.astype(BF16)
            l_new = alpha * l_old + jnp.sum(part, axis=0, keepdims=True)
            return alpha, m_new, l_new

        def add_values(jv, p, alpha):
            acc[...] = alpha * acc[...] + jnp.dot(vt_ref[jv, cols, :], p, preferred_element_type=F32)

        def start():
            qt = qt_ref[i, cols, :].astype(F32)
            sub = lax.broadcasted_iota(jnp.int32, (HEAD_W, tq), 0)
            q2t[0:HEAD_W, 0:tq] = jnp.where(sub < DQK, qt, 0.0).astype(BF16)
            q2t[0:HEAD_W, tq:r] = jnp.where(sub >= DQK, qt, 0.0).astype(BF16)
            sub_r = lax.broadcasted_iota(jnp.int32, (HEAD_W, r), 0)
            q2t[HEAD_W:2 * HEAD_W, :] = jnp.where(sub_r < 3, 1.0, 0.0).astype(BF16)
            acc[...] = jnp.zeros((HEAD_W, r), F32)
            pb[...] = jnp.zeros((tk, r), BF16)
            stat[BMAX_ROW, :] = store_scores(scores(0), sa)
            stat[M_ROW, :] = jnp.full((1, r), NEG_BIG, F32)
            stat[L_ROW, :] = jnp.zeros((1, r), F32)
            stat[ALPHA_ROW, :] = jnp.ones((1, r), F32)

        def stage(j, flip):
            s_in, s_out, p_in, p_out = (sb, sa, pa, pb) if flip else (sa, sb, pb, pa)
            bmax_next = store_scores(scores(j + 1), s_out)
            add_values(jnp.maximum(j - 1, 0), p_in[...], stat[ALPHA_ROW, :])
            alpha, m_new, l_new = softmax(lambda rows: s_in[rows, :], p_out, stat[BMAX_ROW, :],
                                          stat[M_ROW, :], stat[L_ROW, :])
            stat[M_ROW, :] = m_new
            stat[L_ROW, :] = l_new
            stat[ALPHA_ROW, :] = alpha
            stat[BMAX_ROW, :] = bmax_next

        def finish(j_last, flip):
            s_in, p_in, p_free = (sb, pa, pb) if flip else (sa, pb, pa)
            ql = lax.broadcasted_iota(jnp.int32, (slab, r), 1)
            lead = i * tq - j_last * tk + jnp.where(ql >= tq, ql - tq, ql)
            row = lax.broadcasted_iota(jnp.int32, (slab, r), 0)

            def masked(rows):
                return jnp.where(row + rows.start <= lead, s_in[rows, :], NEG_BIG)

            part = jnp.full((8, r), NEG_BIG, F32)
            for a in range(0, tk, slab):
                part = jnp.maximum(part, col_reduce(masked(slice(a, a + slab)), jnp.max))
            add_values(jnp.maximum(j_last - 1, 0), p_in[...], stat[ALPHA_ROW, :])
            alpha, _, l_fin = softmax(masked, p_free, jnp.max(part, axis=0, keepdims=True),
                                      stat[M_ROW, :], stat[L_ROW, :])
            add_values(j_last, p_free[...], alpha)
            on = acc[...] * (1.0 / l_fin)
            ot = on[:, 0:tq] - lam_ref[0] * on[:, tq:r]
            ot = ot * lax.rsqrt(jnp.mean(ot * ot, axis=0, keepdims=True) + EPS)
            o_ref[pl.ds(pl.multiple_of(i * tq, tq), tq), cols] = (ot.T * gain_ref[...]).astype(BF16)

        return fill, start, stage, finish

    def fill_all(jb, carry):
        for hh in range(ATT_HPG):
            head(hh, 0)[0](jb)
        return carry
    lax.fori_loop(0, nblk, fill_all, 0)

    def query_tile(i, carry):
        heads = [head(hh, i) for hh in range(ATT_HPG)]
        for _, start, _, _ in heads:
            start()

        def pair(t, carry):
            for flip in (False, True):
                for _, _, stage, _ in heads:
                    stage(2 * t + int(flip), flip)
            return carry

        n_full = (i * tq) // tk
        lax.fori_loop(0, n_full // 2, pair, 0)

        @pl.when(n_full % 2 == 0)
        def _():
            for _, _, _, finish in heads:
                finish(n_full, False)

        @pl.when(n_full % 2 == 1)
        def _():
            for _, _, stage, _ in heads:
                stage(n_full - 1, False)
            for _, _, _, finish in heads:
                finish(n_full, True)
        return carry

    lax.fori_loop(0, qt_ref.shape[0], query_tile, 0)


def _attn(lam, proj, qt, vt, gain, batch, seq):
    m = proj.shape[0]
    slopes = jnp.exp2(-(ALIBI_MAX_BIAS / HEADS) * jnp.arange(1, HEADS + 1, dtype=F32))
    assert ATT_TQ == IN_TM and ATT_TK == IN_TM
    nq = seq // ATT_TQ
    nk = seq // ATT_TK
    gw = ATT_HPG * HEAD_W
    kcol = KB_TILE * (MIX_W // gw)
    hpg, r = ATT_HPG, 2 * ATT_TQ
    return pl.pallas_call(
        _attn_kernel,
        grid=(batch, HEADS // ATT_HPG),
        in_specs=[pl.BlockSpec(memory_space=pltpu.SMEM),
                  pl.BlockSpec(memory_space=pltpu.SMEM),
                  pl.BlockSpec((nq, gw, ATT_TQ), lambda b, g: (b, g, 0), pipeline_mode=pl.Buffered(1)),
                  pl.BlockSpec((seq, gw), lambda b, g: (b, kcol + g), pipeline_mode=pl.Buffered(1)),
                  pl.BlockSpec((nk, gw, ATT_TK), lambda b, g: (b, g, 0), pipeline_mode=pl.Buffered(1)),
                  pl.BlockSpec((1, HEAD_W), lambda b, g: (0, 0))],
        out_specs=pl.BlockSpec((seq, gw), lambda b, g: (b, g)),
        out_shape=jax.ShapeDtypeStruct((m, MIX_W), BF16),
        scratch_shapes=[pltpu.VMEM((hpg, seq, 2 * HEAD_W), BF16),
                        pltpu.VMEM((hpg, 2 * HEAD_W, r), BF16),
                        pltpu.VMEM((hpg, HEAD_W, r), F32),
                        pltpu.VMEM((hpg, ATT_TK, r), F32),
                        pltpu.VMEM((hpg, ATT_TK, r), F32),
                        pltpu.VMEM((hpg, ATT_TK, r), BF16),
                        pltpu.VMEM((hpg, ATT_TK, r), BF16),
                        pltpu.VMEM((hpg, 8, r), F32)],
        compiler_params=pltpu.CompilerParams(dimension_semantics=("parallel", "parallel"),
                                             vmem_limit_bytes=VMEM_LIMIT),
        name="diffattn",
    )(lam, slopes, qt, proj, vt, gain)


def _channel_kernel(ya_ref, yb_ref, ga0_ref, ga1_ref, gb0_ref, gb1_ref, x_ref,
                    wua_ref, wub_ref, wo_ref, wg_ref, wu_ref, wd_ref,
                    gmix_ref, gpre_ref, gpost_ref, gnext_ref, x2_ref, *maybe_hn_ref):
    ua = jnp.dot(ya_ref[...], wua_ref[...], preferred_element_type=F32)
    ub = jnp.dot(yb_ref[...], wub_ref[...], preferred_element_type=F32)
    ga = jnp.concatenate([ga0_ref[...], ga1_ref[...]], axis=1).astype(F32)
    gb = jnp.concatenate([gb0_ref[...], gb1_ref[...]], axis=1).astype(F32)
    merged = (ga * ua + gb * ub).astype(BF16)
    mix = jnp.dot(merged, wo_ref[...], preferred_element_type=F32)
    x1 = x_ref[...] + _rms(mix, gmix_ref[...])
    h = _rms(x1, gpre_ref[...]).astype(BF16)

    acc = jnp.zeros((CH_TM, D_MODEL), F32)
    for c in range(D_FF // FF_TF):
        cs = slice(c * FF_TF, (c + 1) * FF_TF)
        g = jnp.dot(h, wg_ref[:, cs], preferred_element_type=F32)
        u = jnp.dot(h, wu_ref[:, cs], preferred_element_type=F32)
        ff = (g * _sigmoid(g) * u).astype(BF16)
        acc = acc + jnp.dot(ff, wd_ref[cs, :], preferred_element_type=F32)
    x2 = x1 + _rms(acc, gpost_ref[...])
    x2_ref[...] = x2
    for hn_ref in maybe_hn_ref:
        hn_ref[...] = _rms(x2, gnext_ref[...]).astype(BF16)


def _channel(ya, yb, proj, x2d, wua, wub, wo, wg, wu, wd, layer, gmix, gpre, gpost, gnext, emit_next):
    m = ya.shape[0]
    tm = CH_TM

    def row(width, col=0):
        return pl.BlockSpec((tm, width), lambda i: (i, col))

    out_specs = [row(D_MODEL)] + ([row(D_MODEL)] if emit_next else [])
    out_shape = ([jax.ShapeDtypeStruct((m, D_MODEL), F32)]
                 + ([jax.ShapeDtypeStruct((m, D_MODEL), BF16)] if emit_next else []))
    return pl.pallas_call(
        _channel_kernel,
        grid=(m // tm,),
        in_specs=[row(MIX_W), row(MIX_W)] + [row(MIX_W, GATE_TILE + g) for g in range(4)] + [
                  row(D_MODEL),
                  _resident((MIX_W, D_MODEL), layer), _resident((MIX_W, D_MODEL), layer),
                  _resident((D_MODEL, D_MODEL), layer),
                  _resident((D_MODEL, D_FF), layer), _resident((D_MODEL, D_FF), layer),
                  _resident((D_FF, D_MODEL), layer),
                  _resident((1, D_MODEL)), _resident((1, D_MODEL)), _resident((1, D_MODEL)),
                  _resident((1, D_MODEL))],
        out_specs=out_specs,
        out_shape=out_shape,
        compiler_params=pltpu.CompilerParams(dimension_semantics=("parallel",),
                                             vmem_limit_bytes=VMEM_LIMIT),
        name="channel",
    )(ya, yb, proj, proj, proj, proj, x2d, wua, wub, wo, wg, wu, wd, gmix, gpre, gpost, gnext)


def kernel(x, lower_bounds, norm_mix_pre, norm_mix_post, norm_ffn_pre, norm_ffn_post, w_in, hg_out_norm,
           da_subln, lambda_q1, lambda_k1, lambda_q2, lambda_k2, w_up_a, w_up_b, w_out, w_ffn_gate,
           w_ffn_up, w_ffn_down):
    batch, seq, _ = x.shape
    depth = w_in.shape[0]
    m = batch * seq
    x2d = x.reshape(m, D_MODEL).astype(F32)

    lb_all = jnp.cumsum(jax.nn.softmax(lower_bounds.astype(F32), axis=0), axis=0)
    lb_all = lb_all - lb_all[0:1]

    w_in, w_up_a, w_up_b, w_out, w_ffn_gate, w_ffn_up, w_ffn_down = (
        w.astype(BF16) for w in (w_in, w_up_a, w_up_b, w_out, w_ffn_gate, w_ffn_up, w_ffn_down))

    h = None
    for l in range(depth):
        lb = lb_all[l]
        lbp = jnp.zeros((8, MIX_W), F32)
        lbp = lbp.at[0].set(jnp.maximum(jnp.log(lb), NEG_BIG)).at[1].set(jnp.log1p(-lb)).at[2].set(1.0 - lb)
        proj, lf, lfmin, qt, vt = _in_proj(x2d if h is None else h, norm_mix_pre[l].reshape(1, D_MODEL),
                                           w_in, l, lbp, prenorm=h is None)

        ya = _hgrn(proj, lf, lfmin[:, 0, 0], hg_out_norm[l].reshape(1, HEAD_W), batch, seq)

        lam_init = 0.8 - 0.6 * math.exp(-0.3 * l)
        lam = (jnp.exp(jnp.sum(lambda_q1[l].astype(F32) * lambda_k1[l].astype(F32)))
               - jnp.exp(jnp.sum(lambda_q2[l].astype(F32) * lambda_k2[l].astype(F32))) + lam_init)
        sub_gain = (da_subln[l].astype(F32) * (1.0 - lam_init)).reshape(1, HEAD_W)
        yb = _attn(lam.reshape(1), proj, qt, vt, sub_gain, batch, seq)

        last = l == depth - 1
        g_next = norm_mix_pre[(l + 1) % depth].reshape(1, D_MODEL)
        outs = _channel(ya, yb, proj, x2d, w_up_a, w_up_b, w_out, w_ffn_gate, w_ffn_up, w_ffn_down, l,
                        norm_mix_post[l].reshape(1, D_MODEL),
                        norm_ffn_pre[l].reshape(1, D_MODEL), norm_ffn_post[l].reshape(1, D_MODEL),
                        g_next, emit_next=not last)
        x2d, h = (outs[0], None) if last else outs
    return x2d.reshape(batch, seq, D_MODEL)
```

```python
import functools
import math

import jax
import jax.numpy as jnp
from jax import lax
from jax.experimental import pallas as pl
from jax.experimental.pallas import tpu as pltpu

F32 = jnp.float32
BF16 = jnp.bfloat16

D_MODEL = 1024
HEADS = 4
HEAD_W = 128
MIX_W = HEADS * HEAD_W
DQK = 64
D_FF = 2816
D_IN = 7 * MIX_W + 2 * D_MODEL
EPS = 1e-6
ALIBI_MAX_BIAS = 8.0
NEG_BIG = -1e30
LOG2E = math.log2(math.e)

IN_TM, IN_TN = 512, 512
HG_T, HG_C = 512, 64
HG_GROUP = 8
HG_MAX_FACTORED_DECAY = 80.0
ATT_TQ, ATT_TK = 512, 512
CH_TM, FF_TF = 512, 256
VMEM_LIMIT = 56 * 1024 * 1024


def _sigmoid(x):
    return 1.0 / (1.0 + jnp.exp(-x))


def _rms(x, gain):
    return x * lax.rsqrt(jnp.mean(x * x, axis=-1, keepdims=True) + EPS) * gain


def _resident(shape, layer=None):
    if layer is None:
        return pl.BlockSpec(shape, lambda *_: (0,) * len(shape), pipeline_mode=pl.Buffered(1))
    return pl.BlockSpec((None,) + tuple(shape), lambda *_: (layer,) + (0,) * len(shape),
                        pipeline_mode=pl.Buffered(1))


PROJ_TILES = 9
PROJ_W = PROJ_TILES * 512
KB_TILE, GATE_TILE = 4, 5


def _in_proj_kernel(x_ref, g_ref, w_ref, lb_ref, o_ref, lf_ref, lfmin_ref, qt_ref, vt_ref, h_ref, *,
                    prenorm):
    if prenorm:
        h_ref[...] = _rms(x_ref[...], g_ref[...]).astype(BF16)
    else:
        h_ref[...] = x_ref[...]

    def silu(acc):
        return acc * _sigmoid(acc)

    def forget_gate(acc):
        log_lb, log_1m_lb, one_m_lb = lb_ref[0:1, :], lb_ref[1:2, :], lb_ref[2:3, :]
        e = jnp.exp(-jnp.abs(acc))
        log_sig = jnp.minimum(acc, 0.0) - jnp.log(1.0 + e)
        c = log_1m_lb + log_sig
        hi = jnp.maximum(log_lb, c)
        lf = hi + jnp.log(1.0 + jnp.exp(-jnp.abs(log_lb - c)))
        lf_ref[...] = lf
        lfmin_ref[0] = jnp.full((8, HEAD_W), jnp.min(lf), F32)
        return one_m_lb * (jnp.where(acc >= 0.0, e, 1.0) / (1.0 + e))

    def query_scale(acc):
        return acc * (LOG2E / math.sqrt(DQK))

    def column_tile(jt):
        return jnp.dot(h_ref[...], w_ref[:, jt * IN_TN:(jt + 1) * IN_TN], preferred_element_type=F32)

    plan = [(1, forget_gate, 1), (0, silu, 0), (3, silu, 3)]
    plan += [(7 + g, _sigmoid, GATE_TILE + g) for g in range(4)]
    tail = [(2, None, 2), (5, None, KB_TILE)]
    for jt, epilogue, out_tile in plan:
        o_ref[:, out_tile * IN_TN:(out_tile + 1) * IN_TN] = epilogue(column_tile(jt)).astype(BF16)
    qt_ref[0] = query_scale(column_tile(4)).T.astype(BF16)
    vt_ref[0] = column_tile(6).T.astype(BF16)
    for jt, _, out_tile in tail:
        o_ref[:, out_tile * IN_TN:(out_tile + 1) * IN_TN] = column_tile(jt).astype(BF16)


def _in_proj(x_or_h, gain, w_all, layer, lbp, prenorm):
    m = x_or_h.shape[0]
    nt = m // IN_TM
    return pl.pallas_call(
        functools.partial(_in_proj_kernel, prenorm=prenorm),
        grid=(nt,),
        in_specs=[pl.BlockSpec((IN_TM, D_MODEL), lambda i: (i, 0)),
                  _resident((1, D_MODEL)), _resident((D_MODEL, D_IN), layer), _resident((8, MIX_W))],
        out_specs=[pl.BlockSpec((IN_TM, PROJ_W), lambda i: (i, 0)),
                   pl.BlockSpec((IN_TM, MIX_W), lambda i: (i, 0)),
                   pl.BlockSpec((1, 8, HEAD_W), lambda i: (i, 0, 0)),
                   pl.BlockSpec((1, MIX_W, IN_TM), lambda i: (i, 0, 0)),
                   pl.BlockSpec((1, MIX_W, IN_TM), lambda i: (i, 0, 0))],
        out_shape=[jax.ShapeDtypeStruct((m, PROJ_W), BF16),
                   jax.ShapeDtypeStruct((m, MIX_W), F32),
                   jax.ShapeDtypeStruct((nt, 8, HEAD_W), F32),
                   jax.ShapeDtypeStruct((nt, MIX_W, IN_TM), BF16),
                   jax.ShapeDtypeStruct((nt, MIX_W, IN_TM), BF16)],
        scratch_shapes=[pltpu.VMEM((IN_TM, D_MODEL), BF16)],
        compiler_params=pltpu.CompilerParams(dimension_semantics=("parallel",),
                                             vmem_limit_bytes=VMEM_LIMIT),
        name="in_proj",
    )(x_or_h, gain, w_all, lbp)


def _row_bcast(b, rows, w):
    return jnp.concatenate([jnp.broadcast_to(b[r:r + 1, :], (w, HEAD_W)) for r in rows], axis=0)


def _hgrn_kernel(lfmin_ref, q_ref, lf_ref, k_ref, v_ref, g_ref, gain_ref, o_ref, st_ref):
    @pl.when(pl.program_id(1) == 0)
    def _():
        st_ref[...] = jnp.zeros_like(st_ref)

    c = HG_C
    ri = lax.broadcasted_iota(jnp.int32, (c, c), 0)
    ci = lax.broadcasted_iota(jnp.int32, (c, c), 1)
    tri = (ci <= ri).astype(BF16)
    level_masks = []
    for w in (32, 16, 8):
        tb, sb = ri // w, ci // w
        level_masks.append((tb == sb + 1) & (tb % 2 == 1))
    diag_masks = [(ci == (ri // 8) * 8 + s) & (ri % 8 >= s) for s in range(8)]
    same_block_mask = (ri // 8 == ci // 8) & (ci <= ri)
    ones_sq = jnp.ones((HEAD_W, HEAD_W), BF16)
    gain = gain_ref[...]

    def chunk_group(ig, carry, factored):
        nt_dims = (((1,), (1,)), ((), ()))
        items = [(pl.multiple_of((ig * HG_GROUP + ch) * c, c), slice(h * HEAD_W, (h + 1) * HEAD_W), h)
                 for ch in range(HG_GROUP) for h in range(HEADS)]
        n = len(items)
        q = [q_ref[pl.ds(r0, c), cs].astype(F32) for r0, cs, _ in items]
        k = [k_ref[pl.ds(r0, c), cs].astype(F32) for r0, cs, _ in items]
        v_bf = [v_ref[pl.ds(r0, c), cs] for r0, cs, _ in items]

        b = []
        for r0, cs, _ in items:
            lf = lf_ref[pl.ds(r0, c), cs]
            lf_hi = lf.astype(BF16)
            lf_lo = (lf - lf_hi.astype(F32)).astype(BF16)
            b.append(jnp.dot(tri, lf_hi, preferred_element_type=F32)
                     + jnp.dot(tri, lf_lo, preferred_element_type=F32))
        b_last = [bh[c - 1:c, :] for bh in b]

        upd = []
        for it in range(n):
            ks = (k[it] * jnp.exp(b_last[it] - b[it])).astype(BF16)
            upd.append(lax.dot_general(v_bf[it], ks, (((0,), (0,)), ((), ())), preferred_element_type=F32))
        a = [jnp.zeros((c, c), F32) for _ in items]
        for w, msk in zip((32, 16, 8), level_masks):
            nb = c // w
            for it in range(n):
                start = _row_bcast(b[it], [j * w for j in range(nb)], w)
                nxt = _row_bcast(b[it], [(j + 1) * w for j in range(nb - 1)] + [c - 1], w)
                qh = (q[it] * jnp.exp(b[it] - start)).astype(BF16)
                kh = (k[it] * jnp.exp(nxt - b[it])).astype(BF16)
                aw = lax.dot_general(qh, kh, nt_dims, preferred_element_type=F32)
                a[it] = jnp.where(msk, aw, a[it])
                if factored and w == 8:
                    kb = (k[it] * jnp.exp(start - b[it])).astype(BF16)
                    aw = lax.dot_general(qh, kb, nt_dims, preferred_element_type=F32)
                    a[it] = jnp.where(same_block_mask, aw, a[it])

        if not factored:
            for it in range(n):
                b3 = b[it].reshape(c // 8, 8, HEAD_W)
                q3 = q[it].reshape(c // 8, 8, HEAD_W)
                k3 = k[it].reshape(c // 8, 8, HEAD_W)
                terms = []
                for s in range(8):
                    dec = jnp.exp(jnp.minimum(b3 - b3[:, s:s + 1, :], 0.0))
                    terms.append((q3 * dec * k3[:, s:s + 1, :]).reshape(c, HEAD_W).astype(BF16))
                sums = jnp.dot(jnp.concatenate(terms, axis=0), ones_sq, preferred_element_type=F32)
                for s in range(8):
                    a[it] = jnp.where(diag_masks[s], sums[s * c:(s + 1) * c, 0:c], a[it])

        o = [None] * n
        for h in range(HEADS):
            st = st_ref[h]
            for it in range(h, n, HEADS):
                qi = (q[it] * jnp.exp(b[it])).astype(BF16)
                o[it] = lax.dot_general(qi, st.astype(BF16), nt_dims, preferred_element_type=F32)
                st = st * jnp.exp(b_last[it]) + upd[it]
            st_ref[h] = st
        for it, (r0, cs, _) in enumerate(items):
            oh = o[it] + jnp.dot(a[it].astype(BF16), v_bf[it], preferred_element_type=F32)
            g = g_ref[pl.ds(r0, c), cs].astype(F32)
            o_ref[pl.ds(r0, c), cs] = (_rms(oh, gain) * g).astype(BF16)
        return carry

    tile = pl.program_id(0) * pl.num_programs(1) + pl.program_id(1)
    mild = lfmin_ref[tile] * -8.0 < HG_MAX_FACTORED_DECAY
    n_groups = HG_T // (c * HG_GROUP)

    @pl.when(mild)
    def _():
        lax.fori_loop(0, n_groups, functools.partial(chunk_group, factored=True), 0)

    @pl.when(jnp.logical_not(mild))
    def _():
        lax.fori_loop(0, n_groups, functools.partial(chunk_group, factored=False), 0)


def _hgrn(proj, lf, lfmin, gain, batch, seq):
    m = proj.shape[0]
    nt = seq // HG_T
    assert HG_T == IN_TM

    def spec(col):
        return pl.BlockSpec((HG_T, MIX_W), lambda b, t: (b * nt + t, col))

    return pl.pallas_call(
        _hgrn_kernel,
        grid=(batch, nt),
        in_specs=[pl.BlockSpec(memory_space=pltpu.SMEM),
                  spec(0), spec(0), spec(1), spec(2), spec(3),
                  pl.BlockSpec((1, HEAD_W), lambda b, t: (0, 0))],
        out_specs=spec(0),
        out_shape=jax.ShapeDtypeStruct((m, MIX_W), BF16),
        scratch_shapes=[pltpu.VMEM((HEADS, HEAD_W, HEAD_W), F32)],
        compiler_params=pltpu.CompilerParams(dimension_semantics=("parallel", "arbitrary"),
                                             vmem_limit_bytes=VMEM_LIMIT),
        name="hgrn2",
    )(lfmin, proj, lf, proj, proj, proj, gain)


ATT_HPG = 2


def _attn_kernel(lam_ref, slope_ref, qt_ref, k_ref, vt_ref, gain_ref, o_ref,
                 kaug_ref, q2t_ref, acc_ref, sa_ref, sb_ref, pa_ref, pb_ref, stat_ref):
    tq, r, tk = ATT_TQ, 2 * ATT_TQ, ATT_TK
    grp = pl.program_id(1)
    nblk = k_ref.shape[0] // tk
    slab = 64
    M_ROW, L_ROW, ALPHA_ROW, BMAX_ROW = (slice(k, k + 1) for k in range(4))

    def col_reduce(x, op):
        return op(x.reshape(slab // 8, 8, r), axis=0)

    def head(hh, i):
        cols = slice(hh * HEAD_W, (hh + 1) * HEAD_W)
        kaug, q2t, acc, stat = kaug_ref.at[hh], q2t_ref.at[hh], acc_ref.at[hh], stat_ref.at[hh]
        sa, sb, pa, pb = sa_ref.at[hh], sb_ref.at[hh], pa_ref.at[hh], pb_ref.at[hh]
        slope = slope_ref[grp * ATT_HPG + hh] * LOG2E

        def fill(jb):
            r0 = pl.multiple_of(jb * tk, tk)
            kaug[pl.ds(r0, tk), 0:HEAD_W] = k_ref[pl.ds(r0, tk), cols]
            row = lax.broadcasted_iota(jnp.int32, (tk, HEAD_W), 0) + jb * tk
            lane = lax.broadcasted_iota(jnp.int32, (tk, HEAD_W), 1)
            c = slope * row.astype(F32)
            c1 = c.astype(BF16).astype(F32)
            c2 = (c - c1).astype(BF16).astype(F32)
            c3 = c - c1 - c2
            bias = jnp.where(lane == 0, c1, jnp.where(lane == 1, c2, jnp.where(lane == 2, c3, 0.0)))
            kaug[pl.ds(r0, tk), HEAD_W:2 * HEAD_W] = bias.astype(BF16)

        def scores(j):
            r0 = pl.multiple_of(j * tk, tk)
            return jnp.dot(kaug[pl.ds(r0, tk), :], q2t[...], preferred_element_type=F32)

        def store_scores(s, s_out):
            part = jnp.full((8, r), NEG_BIG, F32)
            for a in range(0, tk, slab):
                sl = s[a:a + slab, :]
                s_out[a:a + slab, :] = sl
                part = jnp.maximum(part, col_reduce(sl, jnp.max))
            return jnp.max(part, axis=0, keepdims=True)

        def softmax(s_of, p_out, bmax, m_old, l_old):
            m_new = jnp.maximum(m_old, bmax)
            alpha = jnp.exp2(m_old - m_new)
            part = jnp.zeros((8, r), F32)
            for a in range(0, tk, slab):
                p = jnp.exp2(s_of(slice(a, a + slab)) - m_new)
                part = part + col_reduce(p, jnp.sum)
                p_out[a:a + slab, :] = p.astype(BF16)
            l_new = alpha * l_old + jnp.sum(part, axis=0, keepdims=True)
            return alpha, m_new, l_new

        def add_values(jv, p, alpha):
            acc[...] = alpha * acc[...] + jnp.dot(vt_ref[jv, cols, :], p, preferred_element_type=F32)

        def start():
            qt = qt_ref[i, cols, :].astype(F32)
            sub = lax.broadcasted_iota(jnp.int32, (HEAD_W, tq), 0)
            q2t[0:HEAD_W, 0:tq] = jnp.where(sub < DQK, qt, 0.0).astype(BF16)
            q2t[0:HEAD_W, tq:r] = jnp.where(sub >= DQK, qt, 0.0).astype(BF16)
            sub_r = lax.broadcasted_iota(jnp.int32, (HEAD_W, r), 0)
            q2t[HEAD_W:2 * HEAD_W, :] = jnp.where(sub_r < 3, 1.0, 0.0).astype(BF16)
            acc[...] = jnp.zeros((HEAD_W, r), F32)
            pb[...] = jnp.zeros((tk, r), BF16)
            stat[BMAX_ROW, :] = store_scores(scores(0), sa)
            stat[M_ROW, :] = jnp.full((1, r), NEG_BIG, F32)
            stat[L_ROW, :] = jnp.zeros((1, r), F32)
            stat[ALPHA_ROW, :] = jnp.ones((1, r), F32)

        def stage(j, flip):
            s_in, s_out, p_in, p_out = (sb, sa, pa, pb) if flip else (sa, sb, pb, pa)
            bmax_next = store_scores(scores(j + 1), s_out)
            add_values(jnp.maximum(j - 1, 0), p_in[...], stat[ALPHA_ROW, :])
            alpha, m_new, l_new = softmax(lambda rows: s_in[rows, :], p_out, stat[BMAX_ROW, :],
                                          stat[M_ROW, :], stat[L_ROW, :])
            stat[M_ROW, :] = m_new
            stat[L_ROW, :] = l_new
            stat[ALPHA_ROW, :] = alpha
            stat[BMAX_ROW, :] = bmax_next

        def finish(j_last, flip):
            s_in, p_in, p_free = (sb, pa, pb) if flip else (sa, pb, pa)
            ql = lax.broadcasted_iota(jnp.int32, (slab, r), 1)
            lead = i * tq - j_last * tk + jnp.where(ql >= tq, ql - tq, ql)
            row = lax.broadcasted_iota(jnp.int32, (slab, r), 0)

            part = jnp.full((8, r), NEG_BIG, F32)
            for a in range(0, tk, slab):
                sl = jnp.where(row + a <= lead, s_in[a:a + slab, :], NEG_BIG)
                s_in[a:a + slab, :] = sl
                part = jnp.maximum(part, col_reduce(sl, jnp.max))
            add_values(jnp.maximum(j_last - 1, 0), p_in[...], stat[ALPHA_ROW, :])
            alpha, _, l_fin = softmax(lambda rows: s_in[rows, :], p_free,
                                      jnp.max(part, axis=0, keepdims=True),
                                      stat[M_ROW, :], stat[L_ROW, :])
            add_values(j_last, p_free[...], alpha)
            on = acc[...] * (1.0 / l_fin)
            ot = on[:, 0:tq] - lam_ref[0] * on[:, tq:r]
            ot = ot * lax.rsqrt(jnp.mean(ot * ot, axis=0, keepdims=True) + EPS)
            o_ref[pl.ds(pl.multiple_of(i * tq, tq), tq), cols] = (ot.T * gain_ref[...]).astype(BF16)

        return fill, start, stage, finish

    def fill_all(jb, carry):
        for hh in range(ATT_HPG):
            head(hh, 0)[0](jb)
        return carry
    lax.fori_loop(0, nblk, fill_all, 0)

    def query_tile(i, carry):
        heads = [head(hh, i) for hh in range(ATT_HPG)]
        for _, start, _, _ in heads:
            start()

        def pair(t, carry):
            for flip in (False, True):
                for _, _, stage, _ in heads:
                    stage(2 * t + int(flip), flip)
            return carry

        n_full = (i * tq) // tk
        lax.fori_loop(0, n_full // 2, pair, 0)

        @pl.when(n_full % 2 == 0)
        def _():
            for _, _, _, finish in heads:
                finish(n_full, False)

        @pl.when(n_full % 2 == 1)
        def _():
            for _, _, stage, _ in heads:
                stage(n_full - 1, False)
            for _, _, _, finish in heads:
                finish(n_full, True)
        return carry

    lax.fori_loop(0, qt_ref.shape[0], query_tile, 0)


def _attn(lam, proj, qt, vt, gain, batch, seq):
    m = proj.shape[0]
    slopes = jnp.exp2(-(ALIBI_MAX_BIAS / HEADS) * jnp.arange(1, HEADS + 1, dtype=F32))
    assert ATT_TQ == IN_TM and ATT_TK == IN_TM
    nq = seq // ATT_TQ
    nk = seq // ATT_TK
    gw = ATT_HPG * HEAD_W
    kcol = KB_TILE * (MIX_W // gw)
    hpg, r = ATT_HPG, 2 * ATT_TQ
    return pl.pallas_call(
        _attn_kernel,
        grid=(batch, HEADS // ATT_HPG),
        in_specs=[pl.BlockSpec(memory_space=pltpu.SMEM),
                  pl.BlockSpec(memory_space=pltpu.SMEM),
                  pl.BlockSpec((nq, gw, ATT_TQ), lambda b, g: (b, g, 0), pipeline_mode=pl.Buffered(1)),
                  pl.BlockSpec((seq, gw), lambda b, g: (b, kcol + g), pipeline_mode=pl.Buffered(1)),
                  pl.BlockSpec((nk, gw, ATT_TK), lambda b, g: (b, g, 0), pipeline_mode=pl.Buffered(1)),
                  pl.BlockSpec((1, HEAD_W), lambda b, g: (0, 0))],
        out_specs=pl.BlockSpec((seq, gw), lambda b, g: (b, g)),
        out_shape=jax.ShapeDtypeStruct((m, MIX_W), BF16),
        scratch_shapes=[pltpu.VMEM((hpg, seq, 2 * HEAD_W), BF16),
                        pltpu.VMEM((hpg, 2 * HEAD_W, r), BF16),
                        pltpu.VMEM((hpg, HEAD_W, r), F32),
                        pltpu.VMEM((hpg, ATT_TK, r), F32),
                        pltpu.VMEM((hpg, ATT_TK, r), F32),
                        pltpu.VMEM((hpg, ATT_TK, r), BF16),
                        pltpu.VMEM((hpg, ATT_TK, r), BF16),
                        pltpu.VMEM((hpg, 8, r), F32)],
        compiler_params=pltpu.CompilerParams(dimension_semantics=("parallel", "parallel"),
                                             vmem_limit_bytes=VMEM_LIMIT),
        name="diffattn",
    )(lam, slopes, qt, proj, vt, gain)


def _channel_kernel(ya_ref, yb_ref, ga0_ref, ga1_ref, gb0_ref, gb1_ref, x_ref,
                    wua_ref, wub_ref, wo_ref, wg_ref, wu_ref, wd_ref,
                    gmix_ref, gpre_ref, gpost_ref, gnext_ref, x2_ref, *maybe_hn_ref):
    ua = jnp.dot(ya_ref[...], wua_ref[...], preferred_element_type=F32)
    ub = jnp.dot(yb_ref[...], wub_ref[...], preferred_element_type=F32)
    ga = jnp.concatenate([ga0_ref[...], ga1_ref[...]], axis=1).astype(F32)
    gb = jnp.concatenate([gb0_ref[...], gb1_ref[...]], axis=1).astype(F32)
    merged = (ga * ua + gb * ub).astype(BF16)
    mix = jnp.dot(merged, wo_ref[...], preferred_element_type=F32)
    x1 = x_ref[...] + _rms(mix, gmix_ref[...])
    h = _rms(x1, gpre_ref[...]).astype(BF16)

    acc = jnp.zeros((CH_TM, D_MODEL), F32)
    for c in range(D_FF // FF_TF):
        cs = slice(c * FF_TF, (c + 1) * FF_TF)
        g = jnp.dot(h, wg_ref[:, cs], preferred_element_type=F32)
        u = jnp.dot(h, wu_ref[:, cs], preferred_element_type=F32)
        ff = (g * _sigmoid(g) * u).astype(BF16)
        acc = acc + jnp.dot(ff, wd_ref[cs, :], preferred_element_type=F32)
    x2 = x1 + _rms(acc, gpost_ref[...])
    x2_ref[...] = x2
    for hn_ref in maybe_hn_ref:
        hn_ref[...] = _rms(x2, gnext_ref[...]).astype(BF16)


def _channel(ya, yb, proj, x2d, wua, wub, wo, wg, wu, wd, layer, gmix, gpre, gpost, gnext, emit_next):
    m = ya.shape[0]
    tm = CH_TM

    def row(width, col=0):
        return pl.BlockSpec((tm, width), lambda i: (i, col))

    out_specs = [row(D_MODEL)] + ([row(D_MODEL)] if emit_next else [])
    out_shape = ([jax.ShapeDtypeStruct((m, D_MODEL), F32)]
                 + ([jax.ShapeDtypeStruct((m, D_MODEL), BF16)] if emit_next else []))
    return pl.pallas_call(
        _channel_kernel,
        grid=(m // tm,),
        in_specs=[row(MIX_W), row(MIX_W)] + [row(MIX_W, GATE_TILE + g) for g in range(4)] + [
                  row(D_MODEL),
                  _resident((MIX_W, D_MODEL), layer), _resident((MIX_W, D_MODEL), layer),
                  _resident((D_MODEL, D_MODEL), layer),
                  _resident((D_MODEL, D_FF), layer), _resident((D_MODEL, D_FF), layer),
                  _resident((D_FF, D_MODEL), layer),
                  _resident((1, D_MODEL)), _resident((1, D_MODEL)), _resident((1, D_MODEL)),
                  _resident((1, D_MODEL))],
        out_specs=out_specs,
        out_shape=out_shape,
        compiler_params=pltpu.CompilerParams(dimension_semantics=("parallel",),
                                             vmem_limit_bytes=VMEM_LIMIT),
        name="channel",
    )(ya, yb, proj, proj, proj, proj, x2d, wua, wub, wo, wg, wu, wd, gmix, gpre, gpost, gnext)


def kernel(x, lower_bounds, norm_mix_pre, norm_mix_post, norm_ffn_pre, norm_ffn_post, w_in, hg_out_norm,
           da_subln, lambda_q1, lambda_k1, lambda_q2, lambda_k2, w_up_a, w_up_b, w_out, w_ffn_gate,
           w_ffn_up, w_ffn_down):
    batch, seq, _ = x.shape
    depth = w_in.shape[0]
    m = batch * seq
    x2d = x.reshape(m, D_MODEL).astype(F32)

    lb_all = jnp.cumsum(jax.nn.softmax(lower_bounds.astype(F32), axis=0), axis=0)
    lb_all = lb_all - lb_all[0:1]

    w_in, w_up_a, w_up_b, w_out, w_ffn_gate, w_ffn_up, w_ffn_down = (
        w.astype(BF16) for w in (w_in, w_up_a, w_up_b, w_out, w_ffn_gate, w_ffn_up, w_ffn_down))

    h = None
    for l in range(depth):
        lb = lb_all[l]
        lbp = jnp.zeros((8, MIX_W), F32)
        lbp = lbp.at[0].set(jnp.maximum(jnp.log(lb), NEG_BIG)).at[1].set(jnp.log1p(-lb)).at[2].set(1.0 - lb)
        proj, lf, lfmin, qt, vt = _in_proj(x2d if h is None else h, norm_mix_pre[l].reshape(1, D_MODEL),
                                           w_in, l, lbp, prenorm=h is None)

        ya = _hgrn(proj, lf, lfmin[:, 0, 0], hg_out_norm[l].reshape(1, HEAD_W), batch, seq)

        lam_init = 0.8 - 0.6 * math.exp(-0.3 * l)
        lam = (jnp.exp(jnp.sum(lambda_q1[l].astype(F32) * lambda_k1[l].astype(F32)))
               - jnp.exp(jnp.sum(lambda_q2[l].astype(F32) * lambda_k2[l].astype(F32))) + lam_init)
        sub_gain = (da_subln[l].astype(F32) * (1.0 - lam_init)).reshape(1, HEAD_W)
        yb = _attn(lam.reshape(1), proj, qt, vt, sub_gain, batch, seq)

        last = l == depth - 1
        g_next = norm_mix_pre[(l + 1) % depth].reshape(1, D_MODEL)
        outs = _channel(ya, yb, proj, x2d, w_up_a, w_up_b, w_out, w_ffn_gate, w_ffn_up, w_ffn_down, l,
                        norm_mix_post[l].reshape(1, D_MODEL),
                        norm_ffn_pre[l].reshape(1, D_MODEL), norm_ffn_post[l].reshape(1, D_MODEL),
                        g_next, emit_next=not last)
        x2d, h = (outs[0], None) if last else outs
    return x2d.reshape(batch, seq, D_MODEL)
```

```python
import functools
import math

import jax
import jax.numpy as jnp
from jax import lax
from jax.experimental import pallas as pl
from jax.experimental.pallas import tpu as pltpu

F32 = jnp.float32
BF16 = jnp.bfloat16

D_MODEL = 1024
HEADS = 4
HEAD_W = 128
MIX_W = HEADS * HEAD_W
DQK = 64
D_FF = 2816
D_IN = 7 * MIX_W + 2 * D_MODEL
EPS = 1e-6
ALIBI_MAX_BIAS = 8.0
NEG_BIG = -1e30
LOG2E = math.log2(math.e)

IN_TM, IN_TN = 512, 512
HG_T, HG_C = 512, 64
HG_GROUP = 8
HG_MAX_FACTORED_DECAY = 80.0
ATT_TQ, ATT_TK = 512, 512
CH_TM, FF_TF = 512, 256
VMEM_LIMIT = 56 * 1024 * 1024


def _sigmoid(x):
    return 1.0 / (1.0 + jnp.exp(-x))


def _rms(x, gain):
    return x * lax.rsqrt(jnp.mean(x * x, axis=-1, keepdims=True) + EPS) * gain


def _resident(shape, layer=None):
    if layer is None:
        return pl.BlockSpec(shape, lambda *_: (0,) * len(shape), pipeline_mode=pl.Buffered(1))
    return pl.BlockSpec((None,) + tuple(shape), lambda *_: (layer,) + (0,) * len(shape),
                        pipeline_mode=pl.Buffered(1))


PROJ_TILES = 9
PROJ_W = PROJ_TILES * 512
KB_TILE, GATE_TILE = 4, 5


def _in_proj_kernel(x_ref, g_ref, w_ref, lb_ref, o_ref, lf_ref, lfmin_ref, qt_ref, vt_ref, h_ref, *,
                    prenorm):
    if prenorm:
        h_ref[...] = _rms(x_ref[...], g_ref[...]).astype(BF16)
    else:
        h_ref[...] = x_ref[...]

    def silu(acc):
        return acc * _sigmoid(acc)

    def forget_gate(acc):
        log_lb, log_1m_lb, one_m_lb = lb_ref[0:1, :], lb_ref[1:2, :], lb_ref[2:3, :]
        e = jnp.exp(-jnp.abs(acc))
        log_sig = jnp.minimum(acc, 0.0) - jnp.log(1.0 + e)
        c = log_1m_lb + log_sig
        hi = jnp.maximum(log_lb, c)
        lf = hi + jnp.log(1.0 + jnp.exp(-jnp.abs(log_lb - c)))
        lf_ref[...] = lf
        lfmin_ref[0] = jnp.full((8, HEAD_W), jnp.min(lf), F32)
        return one_m_lb * (jnp.where(acc >= 0.0, e, 1.0) / (1.0 + e))

    def query_scale(acc):
        return acc * (LOG2E / math.sqrt(DQK))

    def column_tile(jt):
        return jnp.dot(h_ref[...], w_ref[:, jt * IN_TN:(jt + 1) * IN_TN], preferred_element_type=F32)

    plan = [(1, forget_gate, 1), (0, silu, 0), (3, silu, 3)]
    plan += [(7 + g, _sigmoid, GATE_TILE + g) for g in range(4)]
    tail = [(2, None, 2), (5, None, KB_TILE)]
    for jt, epilogue, out_tile in plan:
        o_ref[:, out_tile * IN_TN:(out_tile + 1) * IN_TN] = epilogue(column_tile(jt)).astype(BF16)
    qt_ref[0] = query_scale(column_tile(4)).T.astype(BF16)
    vt_ref[0] = column_tile(6).T.astype(BF16)
    for jt, _, out_tile in tail:
        o_ref[:, out_tile * IN_TN:(out_tile + 1) * IN_TN] = column_tile(jt).astype(BF16)


def _in_proj(x_or_h, gain, w_all, layer, lbp, prenorm):
    m = x_or_h.shape[0]
    nt = m // IN_TM
    return pl.pallas_call(
        functools.partial(_in_proj_kernel, prenorm=prenorm),
        grid=(nt,),
        in_specs=[pl.BlockSpec((IN_TM, D_MODEL), lambda i: (i, 0)),
                  _resident((1, D_MODEL)), _resident((D_MODEL, D_IN), layer), _resident((8, MIX_W))],
        out_specs=[pl.BlockSpec((IN_TM, PROJ_W), lambda i: (i, 0)),
                   pl.BlockSpec((IN_TM, MIX_W), lambda i: (i, 0)),
                   pl.BlockSpec((1, 8, HEAD_W), lambda i: (i, 0, 0)),
                   pl.BlockSpec((1, MIX_W, IN_TM), lambda i: (i, 0, 0)),
                   pl.BlockSpec((1, MIX_W, IN_TM), lambda i: (i, 0, 0))],
        out_shape=[jax.ShapeDtypeStruct((m, PROJ_W), BF16),
                   jax.ShapeDtypeStruct((m, MIX_W), F32),
                   jax.ShapeDtypeStruct((nt, 8, HEAD_W), F32),
                   jax.ShapeDtypeStruct((nt, MIX_W, IN_TM), BF16),
                   jax.ShapeDtypeStruct((nt, MIX_W, IN_TM), BF16)],
        scratch_shapes=[pltpu.VMEM((IN_TM, D_MODEL), BF16)],
        compiler_params=pltpu.CompilerParams(dimension_semantics=("parallel",),
                                             vmem_limit_bytes=VMEM_LIMIT),
        name="in_proj",
    )(x_or_h, gain, w_all, lbp)


def _row_bcast(b, rows, w):
    return jnp.concatenate([jnp.broadcast_to(b[r:r + 1, :], (w, HEAD_W)) for r in rows], axis=0)


def _hgrn_kernel(lfmin_ref, q_ref, lf_ref, k_ref, v_ref, g_ref, gain_ref, o_ref, st_ref):
    @pl.when(pl.program_id(1) == 0)
    def _():
        st_ref[...] = jnp.zeros_like(st_ref)

    c = HG_C
    ri = lax.broadcasted_iota(jnp.int32, (c, c), 0)
    ci = lax.broadcasted_iota(jnp.int32, (c, c), 1)
    tri = (ci <= ri).astype(BF16)
    level_masks = []
    for w in (32, 16, 8):
        tb, sb = ri // w, ci // w
        level_masks.append((tb == sb + 1) & (tb % 2 == 1))
    diag_masks = [(ci == (ri // 8) * 8 + s) & (ri % 8 >= s) for s in range(8)]
    same_block_mask = (ri // 8 == ci // 8) & (ci <= ri)
    ones_sq = jnp.ones((HEAD_W, HEAD_W), BF16)
    gain = gain_ref[...]

    def chunk_group(ig, carry, factored):
        nt_dims = (((1,), (1,)), ((), ()))
        items = [(pl.multiple_of((ig * HG_GROUP + ch) * c, c), slice(h * HEAD_W, (h + 1) * HEAD_W), h)
                 for ch in range(HG_GROUP) for h in range(HEADS)]
        n = len(items)
        q = [q_ref[pl.ds(r0, c), cs].astype(F32) for r0, cs, _ in items]
        k = [k_ref[pl.ds(r0, c), cs].astype(F32) for r0, cs, _ in items]
        v_bf = [v_ref[pl.ds(r0, c), cs] for r0, cs, _ in items]

        b = []
        for r0, cs, _ in items:
            lf = lf_ref[pl.ds(r0, c), cs]
            lf_hi = lf.astype(BF16)
            lf_lo = (lf - lf_hi.astype(F32)).astype(BF16)
            b.append(jnp.dot(tri, lf_hi, preferred_element_type=F32)
                     + jnp.dot(tri, lf_lo, preferred_element_type=F32))
        b_last = [bh[c - 1:c, :] for bh in b]

        upd = []
        for it in range(n):
            ks = (k[it] * jnp.exp(b_last[it] - b[it])).astype(BF16)
            upd.append(lax.dot_general(v_bf[it], ks, (((0,), (0,)), ((), ())), preferred_element_type=F32))
        a = [jnp.zeros((c, c), F32) for _ in items]
        for w, msk in zip((32, 16, 8), level_masks):
            nb = c // w
            for it in range(n):
                start = _row_bcast(b[it], [j * w for j in range(nb)], w)
                nxt = _row_bcast(b[it], [(j + 1) * w for j in range(nb - 1)] + [c - 1], w)
                qh = (q[it] * jnp.exp(b[it] - start)).astype(BF16)
                kh = (k[it] * jnp.exp(nxt - b[it])).astype(BF16)
                aw = lax.dot_general(qh, kh, nt_dims, preferred_element_type=F32)
                a[it] = jnp.where(msk, aw, a[it])
                if factored and w == 8:
                    kb = (k[it] * jnp.exp(start - b[it])).astype(BF16)
                    aw = lax.dot_general(qh, kb, nt_dims, preferred_element_type=F32)
                    a[it] = jnp.where(same_block_mask, aw, a[it])

        if not factored:
            for it in range(n):
                b3 = b[it].reshape(c // 8, 8, HEAD_W)
                q3 = q[it].reshape(c // 8, 8, HEAD_W)
                k3 = k[it].reshape(c // 8, 8, HEAD_W)
                terms = []
                for s in range(8):
                    dec = jnp.exp(jnp.minimum(b3 - b3[:, s:s + 1, :], 0.0))
                    terms.append((q3 * dec * k3[:, s:s + 1, :]).reshape(c, HEAD_W).astype(BF16))
                sums = jnp.dot(jnp.concatenate(terms, axis=0), ones_sq, preferred_element_type=F32)
                for s in range(8):
                    a[it] = jnp.where(diag_masks[s], sums[s * c:(s + 1) * c, 0:c], a[it])

        o = [None] * n
        for h in range(HEADS):
            st = st_ref[h]
            for it in range(h, n, HEADS):
                qi = (q[it] * jnp.exp(b[it])).astype(BF16)
                o[it] = lax.dot_general(qi, st.astype(BF16), nt_dims, preferred_element_type=F32)
                st = st * jnp.exp(b_last[it]) + upd[it]
            st_ref[h] = st
        for it, (r0, cs, _) in enumerate(items):
            oh = o[it] + jnp.dot(a[it].astype(BF16), v_bf[it], preferred_element_type=F32)
            g = g_ref[pl.ds(r0, c), cs].astype(F32)
            o_ref[pl.ds(r0, c), cs] = (_rms(oh, gain) * g).astype(BF16)
        return carry

    tile = pl.program_id(0) * pl.num_programs(1) + pl.program_id(1)
    mild = lfmin_ref[tile] * -8.0 < HG_MAX_FACTORED_DECAY
    n_groups = HG_T // (c * HG_GROUP)

    @pl.when(mild)
    def _():
        lax.fori_loop(0, n_groups, functools.partial(chunk_group, factored=True), 0)

    @pl.when(jnp.logical_not(mild))
    def _():
        lax.fori_loop(0, n_groups, functools.partial(chunk_group, factored=False), 0)


def _hgrn(proj, lf, lfmin, gain, batch, seq):
    m = proj.shape[0]
    nt = seq // HG_T
    assert HG_T == IN_TM

    def spec(col):
        return pl.BlockSpec((HG_T, MIX_W), lambda b, t: (b * nt + t, col))

    return pl.pallas_call(
        _hgrn_kernel,
        grid=(batch, nt),
        in_specs=[pl.BlockSpec(memory_space=pltpu.SMEM),
                  spec(0), spec(0), spec(1), spec(2), spec(3),
                  pl.BlockSpec((1, HEAD_W), lambda b, t: (0, 0))],
        out_specs=spec(0),
        out_shape=jax.ShapeDtypeStruct((m, MIX_W), BF16),
        scratch_shapes=[pltpu.VMEM((HEADS, HEAD_W, HEAD_W), F32)],
        compiler_params=pltpu.CompilerParams(dimension_semantics=("parallel", "arbitrary"),
                                             vmem_limit_bytes=VMEM_LIMIT),
        name="hgrn2",
    )(lfmin, proj, lf, proj, proj, proj, gain)


ATT_HPG = 2
ATT_MAX_SCORE_BOUND = 60.0


def _attn_kernel(lam_ref, slope_ref, qt_ref, k_ref, vt_ref, gain_ref, o_ref,
                 kaug_ref, q2t_ref, acc_ref, sa_ref, sb_ref, pa_ref, pb_ref, stat_ref, lsum_ref):
    tq, r, tk = ATT_TQ, 2 * ATT_TQ, ATT_TK
    grp = pl.program_id(1)
    nblk = k_ref.shape[0] // tk
    slab = 64
    M_ROW, L_ROW, ALPHA_ROW, BMAX_ROW, KNORM_ROW, MREF_ROW = (slice(k, k + 1) for k in range(6))

    def col_reduce(x, op):
        return op(x.reshape(slab // 8, 8, r), axis=0)

    def head(hh, i):
        cols = slice(hh * HEAD_W, (hh + 1) * HEAD_W)
        kaug, q2t, acc, stat = kaug_ref.at[hh], q2t_ref.at[hh], acc_ref.at[hh], stat_ref.at[hh]
        sa, sb, pa, pb = sa_ref.at[hh], sb_ref.at[hh], pa_ref.at[hh], pb_ref.at[hh]
        lsum = lsum_ref.at[hh]
        slope = slope_ref[grp * ATT_HPG + hh] * LOG2E

        def fill(jb, knorm_sq):
            r0 = pl.multiple_of(jb * tk, tk)
            k_blk = k_ref[pl.ds(r0, tk), cols]
            ksq = (k_blk.astype(F32) * k_blk.astype(F32)).astype(BF16)
            di = lax.broadcasted_iota(jnp.int32, (HEAD_W, HEAD_W), 0)
            ci = lax.broadcasted_iota(jnp.int32, (HEAD_W, HEAD_W), 1)
            same_comp = ((di < DQK) == (ci < DQK)).astype(BF16)
            sums = jnp.dot(ksq, same_comp, preferred_element_type=F32)
            knorm_sq = jnp.maximum(knorm_sq, jnp.max(sums, axis=0, keepdims=True))
            kaug[pl.ds(r0, tk), 0:HEAD_W] = k_blk
            row = lax.broadcasted_iota(jnp.int32, (tk, HEAD_W), 0) + jb * tk
            lane = lax.broadcasted_iota(jnp.int32, (tk, HEAD_W), 1)
            c = slope * row.astype(F32)
            c1 = c.astype(BF16).astype(F32)
            c2 = (c - c1).astype(BF16).astype(F32)
            c3 = c - c1 - c2
            bias = jnp.where(lane == 0, c1, jnp.where(lane == 1, c2, jnp.where(lane == 2, c3, 0.0)))
            kaug[pl.ds(r0, tk), HEAD_W:2 * HEAD_W] = bias.astype(BF16)
            return knorm_sq

        def set_key_norms(knorm_sq):
            knorm = jnp.sqrt(knorm_sq * 1.02)
            stat[KNORM_ROW, :] = jnp.concatenate([jnp.broadcast_to(knorm[:, 0:1], (1, tq)),
                                                  jnp.broadcast_to(knorm[:, DQK:DQK + 1], (1, tq))], axis=1)

        def scores(j):
            r0 = pl.multiple_of(j * tk, tk)
            return jnp.dot(kaug[pl.ds(r0, tk), :], q2t[...], preferred_element_type=F32)

        def store_scores(s, s_out):
            part = jnp.full((8, r), NEG_BIG, F32)
            for a in range(0, tk, slab):
                sl = s[a:a + slab, :]
                s_out[a:a + slab, :] = sl
                part = jnp.maximum(part, col_reduce(sl, jnp.max))
            return jnp.max(part, axis=0, keepdims=True)

        def softmax(s_of, p_out, bmax, m_old, l_old):
            m_new = jnp.maximum(m_old, bmax)
            alpha = jnp.exp2(m_old - m_new)
            part = jnp.zeros((8, r), F32)
            for a in range(0, tk, slab):
                p = jnp.exp2(s_of(slice(a, a + slab)) - m_new)
                part = part + col_reduce(p, jnp.sum)
                p_out[a:a + slab, :] = p.astype(BF16)
            l_new = alpha * l_old + jnp.sum(part, axis=0, keepdims=True)
            return alpha, m_new, l_new

        def add_values(jv, p, alpha):
            acc[...] = alpha * acc[...] + jnp.dot(vt_ref[jv, cols, :], p, preferred_element_type=F32)

        def load_queries():
            qt = qt_ref[i, cols, :].astype(F32)
            sub = lax.broadcasted_iota(jnp.int32, (HEAD_W, tq), 0)
            q2t[0:HEAD_W, 0:tq] = jnp.where(sub < DQK, qt, 0.0).astype(BF16)
            q2t[0:HEAD_W, tq:r] = jnp.where(sub >= DQK, qt, 0.0).astype(BF16)
            sub_r = lax.broadcasted_iota(jnp.int32, (HEAD_W, r), 0)
            q2t[HEAD_W:2 * HEAD_W, :] = jnp.where(sub_r < 3, 1.0, 0.0).astype(BF16)
            qsq = qt * qt
            qnorm = jnp.sqrt(jnp.concatenate([jnp.sum(qsq[0:DQK, :], axis=0, keepdims=True),
                                              jnp.sum(qsq[DQK:HEAD_W, :], axis=0, keepdims=True)], axis=1))
            bound = qnorm * stat[KNORM_ROW, :]
            ql = lax.broadcasted_iota(jnp.int32, (1, r), 1)
            q_pos = (i * tq + jnp.where(ql >= tq, ql - tq, ql)).astype(F32)
            stat[MREF_ROW, :] = bound + slope * q_pos + 1.0
            return bound

        def bounded_start():
            acc[...] = jnp.zeros((HEAD_W, r), F32)
            pb[...] = jnp.zeros((tk, r), BF16)
            lsum[...] = jnp.zeros((8, r), F32)

        def bounded_block(j, p_out, masked):
            s = scores(j)
            mref = stat[MREF_ROW, :]
            if masked:
                ql = lax.broadcasted_iota(jnp.int32, (slab, r), 1)
                lead = i * tq - j * tk + jnp.where(ql >= tq, ql - tq, ql)
                row = lax.broadcasted_iota(jnp.int32, (slab, r), 0)
            part = lsum[...]
            for a in range(0, tk, slab):
                e = s[a:a + slab, :] - mref
                if masked:
                    e = jnp.where(row + a <= lead, e, NEG_BIG)
                p = jnp.exp2(e)
                part = part + col_reduce(p, jnp.sum)
                p_out[a:a + slab, :] = p.astype(BF16)
            lsum[...] = part

        def bounded_values(jv, p_in):
            acc[...] = acc[...] + jnp.dot(vt_ref[jv, cols, :], p_in[...], preferred_element_type=F32)

        def bounded_finish():
            write_output(jnp.sum(lsum[...], axis=0, keepdims=True))

        def write_output(l_fin):
            on = acc[...] * (1.0 / l_fin)
            ot = on[:, 0:tq] - lam_ref[0] * on[:, tq:r]
            ot = ot * lax.rsqrt(jnp.mean(ot * ot, axis=0, keepdims=True) + EPS)
            o_ref[pl.ds(pl.multiple_of(i * tq, tq), tq), cols] = (ot.T * gain_ref[...]).astype(BF16)

        def start():
            acc[...] = jnp.zeros((HEAD_W, r), F32)
            pb[...] = jnp.zeros((tk, r), BF16)
            stat[BMAX_ROW, :] = store_scores(scores(0), sa)
            stat[M_ROW, :] = jnp.full((1, r), NEG_BIG, F32)
            stat[L_ROW, :] = jnp.zeros((1, r), F32)
            stat[ALPHA_ROW, :] = jnp.ones((1, r), F32)

        def stage(j, flip):
            s_in, s_out, p_in, p_out = (sb, sa, pa, pb) if flip else (sa, sb, pb, pa)
            bmax_next = store_scores(scores(j + 1), s_out)
            add_values(jnp.maximum(j - 1, 0), p_in[...], stat[ALPHA_ROW, :])
            alpha, m_new, l_new = softmax(lambda rows: s_in[rows, :], p_out, stat[BMAX_ROW, :],
                                          stat[M_ROW, :], stat[L_ROW, :])
            stat[M_ROW, :] = m_new
            stat[L_ROW, :] = l_new
            stat[ALPHA_ROW, :] = alpha
            stat[BMAX_ROW, :] = bmax_next

        def finish(j_last, flip):
            s_in, p_in, p_free = (sb, pa, pb) if flip else (sa, pb, pa)
            ql = lax.broadcasted_iota(jnp.int32, (slab, r), 1)
            lead = i * tq - j_last * tk + jnp.where(ql >= tq, ql - tq, ql)
            row = lax.broadcasted_iota(jnp.int32, (slab, r), 0)

            part = jnp.full((8, r), NEG_BIG, F32)
            for a in range(0, tk, slab):
                sl = jnp.where(row + a <= lead, s_in[a:a + slab, :], NEG_BIG)
                s_in[a:a + slab, :] = sl
                part = jnp.maximum(part, col_reduce(sl, jnp.max))
            add_values(jnp.maximum(j_last - 1, 0), p_in[...], stat[ALPHA_ROW, :])
            alpha, _, l_fin = softmax(lambda rows: s_in[rows, :], p_free,
                                      jnp.max(part, axis=0, keepdims=True),
                                      stat[M_ROW, :], stat[L_ROW, :])
            add_values(j_last, p_free[...], alpha)
            write_output(l_fin)

        return dict(fill=fill, set_key_norms=set_key_norms, load_queries=load_queries,
                    start=start, stage=stage, finish=finish,
                    bounded_start=bounded_start, bounded_block=bounded_block,
                    bounded_values=bounded_values, bounded_finish=bounded_finish, pa=pa, pb=pb)

    def fill_all(jb, carry):
        return tuple(head(hh, 0)["fill"](jb, carry[hh]) for hh in range(ATT_HPG))
    knorm_sq = lax.fori_loop(0, nblk, fill_all, tuple(jnp.zeros((1, HEAD_W), F32) for _ in range(ATT_HPG)))
    for hh in range(ATT_HPG):
        head(hh, 0)["set_key_norms"](knorm_sq[hh])

    def query_tile(i, carry):
        heads = [head(hh, i) for hh in range(ATT_HPG)]
        n_full = (i * tq) // tk
        bound = functools.reduce(jnp.maximum, [h["load_queries"]() for h in heads])
        bounded = jnp.max(bound) < ATT_MAX_SCORE_BOUND

        def each(name, *args):
            for h in heads:
                h[name](*args)

        @pl.when(bounded)
        def _():
            each("bounded_start")

            def pair(t, carry):
                for h in heads:
                    h["bounded_block"](2 * t, h["pa"], False)
                for h in heads:
                    h["bounded_values"](jnp.maximum(2 * t - 1, 0), h["pb"])
                for h in heads:
                    h["bounded_block"](2 * t + 1, h["pb"], False)
                for h in heads:
                    h["bounded_values"](2 * t, h["pa"])
                return carry
            lax.fori_loop(0, n_full // 2, pair, 0)

            @pl.when(n_full % 2 == 0)
            def _():
                for h in heads:
                    h["bounded_block"](n_full, h["pa"], True)
                for h in heads:
                    h["bounded_values"](jnp.maximum(n_full - 1, 0), h["pb"])
                for h in heads:
                    h["bounded_values"](n_full, h["pa"])

            @pl.when(n_full % 2 == 1)
            def _():
                for h in heads:
                    h["bounded_block"](n_full - 1, h["pa"], False)
                for h in heads:
                    h["bounded_values"](jnp.maximum(n_full - 2, 0), h["pb"])
                for h in heads:
                    h["bounded_block"](n_full, h["pb"], True)
                for h in heads:
                    h["bounded_values"](n_full - 1, h["pa"])
                for h in heads:
                    h["bounded_values"](n_full, h["pb"])
            each("bounded_finish")

        @pl.when(jnp.logical_not(bounded))
        def _():
            each("start")

            def pair(t, carry):
                for flip in (False, True):
                    each("stage", 2 * t + int(flip), flip)
                return carry
            lax.fori_loop(0, n_full // 2, pair, 0)

            @pl.when(n_full % 2 == 0)
            def _():
                each("finish", n_full, False)

            @pl.when(n_full % 2 == 1)
            def _():
                each("stage", n_full - 1, False)
                each("finish", n_full, True)
        return carry

    lax.fori_loop(0, qt_ref.shape[0], query_tile, 0)


def _attn(lam, proj, qt, vt, gain, batch, seq):
    m = proj.shape[0]
    slopes = jnp.exp2(-(ALIBI_MAX_BIAS / HEADS) * jnp.arange(1, HEADS + 1, dtype=F32))
    assert ATT_TQ == IN_TM and ATT_TK == IN_TM
    nq = seq // ATT_TQ
    nk = seq // ATT_TK
    gw = ATT_HPG * HEAD_W
    kcol = KB_TILE * (MIX_W // gw)
    hpg, r = ATT_HPG, 2 * ATT_TQ
    return pl.pallas_call(
        _attn_kernel,
        grid=(batch, HEADS // ATT_HPG),
        in_specs=[pl.BlockSpec(memory_space=pltpu.SMEM),
                  pl.BlockSpec(memory_space=pltpu.SMEM),
                  pl.BlockSpec((nq, gw, ATT_TQ), lambda b, g: (b, g, 0), pipeline_mode=pl.Buffered(1)),
                  pl.BlockSpec((seq, gw), lambda b, g: (b, kcol + g), pipeline_mode=pl.Buffered(1)),
                  pl.BlockSpec((nk, gw, ATT_TK), lambda b, g: (b, g, 0), pipeline_mode=pl.Buffered(1)),
                  pl.BlockSpec((1, HEAD_W), lambda b, g: (0, 0))],
        out_specs=pl.BlockSpec((seq, gw), lambda b, g: (b, g)),
        out_shape=jax.ShapeDtypeStruct((m, MIX_W), BF16),
        scratch_shapes=[pltpu.VMEM((hpg, seq, 2 * HEAD_W), BF16),
                        pltpu.VMEM((hpg, 2 * HEAD_W, r), BF16),
                        pltpu.VMEM((hpg, HEAD_W, r), F32),
                        pltpu.VMEM((hpg, ATT_TK, r), F32),
                        pltpu.VMEM((hpg, ATT_TK, r), F32),
                        pltpu.VMEM((hpg, ATT_TK, r), BF16),
                        pltpu.VMEM((hpg, ATT_TK, r), BF16),
                        pltpu.VMEM((hpg, 8, r), F32),
                        pltpu.VMEM((hpg, 8, r), F32)],
        compiler_params=pltpu.CompilerParams(dimension_semantics=("parallel", "parallel"),
                                             vmem_limit_bytes=VMEM_LIMIT),
        name="diffattn",
    )(lam, slopes, qt, proj, vt, gain)


def _channel_kernel(ya_ref, yb_ref, ga0_ref, ga1_ref, gb0_ref, gb1_ref, x_ref,
                    wua_ref, wub_ref, wo_ref, wg_ref, wu_ref, wd_ref,
                    gmix_ref, gpre_ref, gpost_ref, gnext_ref, x2_ref, *maybe_hn_ref):
    ua = jnp.dot(ya_ref[...], wua_ref[...], preferred_element_type=F32)
    ub = jnp.dot(yb_ref[...], wub_ref[...], preferred_element_type=F32)
    ga = jnp.concatenate([ga0_ref[...], ga1_ref[...]], axis=1).astype(F32)
    gb = jnp.concatenate([gb0_ref[...], gb1_ref[...]], axis=1).astype(F32)
    merged = (ga * ua + gb * ub).astype(BF16)
    mix = jnp.dot(merged, wo_ref[...], preferred_element_type=F32)
    x1 = x_ref[...] + _rms(mix, gmix_ref[...])
    h = _rms(x1, gpre_ref[...]).astype(BF16)

    acc = jnp.zeros((CH_TM, D_MODEL), F32)
    for c in range(D_FF // FF_TF):
        cs = slice(c * FF_TF, (c + 1) * FF_TF)
        g = jnp.dot(h, wg_ref[:, cs], preferred_element_type=F32)
        u = jnp.dot(h, wu_ref[:, cs], preferred_element_type=F32)
        ff = (g * _sigmoid(g) * u).astype(BF16)
        acc = acc + jnp.dot(ff, wd_ref[cs, :], preferred_element_type=F32)
    x2 = x1 + _rms(acc, gpost_ref[...])
    x2_ref[...] = x2
    for hn_ref in maybe_hn_ref:
        hn_ref[...] = _rms(x2, gnext_ref[...]).astype(BF16)


def _channel(ya, yb, proj, x2d, wua, wub, wo, wg, wu, wd, layer, gmix, gpre, gpost, gnext, emit_next):
    m = ya.shape[0]
    tm = CH_TM

    def row(width, col=0):
        return pl.BlockSpec((tm, width), lambda i: (i, col))

    out_specs = [row(D_MODEL)] + ([row(D_MODEL)] if emit_next else [])
    out_shape = ([jax.ShapeDtypeStruct((m, D_MODEL), F32)]
                 + ([jax.ShapeDtypeStruct((m, D_MODEL), BF16)] if emit_next else []))
    return pl.pallas_call(
        _channel_kernel,
        grid=(m // tm,),
        in_specs=[row(MIX_W), row(MIX_W)] + [row(MIX_W, GATE_TILE + g) for g in range(4)] + [
                  row(D_MODEL),
                  _resident((MIX_W, D_MODEL), layer), _resident((MIX_W, D_MODEL), layer),
                  _resident((D_MODEL, D_MODEL), layer),
                  _resident((D_MODEL, D_FF), layer), _resident((D_MODEL, D_FF), layer),
                  _resident((D_FF, D_MODEL), layer),
                  _resident((1, D_MODEL)), _resident((1, D_MODEL)), _resident((1, D_MODEL)),
                  _resident((1, D_MODEL))],
        out_specs=out_specs,
        out_shape=out_shape,
        compiler_params=pltpu.CompilerParams(dimension_semantics=("parallel",),
                                             vmem_limit_bytes=VMEM_LIMIT),
        name="channel",
    )(ya, yb, proj, proj, proj, proj, x2d, wua, wub, wo, wg, wu, wd, gmix, gpre, gpost, gnext)


def kernel(x, lower_bounds, norm_mix_pre, norm_mix_post, norm_ffn_pre, norm_ffn_post, w_in, hg_out_norm,
           da_subln, lambda_q1, lambda_k1, lambda_q2, lambda_k2, w_up_a, w_up_b, w_out, w_ffn_gate,
           w_ffn_up, w_ffn_down):
    batch, seq, _ = x.shape
    depth = w_in.shape[0]
    m = batch * seq
    x2d = x.reshape(m, D_MODEL).astype(F32)

    lb_all = jnp.cumsum(jax.nn.softmax(lower_bounds.astype(F32), axis=0), axis=0)
    lb_all = lb_all - lb_all[0:1]

    w_in, w_up_a, w_up_b, w_out, w_ffn_gate, w_ffn_up, w_ffn_down = (
        w.astype(BF16) for w in (w_in, w_up_a, w_up_b, w_out, w_ffn_gate, w_ffn_up, w_ffn_down))

    h = None
    for l in range(depth):
        lb = lb_all[l]
        lbp = jnp.zeros((8, MIX_W), F32)
        lbp = lbp.at[0].set(jnp.maximum(jnp.log(lb), NEG_BIG)).at[1].set(jnp.log1p(-lb)).at[2].set(1.0 - lb)
        proj, lf, lfmin, qt, vt = _in_proj(x2d if h is None else h, norm_mix_pre[l].reshape(1, D_MODEL),
                                           w_in, l, lbp, prenorm=h is None)

        ya = _hgrn(proj, lf, lfmin[:, 0, 0], hg_out_norm[l].reshape(1, HEAD_W), batch, seq)

        lam_init = 0.8 - 0.6 * math.exp(-0.3 * l)
        lam = (jnp.exp(jnp.sum(lambda_q1[l].astype(F32) * lambda_k1[l].astype(F32)))
               - jnp.exp(jnp.sum(lambda_q2[l].astype(F32) * lambda_k2[l].astype(F32))) + lam_init)
        sub_gain = (da_subln[l].astype(F32) * (1.0 - lam_init)).reshape(1, HEAD_W)
        yb = _attn(lam.reshape(1), proj, qt, vt, sub_gain, batch, seq)

        last = l == depth - 1
        g_next = norm_mix_pre[(l + 1) % depth].reshape(1, D_MODEL)
        outs = _channel(ya, yb, proj, x2d, w_up_a, w_up_b, w_out, w_ffn_gate, w_ffn_up, w_ffn_down, l,
                        norm_mix_post[l].reshape(1, D_MODEL),
                        norm_ffn_pre[l].reshape(1, D_MODEL), norm_ffn_post[l].reshape(1, D_MODEL),
                        g_next, emit_next=not last)
        x2d, h = (outs[0], None) if last else outs
    return x2d.reshape(batch, seq, D_MODEL)
```

```python
import functools
import math

import jax
import jax.numpy as jnp
from jax import lax
from jax.experimental import pallas as pl
from jax.experimental.pallas import tpu as pltpu

F32 = jnp.float32
BF16 = jnp.bfloat16

D_MODEL = 1024
HEADS = 4
HEAD_W = 128
MIX_W = HEADS * HEAD_W
DQK = 64
D_FF = 2816
D_IN = 7 * MIX_W + 2 * D_MODEL
EPS = 1e-6
ALIBI_MAX_BIAS = 8.0
NEG_BIG = -1e30
LOG2E = math.log2(math.e)

IN_TM, IN_TN = 512, 512
HG_T, HG_C = 512, 64
HG_GROUP = 8
HG_MAX_FACTORED_DECAY = 80.0
ATT_TQ, ATT_TK = 512, 512
CH_TM, FF_TF = 512, 256
VMEM_LIMIT = 56 * 1024 * 1024


def _sigmoid(x):
    return 1.0 / (1.0 + jnp.exp(-x))


def _rms(x, gain):
    return x * lax.rsqrt(jnp.mean(x * x, axis=-1, keepdims=True) + EPS) * gain


def _resident(shape, layer=None):
    if layer is None:
        return pl.BlockSpec(shape, lambda *_: (0,) * len(shape), pipeline_mode=pl.Buffered(1))
    return pl.BlockSpec((None,) + tuple(shape), lambda *_: (layer,) + (0,) * len(shape),
                        pipeline_mode=pl.Buffered(1))


PROJ_TILES = 9
PROJ_W = PROJ_TILES * 512
KB_TILE, GATE_TILE = 4, 5


def _in_proj_kernel(x_ref, g_ref, w_ref, lb_ref, o_ref, lf_ref, lfmin_ref, qt_ref, vt_ref, h_ref, *,
                    prenorm):
    if prenorm:
        h_ref[...] = _rms(x_ref[...], g_ref[...]).astype(BF16)
    else:
        h_ref[...] = x_ref[...]

    def silu(acc):
        return acc * _sigmoid(acc)

    def forget_gate(acc):
        log_lb, log_1m_lb, one_m_lb = lb_ref[0:1, :], lb_ref[1:2, :], lb_ref[2:3, :]
        e = jnp.exp(-jnp.abs(acc))
        log_sig = jnp.minimum(acc, 0.0) - jnp.log(1.0 + e)
        c = log_1m_lb + log_sig
        hi = jnp.maximum(log_lb, c)
        lf = hi + jnp.log(1.0 + jnp.exp(-jnp.abs(log_lb - c)))
        lf_ref[...] = lf
        lfmin_ref[0] = jnp.full((8, HEAD_W), jnp.min(lf), F32)
        return one_m_lb * (jnp.where(acc >= 0.0, e, 1.0) / (1.0 + e))

    def query_scale(acc):
        return acc * (LOG2E / math.sqrt(DQK))

    def column_tile(jt):
        return jnp.dot(h_ref[...], w_ref[:, jt * IN_TN:(jt + 1) * IN_TN], preferred_element_type=F32)

    plan = [(1, forget_gate, 1), (0, silu, 0), (3, silu, 3)]
    plan += [(7 + g, _sigmoid, GATE_TILE + g) for g in range(4)]
    tail = [(2, None, 2), (5, None, KB_TILE)]
    for jt, epilogue, out_tile in plan:
        o_ref[:, out_tile * IN_TN:(out_tile + 1) * IN_TN] = epilogue(column_tile(jt)).astype(BF16)
    qt_ref[0] = query_scale(column_tile(4)).T.astype(BF16)
    vt_ref[0] = column_tile(6).T.astype(BF16)
    for jt, _, out_tile in tail:
        o_ref[:, out_tile * IN_TN:(out_tile + 1) * IN_TN] = column_tile(jt).astype(BF16)


def _in_proj(x_or_h, gain, w_all, layer, lbp, prenorm):
    m = x_or_h.shape[0]
    nt = m // IN_TM
    return pl.pallas_call(
        functools.partial(_in_proj_kernel, prenorm=prenorm),
        grid=(nt,),
        in_specs=[pl.BlockSpec((IN_TM, D_MODEL), lambda i: (i, 0)),
                  _resident((1, D_MODEL)), _resident((D_MODEL, D_IN), layer), _resident((8, MIX_W))],
        out_specs=[pl.BlockSpec((IN_TM, PROJ_W), lambda i: (i, 0)),
                   pl.BlockSpec((IN_TM, MIX_W), lambda i: (i, 0)),
                   pl.BlockSpec((1, 8, HEAD_W), lambda i: (i, 0, 0)),
                   pl.BlockSpec((1, MIX_W, IN_TM), lambda i: (i, 0, 0)),
                   pl.BlockSpec((1, MIX_W, IN_TM), lambda i: (i, 0, 0))],
        out_shape=[jax.ShapeDtypeStruct((m, PROJ_W), BF16),
                   jax.ShapeDtypeStruct((m, MIX_W), F32),
                   jax.ShapeDtypeStruct((nt, 8, HEAD_W), F32),
                   jax.ShapeDtypeStruct((nt, MIX_W, IN_TM), BF16),
                   jax.ShapeDtypeStruct((nt, MIX_W, IN_TM), BF16)],
        scratch_shapes=[pltpu.VMEM((IN_TM, D_MODEL), BF16)],
        compiler_params=pltpu.CompilerParams(dimension_semantics=("parallel",),
                                             vmem_limit_bytes=VMEM_LIMIT),
        name="in_proj",
    )(x_or_h, gain, w_all, lbp)


def _row_bcast(b, rows, w):
    return jnp.concatenate([jnp.broadcast_to(b[r:r + 1, :], (w, HEAD_W)) for r in rows], axis=0)


def _hgrn_kernel(lfmin_ref, q_ref, lf_ref, k_ref, v_ref, g_ref, gain_ref, o_ref, st_ref):
    @pl.when(pl.program_id(1) == 0)
    def _():
        st_ref[...] = jnp.zeros_like(st_ref)

    c = HG_C
    ri = lax.broadcasted_iota(jnp.int32, (c, c), 0)
    ci = lax.broadcasted_iota(jnp.int32, (c, c), 1)
    tri = (ci <= ri).astype(BF16)
    level_masks = []
    for w in (32, 16, 8):
        tb, sb = ri // w, ci // w
        level_masks.append((tb == sb + 1) & (tb % 2 == 1))
    diag_masks = [(ci == (ri // 8) * 8 + s) & (ri % 8 >= s) for s in range(8)]
    same_block_mask = (ri // 8 == ci // 8) & (ci <= ri)
    ones_sq = jnp.ones((HEAD_W, HEAD_W), BF16)
    gain = gain_ref[...]

    def chunk_group(ig, carry, factored):
        nt_dims = (((1,), (1,)), ((), ()))
        items = [(pl.multiple_of((ig * HG_GROUP + ch) * c, c), slice(h * HEAD_W, (h + 1) * HEAD_W), h)
                 for ch in range(HG_GROUP) for h in range(HEADS)]
        n = len(items)
        q = [q_ref[pl.ds(r0, c), cs].astype(F32) for r0, cs, _ in items]
        k = [k_ref[pl.ds(r0, c), cs].astype(F32) for r0, cs, _ in items]
        v_bf = [v_ref[pl.ds(r0, c), cs] for r0, cs, _ in items]

        b = []
        for r0, cs, _ in items:
            lf = lf_ref[pl.ds(r0, c), cs]
            lf_hi = lf.astype(BF16)
            lf_lo = (lf - lf_hi.astype(F32)).astype(BF16)
            b.append(jnp.dot(tri, lf_hi, preferred_element_type=F32)
                     + jnp.dot(tri, lf_lo, preferred_element_type=F32))
        b_last = [bh[c - 1:c, :] for bh in b]

        upd = []
        for it in range(n):
            ks = (k[it] * jnp.exp(b_last[it] - b[it])).astype(BF16)
            upd.append(lax.dot_general(v_bf[it], ks, (((0,), (0,)), ((), ())), preferred_element_type=F32))
        a = [jnp.zeros((c, c), F32) for _ in items]
        for w, msk in zip((32, 16, 8), level_masks):
            nb = c // w
            for it in range(n):
                start = _row_bcast(b[it], [j * w for j in range(nb)], w)
                nxt = _row_bcast(b[it], [(j + 1) * w for j in range(nb - 1)] + [c - 1], w)
                qh = (q[it] * jnp.exp(b[it] - start)).astype(BF16)
                kh = (k[it] * jnp.exp(nxt - b[it])).astype(BF16)
                aw = lax.dot_general(qh, kh, nt_dims, preferred_element_type=F32)
                a[it] = jnp.where(msk, aw, a[it])
                if factored and w == 8:
                    kb = (k[it] * jnp.exp(start - b[it])).astype(BF16)
                    aw = lax.dot_general(qh, kb, nt_dims, preferred_element_type=F32)
                    a[it] = jnp.where(same_block_mask, aw, a[it])

        if not factored:
            for it in range(n):
                b3 = b[it].reshape(c // 8, 8, HEAD_W)
                q3 = q[it].reshape(c // 8, 8, HEAD_W)
                k3 = k[it].reshape(c // 8, 8, HEAD_W)
                terms = []
                for s in range(8):
                    dec = jnp.exp(jnp.minimum(b3 - b3[:, s:s + 1, :], 0.0))
                    terms.append((q3 * dec * k3[:, s:s + 1, :]).reshape(c, HEAD_W).astype(BF16))
                sums = jnp.dot(jnp.concatenate(terms, axis=0), ones_sq, preferred_element_type=F32)
                for s in range(8):
                    a[it] = jnp.where(diag_masks[s], sums[s * c:(s + 1) * c, 0:c], a[it])

        o = [None] * n
        for h in range(HEADS):
            st = st_ref[h]
            for it in range(h, n, HEADS):
                qi = (q[it] * jnp.exp(b[it])).astype(BF16)
                o[it] = lax.dot_general(qi, st.astype(BF16), nt_dims, preferred_element_type=F32)
                st = st * jnp.exp(b_last[it]) + upd[it]
            st_ref[h] = st
        for it, (r0, cs, _) in enumerate(items):
            oh = o[it] + jnp.dot(a[it].astype(BF16), v_bf[it], preferred_element_type=F32)
            g = g_ref[pl.ds(r0, c), cs].astype(F32)
            o_ref[pl.ds(r0, c), cs] = (_rms(oh, gain) * g).astype(BF16)
        return carry

    tile = pl.program_id(0) * pl.num_programs(1) + pl.program_id(1)
    mild = lfmin_ref[tile] * -8.0 < HG_MAX_FACTORED_DECAY
    n_groups = HG_T // (c * HG_GROUP)

    @pl.when(mild)
    def _():
        lax.fori_loop(0, n_groups, functools.partial(chunk_group, factored=True), 0)

    @pl.when(jnp.logical_not(mild))
    def _():
        lax.fori_loop(0, n_groups, functools.partial(chunk_group, factored=False), 0)


def _hgrn(proj, lf, lfmin, gain, batch, seq):
    m = proj.shape[0]
    nt = seq // HG_T
    assert HG_T == IN_TM

    def spec(col):
        return pl.BlockSpec((HG_T, MIX_W), lambda b, t: (b * nt + t, col))

    return pl.pallas_call(
        _hgrn_kernel,
        grid=(batch, nt),
        in_specs=[pl.BlockSpec(memory_space=pltpu.SMEM),
                  spec(0), spec(0), spec(1), spec(2), spec(3),
                  pl.BlockSpec((1, HEAD_W), lambda b, t: (0, 0))],
        out_specs=spec(0),
        out_shape=jax.ShapeDtypeStruct((m, MIX_W), BF16),
        scratch_shapes=[pltpu.VMEM((HEADS, HEAD_W, HEAD_W), F32)],
        compiler_params=pltpu.CompilerParams(dimension_semantics=("parallel", "arbitrary"),
                                             vmem_limit_bytes=VMEM_LIMIT),
        name="hgrn2",
    )(lfmin, proj, lf, proj, proj, proj, gain)


ATT_HPG = 2
ATT_MAX_SCORE_BOUND = 60.0


def _attn_kernel(lam_ref, slope_ref, qt_ref, k_ref, vt_ref, gain_ref, o_ref,
                 kaug_ref, q2t_ref, acc_ref, sa_ref, sb_ref, pa_ref, pb_ref, stat_ref, lsum_ref):
    tq, r, tk = ATT_TQ, 2 * ATT_TQ, ATT_TK
    grp = pl.program_id(1)
    nblk = k_ref.shape[0] // tk
    slab = 64
    M_ROW, L_ROW, ALPHA_ROW, BMAX_ROW, KNORM_ROW, MREF_ROW = (slice(k, k + 1) for k in range(6))

    def col_reduce(x, op):
        return op(x.reshape(slab // 8, 8, r), axis=0)

    di = lax.broadcasted_iota(jnp.int32, (HEAD_W, HEAD_W), 0)
    ci = lax.broadcasted_iota(jnp.int32, (HEAD_W, HEAD_W), 1)
    same_comp = ((di < DQK) == (ci < DQK)).astype(BF16)

    def head(hh, i):
        cols = slice(hh * HEAD_W, (hh + 1) * HEAD_W)
        kaug, q2t, acc, stat = kaug_ref.at[hh], q2t_ref.at[hh], acc_ref.at[hh], stat_ref.at[hh]
        sa, sb = sa_ref, sb_ref
        pa, pb = pa_ref.at[hh], pb_ref.at[hh]
        lsum = lsum_ref.at[hh]
        slope = slope_ref[grp * ATT_HPG + hh] * LOG2E

        def fill(jb, knorm_sq):
            r0 = pl.multiple_of(jb * tk, tk)
            k_blk = k_ref[pl.ds(r0, tk), cols]
            ksq = (k_blk.astype(F32) * k_blk.astype(F32)).astype(BF16)
            sums = jnp.dot(ksq, same_comp, preferred_element_type=F32)
            knorm_sq = jnp.maximum(knorm_sq, jnp.max(sums, axis=0, keepdims=True))
            kaug[pl.ds(r0, tk), 0:HEAD_W] = k_blk
            row = lax.broadcasted_iota(jnp.int32, (tk, HEAD_W), 0) + jb * tk
            lane = lax.broadcasted_iota(jnp.int32, (tk, HEAD_W), 1)
            c = slope * row.astype(F32)
            c1 = c.astype(BF16).astype(F32)
            c2 = (c - c1).astype(BF16).astype(F32)
            c3 = c - c1 - c2
            bias = jnp.where(lane == 0, c1, jnp.where(lane == 1, c2, jnp.where(lane == 2, c3, 0.0)))
            kaug[pl.ds(r0, tk), HEAD_W:2 * HEAD_W] = bias.astype(BF16)
            return knorm_sq

        def set_key_norms(knorm_sq):
            knorm = jnp.sqrt(knorm_sq * 1.02)
            stat[KNORM_ROW, :] = jnp.concatenate([jnp.broadcast_to(knorm[:, 0:1], (1, tq)),
                                                  jnp.broadcast_to(knorm[:, DQK:DQK + 1], (1, tq))], axis=1)

        def scores(j):
            r0 = pl.multiple_of(j * tk, tk)
            return jnp.dot(kaug[pl.ds(r0, tk), :], q2t[...], preferred_element_type=F32)

        def store_scores(s, s_out):
            part = jnp.full((8, r), NEG_BIG, F32)
            for a in range(0, tk, slab):
                sl = s[a:a + slab, :]
                s_out[a:a + slab, :] = sl
                part = jnp.maximum(part, col_reduce(sl, jnp.max))
            return jnp.max(part, axis=0, keepdims=True)

        def softmax(s_of, p_out, bmax, m_old, l_old):
            m_new = jnp.maximum(m_old, bmax)
            alpha = jnp.exp2(m_old - m_new)
            part = jnp.zeros((8, r), F32)
            for a in range(0, tk, slab):
                p = jnp.exp2(s_of(slice(a, a + slab)) - m_new)
                part = part + col_reduce(p, jnp.sum)
                p_out[a:a + slab, :] = p.astype(BF16)
            l_new = alpha * l_old + jnp.sum(part, axis=0, keepdims=True)
            return alpha, m_new, l_new

        def add_values(jv, p, alpha):
            acc[...] = alpha * acc[...] + jnp.dot(vt_ref[jv, cols, :], p, preferred_element_type=F32)

        def load_queries():
            qt = qt_ref[i, cols, :].astype(F32)
            sub = lax.broadcasted_iota(jnp.int32, (HEAD_W, tq), 0)
            q2t[0:HEAD_W, 0:tq] = jnp.where(sub < DQK, qt, 0.0).astype(BF16)
            q2t[0:HEAD_W, tq:r] = jnp.where(sub >= DQK, qt, 0.0).astype(BF16)
            sub_r = lax.broadcasted_iota(jnp.int32, (HEAD_W, r), 0)
            q2t[HEAD_W:2 * HEAD_W, :] = jnp.where(sub_r < 3, 1.0, 0.0).astype(BF16)
            qsq = qt * qt
            qnorm = jnp.sqrt(jnp.concatenate([jnp.sum(qsq[0:DQK, :], axis=0, keepdims=True),
                                              jnp.sum(qsq[DQK:HEAD_W, :], axis=0, keepdims=True)], axis=1))
            bound = qnorm * stat[KNORM_ROW, :]
            ql = lax.broadcasted_iota(jnp.int32, (1, r), 1)
            q_pos = (i * tq + jnp.where(ql >= tq, ql - tq, ql)).astype(F32)
            stat[MREF_ROW, :] = bound + slope * q_pos + 1.0
            return bound

        def bounded_start():
            acc[...] = jnp.zeros((HEAD_W, r), F32)
            lsum[...] = jnp.zeros((8, r), F32)

        def bounded_block(j, p_out, masked):
            s = scores(j)
            mref = stat[MREF_ROW, :]
            if masked:
                ql = lax.broadcasted_iota(jnp.int32, (slab, r), 1)
                lead = i * tq - j * tk + jnp.where(ql >= tq, ql - tq, ql)
                row = lax.broadcasted_iota(jnp.int32, (slab, r), 0)
            part = lsum[...]
            for a in range(0, tk, slab):
                e = s[a:a + slab, :] - mref
                if masked:
                    e = jnp.where(row + a <= lead, e, NEG_BIG)
                p = jnp.exp2(e)
                part = part + col_reduce(p, jnp.sum)
                p_out[a:a + slab, :] = p.astype(BF16)
            lsum[...] = part

        def bounded_values(jv, p_in):
            acc[...] = acc[...] + jnp.dot(vt_ref[jv, cols, :], p_in[...], preferred_element_type=F32)

        def bounded_finish():
            write_output(jnp.sum(lsum[...], axis=0, keepdims=True))

        def write_output(l_fin):
            on = acc[...] * (1.0 / l_fin)
            ot = on[:, 0:tq] - lam_ref[0] * on[:, tq:r]
            ot = ot * lax.rsqrt(jnp.mean(ot * ot, axis=0, keepdims=True) + EPS)
            o_ref[pl.ds(pl.multiple_of(i * tq, tq), tq), cols] = (ot.T * gain_ref[...]).astype(BF16)

        def start():
            acc[...] = jnp.zeros((HEAD_W, r), F32)
            pb[...] = jnp.zeros((tk, r), BF16)
            stat[BMAX_ROW, :] = store_scores(scores(0), sa)
            stat[M_ROW, :] = jnp.full((1, r), NEG_BIG, F32)
            stat[L_ROW, :] = jnp.zeros((1, r), F32)
            stat[ALPHA_ROW, :] = jnp.ones((1, r), F32)

        def stage(j, flip):
            s_in, s_out, p_in, p_out = (sb, sa, pa, pb) if flip else (sa, sb, pb, pa)
            bmax_next = store_scores(scores(j + 1), s_out)
            add_values(jnp.maximum(j - 1, 0), p_in[...], stat[ALPHA_ROW, :])
            alpha, m_new, l_new = softmax(lambda rows: s_in[rows, :], p_out, stat[BMAX_ROW, :],
                                          stat[M_ROW, :], stat[L_ROW, :])
            stat[M_ROW, :] = m_new
            stat[L_ROW, :] = l_new
            stat[ALPHA_ROW, :] = alpha
            stat[BMAX_ROW, :] = bmax_next

        def finish(j_last, flip):
            s_in, p_in, p_free = (sb, pa, pb) if flip else (sa, pb, pa)
            ql = lax.broadcasted_iota(jnp.int32, (slab, r), 1)
            lead = i * tq - j_last * tk + jnp.where(ql >= tq, ql - tq, ql)
            row = lax.broadcasted_iota(jnp.int32, (slab, r), 0)

            part = jnp.full((8, r), NEG_BIG, F32)
            for a in range(0, tk, slab):
                sl = jnp.where(row + a <= lead, s_in[a:a + slab, :], NEG_BIG)
                s_in[a:a + slab, :] = sl
                part = jnp.maximum(part, col_reduce(sl, jnp.max))
            add_values(jnp.maximum(j_last - 1, 0), p_in[...], stat[ALPHA_ROW, :])
            alpha, _, l_fin = softmax(lambda rows: s_in[rows, :], p_free,
                                      jnp.max(part, axis=0, keepdims=True),
                                      stat[M_ROW, :], stat[L_ROW, :])
            add_values(j_last, p_free[...], alpha)
            write_output(l_fin)

        return dict(fill=fill, set_key_norms=set_key_norms, load_queries=load_queries,
                    start=start, stage=stage, finish=finish,
                    bounded_start=bounded_start, bounded_block=bounded_block,
                    bounded_values=bounded_values, bounded_finish=bounded_finish, pa=pa, pb=pb)

    def fill_all(jb, carry):
        return tuple(head(hh, 0)["fill"](jb, carry[hh]) for hh in range(ATT_HPG))
    knorm_sq = lax.fori_loop(0, nblk, fill_all, tuple(jnp.zeros((1, HEAD_W), F32) for _ in range(ATT_HPG)))
    for hh in range(ATT_HPG):
        head(hh, 0)["set_key_norms"](knorm_sq[hh])

    def query_tile(i, carry):
        heads = [head(hh, i) for hh in range(ATT_HPG)]
        n_full = (i * tq) // tk
        bound = functools.reduce(jnp.maximum, [h["load_queries"]() for h in heads])
        bounded = jnp.max(bound) < ATT_MAX_SCORE_BOUND

        def each(name, *args):
            for h in heads:
                h[name](*args)

        @pl.when(bounded)
        def _():
            each("bounded_start")
            for h in heads:
                h["bounded_block"](n_full, h["pb"], True)

            def pair(t, owed):
                for h in heads:
                    h["bounded_block"](2 * t, h["pa"], False)
                for h in heads:
                    h["bounded_values"](owed, h["pb"])
                for h in heads:
                    h["bounded_block"](2 * t + 1, h["pb"], False)
                for h in heads:
                    h["bounded_values"](2 * t, h["pa"])
                return 2 * t + 1
            owed = lax.fori_loop(0, n_full // 2, pair, n_full)

            @pl.when(n_full % 2 == 0)
            def _():
                for h in heads:
                    h["bounded_values"](owed, h["pb"])

            @pl.when(n_full % 2 == 1)
            def _():
                for h in heads:
                    h["bounded_block"](n_full - 1, h["pa"], False)
                for h in heads:
                    h["bounded_values"](owed, h["pb"])
                for h in heads:
                    h["bounded_values"](n_full - 1, h["pa"])
            each("bounded_finish")

        @pl.when(jnp.logical_not(bounded))
        def _():
            for h in heads:
                h["start"]()

                def pair(t, carry, h=h):
                    h["stage"](2 * t, False)
                    h["stage"](2 * t + 1, True)
                    return carry
                lax.fori_loop(0, n_full // 2, pair, 0)

                @pl.when(n_full % 2 == 0)
                def _(h=h):
                    h["finish"](n_full, False)

                @pl.when(n_full % 2 == 1)
                def _(h=h):
                    h["stage"](n_full - 1, False)
                    h["finish"](n_full, True)
        return carry

    lax.fori_loop(0, qt_ref.shape[0], query_tile, 0)


def _attn(lam, proj, qt, vt, gain, batch, seq):
    m = proj.shape[0]
    slopes = jnp.exp2(-(ALIBI_MAX_BIAS / HEADS) * jnp.arange(1, HEADS + 1, dtype=F32))
    assert ATT_TQ == IN_TM and ATT_TK == IN_TM
    nq = seq // ATT_TQ
    nk = seq // ATT_TK
    gw = ATT_HPG * HEAD_W
    kcol = KB_TILE * (MIX_W // gw)
    hpg, r = ATT_HPG, 2 * ATT_TQ
    return pl.pallas_call(
        _attn_kernel,
        grid=(batch, HEADS // ATT_HPG),
        in_specs=[pl.BlockSpec(memory_space=pltpu.SMEM),
                  pl.BlockSpec(memory_space=pltpu.SMEM),
                  pl.BlockSpec((nq, gw, ATT_TQ), lambda b, g: (b, g, 0)),
                  pl.BlockSpec((seq, gw), lambda b, g: (b, kcol + g)),
                  pl.BlockSpec((nk, gw, ATT_TK), lambda b, g: (b, g, 0)),
                  pl.BlockSpec((1, HEAD_W), lambda b, g: (0, 0))],
        out_specs=pl.BlockSpec((seq, gw), lambda b, g: (b, g)),
        out_shape=jax.ShapeDtypeStruct((m, MIX_W), BF16),
        scratch_shapes=[pltpu.VMEM((hpg, seq, 2 * HEAD_W), BF16),
                        pltpu.VMEM((hpg, 2 * HEAD_W, r), BF16),
                        pltpu.VMEM((hpg, HEAD_W, r), F32),
                        pltpu.VMEM((ATT_TK, r), F32),
                        pltpu.VMEM((ATT_TK, r), F32),
                        pltpu.VMEM((hpg, ATT_TK, r), BF16),
                        pltpu.VMEM((hpg, ATT_TK, r), BF16),
                        pltpu.VMEM((hpg, 8, r), F32),
                        pltpu.VMEM((hpg, 8, r), F32)],
        compiler_params=pltpu.CompilerParams(dimension_semantics=("parallel", "parallel"),
                                             vmem_limit_bytes=VMEM_LIMIT),
        name="diffattn",
    )(lam, slopes, qt, proj, vt, gain)


def _channel_kernel(ya_ref, yb_ref, ga0_ref, ga1_ref, gb0_ref, gb1_ref, x_ref,
                    wua_ref, wub_ref, wo_ref, wg_ref, wu_ref, wd_ref,
                    gmix_ref, gpre_ref, gpost_ref, gnext_ref, x2_ref, *maybe_hn_ref):
    ua = jnp.dot(ya_ref[...], wua_ref[...], preferred_element_type=F32)
    ub = jnp.dot(yb_ref[...], wub_ref[...], preferred_element_type=F32)
    ga = jnp.concatenate([ga0_ref[...], ga1_ref[...]], axis=1).astype(F32)
    gb = jnp.concatenate([gb0_ref[...], gb1_ref[...]], axis=1).astype(F32)
    merged = (ga * ua + gb * ub).astype(BF16)
    mix = jnp.dot(merged, wo_ref[...], preferred_element_type=F32)
    x1 = x_ref[...] + _rms(mix, gmix_ref[...])
    h = _rms(x1, gpre_ref[...]).astype(BF16)

    acc = jnp.zeros((CH_TM, D_MODEL), F32)
    for c in range(D_FF // FF_TF):
        cs = slice(c * FF_TF, (c + 1) * FF_TF)
        g = jnp.dot(h, wg_ref[:, cs], preferred_element_type=F32)
        u = jnp.dot(h, wu_ref[:, cs], preferred_element_type=F32)
        ff = (g * _sigmoid(g) * u).astype(BF16)
        acc = acc + jnp.dot(ff, wd_ref[cs, :], preferred_element_type=F32)
    x2 = x1 + _rms(acc, gpost_ref[...])
    x2_ref[...] = x2
    for hn_ref in maybe_hn_ref:
        hn_ref[...] = _rms(x2, gnext_ref[...]).astype(BF16)


def _channel(ya, yb, proj, x2d, wua, wub, wo, wg, wu, wd, layer, gmix, gpre, gpost, gnext, emit_next):
    m = ya.shape[0]
    tm = CH_TM

    def row(width, col=0):
        return pl.BlockSpec((tm, width), lambda i: (i, col))

    out_specs = [row(D_MODEL)] + ([row(D_MODEL)] if emit_next else [])
    out_shape = ([jax.ShapeDtypeStruct((m, D_MODEL), F32)]
                 + ([jax.ShapeDtypeStruct((m, D_MODEL), BF16)] if emit_next else []))
    return pl.pallas_call(
        _channel_kernel,
        grid=(m // tm,),
        in_specs=[row(MIX_W), row(MIX_W)] + [row(MIX_W, GATE_TILE + g) for g in range(4)] + [
                  row(D_MODEL),
                  _resident((MIX_W, D_MODEL), layer), _resident((MIX_W, D_MODEL), layer),
                  _resident((D_MODEL, D_MODEL), layer),
                  _resident((D_MODEL, D_FF), layer), _resident((D_MODEL, D_FF), layer),
                  _resident((D_FF, D_MODEL), layer),
                  _resident((1, D_MODEL)), _resident((1, D_MODEL)), _resident((1, D_MODEL)),
                  _resident((1, D_MODEL))],
        out_specs=out_specs,
        out_shape=out_shape,
        compiler_params=pltpu.CompilerParams(dimension_semantics=("parallel",),
                                             vmem_limit_bytes=VMEM_LIMIT),
        name="channel",
    )(ya, yb, proj, proj, proj, proj, x2d, wua, wub, wo, wg, wu, wd, gmix, gpre, gpost, gnext)


def kernel(x, lower_bounds, norm_mix_pre, norm_mix_post, norm_ffn_pre, norm_ffn_post, w_in, hg_out_norm,
           da_subln, lambda_q1, lambda_k1, lambda_q2, lambda_k2, w_up_a, w_up_b, w_out, w_ffn_gate,
           w_ffn_up, w_ffn_down):
    batch, seq, _ = x.shape
    depth = w_in.shape[0]
    m = batch * seq
    x2d = x.reshape(m, D_MODEL).astype(F32)

    lb_all = jnp.cumsum(jax.nn.softmax(lower_bounds.astype(F32), axis=0), axis=0)
    lb_all = lb_all - lb_all[0:1]

    w_in, w_up_a, w_up_b, w_out, w_ffn_gate, w_ffn_up, w_ffn_down = (
        w.astype(BF16) for w in (w_in, w_up_a, w_up_b, w_out, w_ffn_gate, w_ffn_up, w_ffn_down))

    h = None
    for l in range(depth):
        lb = lb_all[l]
        lbp = jnp.zeros((8, MIX_W), F32)
        lbp = lbp.at[0].set(jnp.maximum(jnp.log(lb), NEG_BIG)).at[1].set(jnp.log1p(-lb)).at[2].set(1.0 - lb)
        proj, lf, lfmin, qt, vt = _in_proj(x2d if h is None else h, norm_mix_pre[l].reshape(1, D_MODEL),
                                           w_in, l, lbp, prenorm=h is None)

        ya = _hgrn(proj, lf, lfmin[:, 0, 0], hg_out_norm[l].reshape(1, HEAD_W), batch, seq)

        lam_init = 0.8 - 0.6 * math.exp(-0.3 * l)
        lam = (jnp.exp(jnp.sum(lambda_q1[l].astype(F32) * lambda_k1[l].astype(F32)))
               - jnp.exp(jnp.sum(lambda_q2[l].astype(F32) * lambda_k2[l].astype(F32))) + lam_init)
        sub_gain = (da_subln[l].astype(F32) * (1.0 - lam_init)).reshape(1, HEAD_W)
        yb = _attn(lam.reshape(1), proj, qt, vt, sub_gain, batch, seq)

        last = l == depth - 1
        g_next = norm_mix_pre[(l + 1) % depth].reshape(1, D_MODEL)
        outs = _channel(ya, yb, proj, x2d, w_up_a, w_up_b, w_out, w_ffn_gate, w_ffn_up, w_ffn_down, l,
                        norm_mix_post[l].reshape(1, D_MODEL),
                        norm_ffn_pre[l].reshape(1, D_MODEL), norm_ffn_post[l].reshape(1, D_MODEL),
                        g_next, emit_next=not last)
        x2d, h = (outs[0], None) if last else outs
    return x2d.reshape(batch, seq, D_MODEL)
```

```python
import functools
import math

import jax
import jax.numpy as jnp
from jax import lax
from jax.experimental import pallas as pl
from jax.experimental.pallas import tpu as pltpu

F32 = jnp.float32
BF16 = jnp.bfloat16

D_MODEL = 1024
HEADS = 4
HEAD_W = 128
MIX_W = HEADS * HEAD_W
DQK = 64
D_FF = 2816
D_IN = 7 * MIX_W + 2 * D_MODEL
EPS = 1e-6
ALIBI_MAX_BIAS = 8.0
NEG_BIG = -1e30
LOG2E = math.log2(math.e)

IN_TM, IN_TN = 512, 512
IN_ROW_PARTS = 4
HG_T, HG_C = 512, 64
HG_GROUP = 8
HG_MAX_FACTORED_DECAY = 80.0
ATT_TQ, ATT_TK = 512, 512
CH_TM, FF_TF = 512, 256
VMEM_LIMIT = 56 * 1024 * 1024


def _sigmoid(x):
    return 1.0 / (1.0 + jnp.exp(-x))


def _rms(x, gain):
    return x * lax.rsqrt(jnp.mean(x * x, axis=-1, keepdims=True) + EPS) * gain


def _resident(shape, layer=None):
    if layer is None:
        return pl.BlockSpec(shape, lambda *_: (0,) * len(shape), pipeline_mode=pl.Buffered(1))
    return pl.BlockSpec((None,) + tuple(shape), lambda *_: (layer,) + (0,) * len(shape),
                        pipeline_mode=pl.Buffered(1))


PROJ_TILES = 9
PROJ_W = PROJ_TILES * 512
KB_TILE, GATE_TILE = 4, 5


def _in_proj_kernel(x_ref, g_ref, w_ref, lb_ref, o_ref, lf_ref, lfmin_ref, qt_ref, vt_ref, h_ref, *,
                    prenorm):
    if prenorm:
        h_ref[...] = _rms(x_ref[...], g_ref[...]).astype(BF16)
    else:
        h_ref[...] = x_ref[...]

    def silu(acc):
        return acc * _sigmoid(acc)

    lf_mins = []

    def forget_gate(acc, rs):
        log_lb, log_1m_lb, one_m_lb = lb_ref[0:1, :], lb_ref[1:2, :], lb_ref[2:3, :]
        e = jnp.exp(-jnp.abs(acc))
        log_sig = jnp.minimum(acc, 0.0) - jnp.log(1.0 + e)
        c = log_1m_lb + log_sig
        hi = jnp.maximum(log_lb, c)
        lf = hi + jnp.log(1.0 + jnp.exp(-jnp.abs(log_lb - c)))
        lf_ref[rs, :] = lf
        lf_mins.append(jnp.min(lf))
        return one_m_lb * (jnp.where(acc >= 0.0, e, 1.0) / (1.0 + e))

    def query_scale(acc):
        return acc * (LOG2E / math.sqrt(DQK))

    row_parts = [slice(r0, r0 + IN_TM // IN_ROW_PARTS) for r0 in range(0, IN_TM, IN_TM // IN_ROW_PARTS)]

    def column_tile(jt, rs):
        return jnp.dot(h_ref[rs, :], w_ref[:, jt * IN_TN:(jt + 1) * IN_TN], preferred_element_type=F32)

    plan = [(0, silu, 0), (3, silu, 3)] + [(7 + g, _sigmoid, GATE_TILE + g) for g in range(4)]
    tail = [(2, None, 2), (5, None, KB_TILE)]
    for rs in row_parts:
        o_ref[rs, IN_TN:2 * IN_TN] = forget_gate(column_tile(1, rs), rs).astype(BF16)
    lfmin_ref[0] = jnp.full((8, HEAD_W), functools.reduce(jnp.minimum, lf_mins), F32)
    for jt, epilogue, out_tile in plan:
        for rs in row_parts:
            o_ref[rs, out_tile * IN_TN:(out_tile + 1) * IN_TN] = epilogue(column_tile(jt, rs)).astype(BF16)
    for rs in row_parts:
        qt_ref[0, :, rs] = query_scale(column_tile(4, rs)).T.astype(BF16)
        vt_ref[0, :, rs] = column_tile(6, rs).T.astype(BF16)
    for jt, _, out_tile in tail:
        for rs in row_parts:
            o_ref[rs, out_tile * IN_TN:(out_tile + 1) * IN_TN] = column_tile(jt, rs).astype(BF16)


def _in_proj(x_or_h, gain, w_all, layer, lbp, prenorm):
    m = x_or_h.shape[0]
    nt = m // IN_TM
    return pl.pallas_call(
        functools.partial(_in_proj_kernel, prenorm=prenorm),
        grid=(nt,),
        in_specs=[pl.BlockSpec((IN_TM, D_MODEL), lambda i: (i, 0)),
                  _resident((1, D_MODEL)), _resident((D_MODEL, D_IN), layer), _resident((8, MIX_W))],
        out_specs=[pl.BlockSpec((IN_TM, PROJ_W), lambda i: (i, 0)),
                   pl.BlockSpec((IN_TM, MIX_W), lambda i: (i, 0)),
                   pl.BlockSpec((1, 8, HEAD_W), lambda i: (i, 0, 0)),
                   pl.BlockSpec((1, MIX_W, IN_TM), lambda i: (i, 0, 0)),
                   pl.BlockSpec((1, MIX_W, IN_TM), lambda i: (i, 0, 0))],
        out_shape=[jax.ShapeDtypeStruct((m, PROJ_W), BF16),
                   jax.ShapeDtypeStruct((m, MIX_W), F32),
                   jax.ShapeDtypeStruct((nt, 8, HEAD_W), F32),
                   jax.ShapeDtypeStruct((nt, MIX_W, IN_TM), BF16),
                   jax.ShapeDtypeStruct((nt, MIX_W, IN_TM), BF16)],
        scratch_shapes=[pltpu.VMEM((IN_TM, D_MODEL), BF16)],
        compiler_params=pltpu.CompilerParams(dimension_semantics=("parallel",),
                                             vmem_limit_bytes=VMEM_LIMIT),
        name="in_proj",
    )(x_or_h, gain, w_all, lbp)


def _row_bcast(b, rows, w):
    return jnp.concatenate([jnp.broadcast_to(b[r:r + 1, :], (w, HEAD_W)) for r in rows], axis=0)


def _hgrn_kernel(lfmin_ref, q_ref, lf_ref, k_ref, v_ref, g_ref, gain_ref, o_ref, st_ref):
    @pl.when(pl.program_id(1) == 0)
    def _():
        st_ref[...] = jnp.zeros_like(st_ref)

    c = HG_C
    ri = lax.broadcasted_iota(jnp.int32, (c, c), 0)
    ci = lax.broadcasted_iota(jnp.int32, (c, c), 1)
    tri = (ci <= ri).astype(BF16)
    level_masks = []
    for w in (32, 16, 8):
        tb, sb = ri // w, ci // w
        level_masks.append((tb == sb + 1) & (tb % 2 == 1))
    diag_masks = [(ci == (ri // 8) * 8 + s) & (ri % 8 >= s) for s in range(8)]
    same_block_mask = (ri // 8 == ci // 8) & (ci <= ri)
    ones_sq = jnp.ones((HEAD_W, HEAD_W), BF16)
    gain = gain_ref[...]

    def chunk_group(ig, carry, factored):
        nt_dims = (((1,), (1,)), ((), ()))
        items = [(pl.multiple_of((ig * HG_GROUP + ch) * c, c), slice(h * HEAD_W, (h + 1) * HEAD_W), h)
                 for ch in range(HG_GROUP) for h in range(HEADS)]
        n = len(items)
        q = [q_ref[pl.ds(r0, c), cs].astype(F32) for r0, cs, _ in items]
        k = [k_ref[pl.ds(r0, c), cs].astype(F32) for r0, cs, _ in items]
        v_bf = [v_ref[pl.ds(r0, c), cs] for r0, cs, _ in items]

        b = []
        for r0, cs, _ in items:
            lf = lf_ref[pl.ds(r0, c), cs]
            lf_hi = lf.astype(BF16)
            lf_lo = (lf - lf_hi.astype(F32)).astype(BF16)
            b.append(jnp.dot(tri, lf_hi, preferred_element_type=F32)
                     + jnp.dot(tri, lf_lo, preferred_element_type=F32))
        b_last = [bh[c - 1:c, :] for bh in b]

        upd = []
        for it in range(n):
            ks = (k[it] * jnp.exp(b_last[it] - b[it])).astype(BF16)
            upd.append(lax.dot_general(v_bf[it], ks, (((0,), (0,)), ((), ())), preferred_element_type=F32))
        a = [jnp.zeros((c, c), F32) for _ in items]
        for w, msk in zip((32, 16, 8), level_masks):
            nb = c // w
            for it in range(n):
                start = _row_bcast(b[it], [j * w for j in range(nb)], w)
                nxt = _row_bcast(b[it], [(j + 1) * w for j in range(nb - 1)] + [c - 1], w)
                qh = (q[it] * jnp.exp(b[it] - start)).astype(BF16)
                kh = (k[it] * jnp.exp(nxt - b[it])).astype(BF16)
                aw = lax.dot_general(qh, kh, nt_dims, preferred_element_type=F32)
                a[it] = jnp.where(msk, aw, a[it])
                if factored and w == 8:
                    kb = (k[it] * jnp.exp(start - b[it])).astype(BF16)
                    aw = lax.dot_general(qh, kb, nt_dims, preferred_element_type=F32)
                    a[it] = jnp.where(same_block_mask, aw, a[it])

        if not factored:
            for it in range(n):
                b3 = b[it].reshape(c // 8, 8, HEAD_W)
                q3 = q[it].reshape(c // 8, 8, HEAD_W)
                k3 = k[it].reshape(c // 8, 8, HEAD_W)
                terms = []
                for s in range(8):
                    dec = jnp.exp(jnp.minimum(b3 - b3[:, s:s + 1, :], 0.0))
                    terms.append((q3 * dec * k3[:, s:s + 1, :]).reshape(c, HEAD_W).astype(BF16))
                sums = jnp.dot(jnp.concatenate(terms, axis=0), ones_sq, preferred_element_type=F32)
                for s in range(8):
                    a[it] = jnp.where(diag_masks[s], sums[s * c:(s + 1) * c, 0:c], a[it])

        o = [None] * n
        for h in range(HEADS):
            st = st_ref[h]
            for it in range(h, n, HEADS):
                qi = (q[it] * jnp.exp(b[it])).astype(BF16)
                o[it] = lax.dot_general(qi, st.astype(BF16), nt_dims, preferred_element_type=F32)
                st = st * jnp.exp(b_last[it]) + upd[it]
            st_ref[h] = st
        for it, (r0, cs, _) in enumerate(items):
            oh = o[it] + jnp.dot(a[it].astype(BF16), v_bf[it], preferred_element_type=F32)
            g = g_ref[pl.ds(r0, c), cs].astype(F32)
            o_ref[pl.ds(r0, c), cs] = (_rms(oh, gain) * g).astype(BF16)
        return carry

    tile = pl.program_id(0) * pl.num_programs(1) + pl.program_id(1)
    mild = lfmin_ref[tile] * -8.0 < HG_MAX_FACTORED_DECAY
    n_groups = HG_T // (c * HG_GROUP)

    @pl.when(mild)
    def _():
        lax.fori_loop(0, n_groups, functools.partial(chunk_group, factored=True), 0)

    @pl.when(jnp.logical_not(mild))
    def _():
        lax.fori_loop(0, n_groups, functools.partial(chunk_group, factored=False), 0)


def _hgrn(proj, lf, lfmin, gain, batch, seq):
    m = proj.shape[0]
    nt = seq // HG_T
    assert HG_T == IN_TM

    def spec(col):
        return pl.BlockSpec((HG_T, MIX_W), lambda b, t: (b * nt + t, col))

    return pl.pallas_call(
        _hgrn_kernel,
        grid=(batch, nt),
        in_specs=[pl.BlockSpec(memory_space=pltpu.SMEM),
                  spec(0), spec(0), spec(1), spec(2), spec(3),
                  pl.BlockSpec((1, HEAD_W), lambda b, t: (0, 0))],
        out_specs=spec(0),
        out_shape=jax.ShapeDtypeStruct((m, MIX_W), BF16),
        scratch_shapes=[pltpu.VMEM((HEADS, HEAD_W, HEAD_W), F32)],
        compiler_params=pltpu.CompilerParams(dimension_semantics=("parallel", "arbitrary"),
                                             vmem_limit_bytes=VMEM_LIMIT),
        name="hgrn2",
    )(lfmin, proj, lf, proj, proj, proj, gain)


ATT_HPG = 2
ATT_MAX_SCORE_BOUND = 60.0


def _attn_kernel(lam_ref, slope_ref, qt_ref, k_ref, vt_ref, gain_ref, o_ref,
                 kaug_ref, q2t_ref, acc_ref, sa_ref, sb_ref, pa_ref, pb_ref, stat_ref, lsum_ref):
    tq, r, tk = ATT_TQ, 2 * ATT_TQ, ATT_TK
    grp = pl.program_id(1)
    nblk = k_ref.shape[0] // tk
    slab = 64
    M_ROW, L_ROW, ALPHA_ROW, BMAX_ROW, KNORM_ROW, MREF_ROW = (slice(k, k + 1) for k in range(6))

    def col_reduce(x, op):
        return op(x.reshape(slab // 8, 8, r), axis=0)

    di = lax.broadcasted_iota(jnp.int32, (HEAD_W, HEAD_W), 0)
    ci = lax.broadcasted_iota(jnp.int32, (HEAD_W, HEAD_W), 1)
    same_comp = ((di < DQK) == (ci < DQK)).astype(BF16)

    def head(hh, i):
        cols = slice(hh * HEAD_W, (hh + 1) * HEAD_W)
        kaug, q2t, acc, stat = kaug_ref.at[hh], q2t_ref.at[hh], acc_ref.at[hh], stat_ref.at[hh]
        sa, sb = sa_ref, sb_ref
        pa, pb = pa_ref.at[hh], pb_ref.at[hh]
        lsum = lsum_ref.at[hh]
        slope = slope_ref[grp * ATT_HPG + hh] * LOG2E

        def fill(jb, knorm_sq):
            r0 = pl.multiple_of(jb * tk, tk)
            k_blk = k_ref[pl.ds(r0, tk), cols]
            ksq = (k_blk.astype(F32) * k_blk.astype(F32)).astype(BF16)
            sums = jnp.dot(ksq, same_comp, preferred_element_type=F32)
            knorm_sq = jnp.maximum(knorm_sq, jnp.max(sums, axis=0, keepdims=True))
            kaug[pl.ds(r0, tk), 0:HEAD_W] = k_blk
            row = lax.broadcasted_iota(jnp.int32, (tk, HEAD_W), 0) + jb * tk
            lane = lax.broadcasted_iota(jnp.int32, (tk, HEAD_W), 1)
            c = slope * row.astype(F32)
            c1 = c.astype(BF16).astype(F32)
            c2 = (c - c1).astype(BF16).astype(F32)
            c3 = c - c1 - c2
            bias = jnp.where(lane == 0, c1, jnp.where(lane == 1, c2, jnp.where(lane == 2, c3, 0.0)))
            kaug[pl.ds(r0, tk), HEAD_W:2 * HEAD_W] = bias.astype(BF16)
            return knorm_sq

        def set_key_norms(knorm_sq):
            knorm = jnp.sqrt(knorm_sq * 1.02)
            stat[KNORM_ROW, :] = jnp.concatenate([jnp.broadcast_to(knorm[:, 0:1], (1, tq)),
                                                  jnp.broadcast_to(knorm[:, DQK:DQK + 1], (1, tq))], axis=1)

        def scores(j):
            r0 = pl.multiple_of(j * tk, tk)
            return jnp.dot(kaug[pl.ds(r0, tk), :], q2t[...], preferred_element_type=F32)

        def store_scores(s, s_out):
            part = jnp.full((8, r), NEG_BIG, F32)
            for a in range(0, tk, slab):
                sl = s[a:a + slab, :]
                s_out[a:a + slab, :] = sl
                part = jnp.maximum(part, col_reduce(sl, jnp.max))
            return jnp.max(part, axis=0, keepdims=True)

        def softmax(s_of, p_out, bmax, m_old, l_old):
            m_new = jnp.maximum(m_old, bmax)
            alpha = jnp.exp2(m_old - m_new)
            part = jnp.zeros((8, r), F32)
            for a in range(0, tk, slab):
                p = jnp.exp2(s_of(slice(a, a + slab)) - m_new)
                part = part + col_reduce(p, jnp.sum)
                p_out[a:a + slab, :] = p.astype(BF16)
            l_new = alpha * l_old + jnp.sum(part, axis=0, keepdims=True)
            return alpha, m_new, l_new

        def add_values(jv, p, alpha):
            acc[...] = alpha * acc[...] + jnp.dot(vt_ref[jv, cols, :], p, preferred_element_type=F32)

        def load_queries():
            qt = qt_ref[i, cols, :].astype(F32)
            sub = lax.broadcasted_iota(jnp.int32, (HEAD_W, tq), 0)
            q2t[0:HEAD_W, 0:tq] = jnp.where(sub < DQK, qt, 0.0).astype(BF16)
            q2t[0:HEAD_W, tq:r] = jnp.where(sub >= DQK, qt, 0.0).astype(BF16)
            sub_r = lax.broadcasted_iota(jnp.int32, (HEAD_W, r), 0)
            q2t[HEAD_W:2 * HEAD_W, :] = jnp.where(sub_r < 3, 1.0, 0.0).astype(BF16)
            qsq = qt * qt
            qnorm = jnp.sqrt(jnp.concatenate([jnp.sum(qsq[0:DQK, :], axis=0, keepdims=True),
                                              jnp.sum(qsq[DQK:HEAD_W, :], axis=0, keepdims=True)], axis=1))
            bound = qnorm * stat[KNORM_ROW, :]
            ql = lax.broadcasted_iota(jnp.int32, (1, r), 1)
            q_pos = (i * tq + jnp.where(ql >= tq, ql - tq, ql)).astype(F32)
            stat[MREF_ROW, :] = bound + slope * q_pos + 1.0
            return bound

        def bounded_start():
            acc[...] = jnp.zeros((HEAD_W, r), F32)
            lsum[...] = jnp.zeros((8, r), F32)

        def bounded_block(j, p_out, masked):
            s = scores(j)
            mref = stat[MREF_ROW, :]
            if masked:
                ql = lax.broadcasted_iota(jnp.int32, (slab, r), 1)
                lead = i * tq - j * tk + jnp.where(ql >= tq, ql - tq, ql)
                row = lax.broadcasted_iota(jnp.int32, (slab, r), 0)
            part = lsum[...]
            for a in range(0, tk, slab):
                e = s[a:a + slab, :] - mref
                if masked:
                    e = jnp.where(row + a <= lead, e, NEG_BIG)
                p = jnp.exp2(e)
                part = part + col_reduce(p, jnp.sum)
                p_out[a:a + slab, :] = p.astype(BF16)
            lsum[...] = part

        def bounded_values(jv, p_in):
            acc[...] = acc[...] + jnp.dot(vt_ref[jv, cols, :], p_in[...], preferred_element_type=F32)

        def bounded_finish():
            write_output(jnp.sum(lsum[...], axis=0, keepdims=True))

        def write_output(l_fin):
            on = acc[...] * (1.0 / l_fin)
            ot = on[:, 0:tq] - lam_ref[0] * on[:, tq:r]
            ot = ot * lax.rsqrt(jnp.mean(ot * ot, axis=0, keepdims=True) + EPS)
            o_ref[pl.ds(pl.multiple_of(i * tq, tq), tq), cols] = (ot.T * gain_ref[...]).astype(BF16)

        def start():
            acc[...] = jnp.zeros((HEAD_W, r), F32)
            pb[...] = jnp.zeros((tk, r), BF16)
            stat[BMAX_ROW, :] = store_scores(scores(0), sa)
            stat[M_ROW, :] = jnp.full((1, r), NEG_BIG, F32)
            stat[L_ROW, :] = jnp.zeros((1, r), F32)
            stat[ALPHA_ROW, :] = jnp.ones((1, r), F32)

        def stage(j, flip):
            s_in, s_out, p_in, p_out = (sb, sa, pa, pb) if flip else (sa, sb, pb, pa)
            bmax_next = store_scores(scores(j + 1), s_out)
            add_values(jnp.maximum(j - 1, 0), p_in[...], stat[ALPHA_ROW, :])
            alpha, m_new, l_new = softmax(lambda rows: s_in[rows, :], p_out, stat[BMAX_ROW, :],
                                          stat[M_ROW, :], stat[L_ROW, :])
            stat[M_ROW, :] = m_new
            stat[L_ROW, :] = l_new
            stat[ALPHA_ROW, :] = alpha
            stat[BMAX_ROW, :] = bmax_next

        def finish(j_last, flip):
            s_in, p_in, p_free = (sb, pa, pb) if flip else (sa, pb, pa)
            ql = lax.broadcasted_iota(jnp.int32, (slab, r), 1)
            lead = i * tq - j_last * tk + jnp.where(ql >= tq, ql - tq, ql)
            row = lax.broadcasted_iota(jnp.int32, (slab, r), 0)

            part = jnp.full((8, r), NEG_BIG, F32)
            for a in range(0, tk, slab):
                sl = jnp.where(row + a <= lead, s_in[a:a + slab, :], NEG_BIG)
                s_in[a:a + slab, :] = sl
                part = jnp.maximum(part, col_reduce(sl, jnp.max))
            add_values(jnp.maximum(j_last - 1, 0), p_in[...], stat[ALPHA_ROW, :])
            alpha, _, l_fin = softmax(lambda rows: s_in[rows, :], p_free,
                                      jnp.max(part, axis=0, keepdims=True),
                                      stat[M_ROW, :], stat[L_ROW, :])
            add_values(j_last, p_free[...], alpha)
            write_output(l_fin)

        return dict(fill=fill, set_key_norms=set_key_norms, load_queries=load_queries,
                    start=start, stage=stage, finish=finish,
                    bounded_start=bounded_start, bounded_block=bounded_block,
                    bounded_values=bounded_values, bounded_finish=bounded_finish, pa=pa, pb=pb)

    def fill_all(jb, carry):
        return tuple(head(hh, 0)["fill"](jb, carry[hh]) for hh in range(ATT_HPG))
    knorm_sq = lax.fori_loop(0, nblk, fill_all, tuple(jnp.zeros((1, HEAD_W), F32) for _ in range(ATT_HPG)))
    for hh in range(ATT_HPG):
        head(hh, 0)["set_key_norms"](knorm_sq[hh])

    def query_tile(i, carry):
        heads = [head(hh, i) for hh in range(ATT_HPG)]
        n_full = (i * tq) // tk
        bound = functools.reduce(jnp.maximum, [h["load_queries"]() for h in heads])
        bounded = jnp.max(bound) < ATT_MAX_SCORE_BOUND

        def each(name, *args):
            for h in heads:
                h[name](*args)

        @pl.when(bounded)
        def _():
            each("bounded_start")
            for h in heads:
                h["bounded_block"](n_full, h["pb"], True)

            def pair(t, owed):
                for h in heads:
                    h["bounded_block"](2 * t, h["pa"], False)
                for h in heads:
                    h["bounded_values"](owed, h["pb"])
                for h in heads:
                    h["bounded_block"](2 * t + 1, h["pb"], False)
                for h in heads:
                    h["bounded_values"](2 * t, h["pa"])
                return 2 * t + 1
            owed = lax.fori_loop(0, n_full // 2, pair, n_full)

            @pl.when(n_full % 2 == 0)
            def _():
                for h in heads:
                    h["bounded_values"](owed, h["pb"])

            @pl.when(n_full % 2 == 1)
            def _():
                for h in heads:
                    h["bounded_block"](n_full - 1, h["pa"], False)
                for h in heads:
                    h["bounded_values"](owed, h["pb"])
                for h in heads:
                    h["bounded_values"](n_full - 1, h["pa"])
            each("bounded_finish")

        @pl.when(jnp.logical_not(bounded))
        def _():
            for h in heads:
                h["start"]()

                def pair(t, carry, h=h):
                    h["stage"](2 * t, False)
                    h["stage"](2 * t + 1, True)
                    return carry
                lax.fori_loop(0, n_full // 2, pair, 0)

                @pl.when(n_full % 2 == 0)
                def _(h=h):
                    h["finish"](n_full, False)

                @pl.when(n_full % 2 == 1)
                def _(h=h):
                    h["stage"](n_full - 1, False)
                    h["finish"](n_full, True)
        return carry

    lax.fori_loop(0, qt_ref.shape[0], query_tile, 0)


def _attn(lam, proj, qt, vt, gain, batch, seq):
    m = proj.shape[0]
    slopes = jnp.exp2(-(ALIBI_MAX_BIAS / HEADS) * jnp.arange(1, HEADS + 1, dtype=F32))
    assert ATT_TQ == IN_TM and ATT_TK == IN_TM
    nq = seq // ATT_TQ
    nk = seq // ATT_TK
    gw = ATT_HPG * HEAD_W
    kcol = KB_TILE * (MIX_W // gw)
    hpg, r = ATT_HPG, 2 * ATT_TQ
    return pl.pallas_call(
        _attn_kernel,
        grid=(batch, HEADS // ATT_HPG),
        in_specs=[pl.BlockSpec(memory_space=pltpu.SMEM),
                  pl.BlockSpec(memory_space=pltpu.SMEM),
                  pl.BlockSpec((nq, gw, ATT_TQ), lambda b, g: (b, g, 0)),
                  pl.BlockSpec((seq, gw), lambda b, g: (b, kcol + g)),
                  pl.BlockSpec((nk, gw, ATT_TK), lambda b, g: (b, g, 0)),
                  pl.BlockSpec((1, HEAD_W), lambda b, g: (0, 0))],
        out_specs=pl.BlockSpec((seq, gw), lambda b, g: (b, g)),
        out_shape=jax.ShapeDtypeStruct((m, MIX_W), BF16),
        scratch_shapes=[pltpu.VMEM((hpg, seq, 2 * HEAD_W), BF16),
                        pltpu.VMEM((hpg, 2 * HEAD_W, r), BF16),
                        pltpu.VMEM((hpg, HEAD_W, r), F32),
                        pltpu.VMEM((ATT_TK, r), F32),
                        pltpu.VMEM((ATT_TK, r), F32),
                        pltpu.VMEM((hpg, ATT_TK, r), BF16),
                        pltpu.VMEM((hpg, ATT_TK, r), BF16),
                        pltpu.VMEM((hpg, 8, r), F32),
                        pltpu.VMEM((hpg, 8, r), F32)],
        compiler_params=pltpu.CompilerParams(dimension_semantics=("parallel", "parallel"),
                                             vmem_limit_bytes=VMEM_LIMIT),
        name="diffattn",
    )(lam, slopes, qt, proj, vt, gain)


def _channel_kernel(ya_ref, yb_ref, ga0_ref, ga1_ref, gb0_ref, gb1_ref, x_ref,
                    wua_ref, wub_ref, wo_ref, wg_ref, wu_ref, wd_ref,
                    gmix_ref, gpre_ref, gpost_ref, gnext_ref, x2_ref, *maybe_hn_ref):
    ua = jnp.dot(ya_ref[...], wua_ref[...], preferred_element_type=F32)
    ub = jnp.dot(yb_ref[...], wub_ref[...], preferred_element_type=F32)
    ga = jnp.concatenate([ga0_ref[...], ga1_ref[...]], axis=1).astype(F32)
    gb = jnp.concatenate([gb0_ref[...], gb1_ref[...]], axis=1).astype(F32)
    merged = (ga * ua + gb * ub).astype(BF16)
    mix = jnp.dot(merged, wo_ref[...], preferred_element_type=F32)
    x1 = x_ref[...] + _rms(mix, gmix_ref[...])
    h = _rms(x1, gpre_ref[...]).astype(BF16)

    acc = jnp.zeros((CH_TM, D_MODEL), F32)
    for c in range(D_FF // FF_TF):
        cs = slice(c * FF_TF, (c + 1) * FF_TF)
        g = jnp.dot(h, wg_ref[:, cs], preferred_element_type=F32)
        u = jnp.dot(h, wu_ref[:, cs], preferred_element_type=F32)
        ff = (g * _sigmoid(g) * u).astype(BF16)
        acc = acc + jnp.dot(ff, wd_ref[cs, :], preferred_element_type=F32)
    x2 = x1 + _rms(acc, gpost_ref[...])
    x2_ref[...] = x2
    for hn_ref in maybe_hn_ref:
        hn_ref[...] = _rms(x2, gnext_ref[...]).astype(BF16)


def _channel(ya, yb, proj, x2d, wua, wub, wo, wg, wu, wd, layer, gmix, gpre, gpost, gnext, emit_next):
    m = ya.shape[0]
    tm = CH_TM

    def row(width, col=0):
        return pl.BlockSpec((tm, width), lambda i: (i, col))

    out_specs = [row(D_MODEL)] + ([row(D_MODEL)] if emit_next else [])
    out_shape = ([jax.ShapeDtypeStruct((m, D_MODEL), F32)]
                 + ([jax.ShapeDtypeStruct((m, D_MODEL), BF16)] if emit_next else []))
    return pl.pallas_call(
        _channel_kernel,
        grid=(m // tm,),
        in_specs=[row(MIX_W), row(MIX_W)] + [row(MIX_W, GATE_TILE + g) for g in range(4)] + [
                  row(D_MODEL),
                  _resident((MIX_W, D_MODEL), layer), _resident((MIX_W, D_MODEL), layer),
                  _resident((D_MODEL, D_MODEL), layer),
                  _resident((D_MODEL, D_FF), layer), _resident((D_MODEL, D_FF), layer),
                  _resident((D_FF, D_MODEL), layer),
                  _resident((1, D_MODEL)), _resident((1, D_MODEL)), _resident((1, D_MODEL)),
                  _resident((1, D_MODEL))],
        out_specs=out_specs,
        out_shape=out_shape,
        compiler_params=pltpu.CompilerParams(dimension_semantics=("parallel",),
                                             vmem_limit_bytes=VMEM_LIMIT),
        name="channel",
    )(ya, yb, proj, proj, proj, proj, x2d, wua, wub, wo, wg, wu, wd, gmix, gpre, gpost, gnext)


def kernel(x, lower_bounds, norm_mix_pre, norm_mix_post, norm_ffn_pre, norm_ffn_post, w_in, hg_out_norm,
           da_subln, lambda_q1, lambda_k1, lambda_q2, lambda_k2, w_up_a, w_up_b, w_out, w_ffn_gate,
           w_ffn_up, w_ffn_down):
    batch, seq, _ = x.shape
    depth = w_in.shape[0]
    m = batch * seq
    x2d = x.reshape(m, D_MODEL).astype(F32)

    lb_all = jnp.cumsum(jax.nn.softmax(lower_bounds.astype(F32), axis=0), axis=0)
    lb_all = lb_all - lb_all[0:1]

    w_in, w_up_a, w_up_b, w_out, w_ffn_gate, w_ffn_up, w_ffn_down = (
        w.astype(BF16) for w in (w_in, w_up_a, w_up_b, w_out, w_ffn_gate, w_ffn_up, w_ffn_down))

    h = None
    for l in range(depth):
        lb = lb_all[l]
        lbp = jnp.zeros((8, MIX_W), F32)
        lbp = lbp.at[0].set(jnp.maximum(jnp.log(lb), NEG_BIG)).at[1].set(jnp.log1p(-lb)).at[2].set(1.0 - lb)
        proj, lf, lfmin, qt, vt = _in_proj(x2d if h is None else h, norm_mix_pre[l].reshape(1, D_MODEL),
                                           w_in, l, lbp, prenorm=h is None)

        ya = _hgrn(proj, lf, lfmin[:, 0, 0], hg_out_norm[l].reshape(1, HEAD_W), batch, seq)

        lam_init = 0.8 - 0.6 * math.exp(-0.3 * l)
        lam = (jnp.exp(jnp.sum(lambda_q1[l].astype(F32) * lambda_k1[l].astype(F32)))
               - jnp.exp(jnp.sum(lambda_q2[l].astype(F32) * lambda_k2[l].astype(F32))) + lam_init)
        sub_gain = (da_subln[l].astype(F32) * (1.0 - lam_init)).reshape(1, HEAD_W)
        yb = _attn(lam.reshape(1), proj, qt, vt, sub_gain, batch, seq)

        last = l == depth - 1
        g_next = norm_mix_pre[(l + 1) % depth].reshape(1, D_MODEL)
        outs = _channel(ya, yb, proj, x2d, w_up_a, w_up_b, w_out, w_ffn_gate, w_ffn_up, w_ffn_down, l,
                        norm_mix_post[l].reshape(1, D_MODEL),
                        norm_ffn_pre[l].reshape(1, D_MODEL), norm_ffn_post[l].reshape(1, D_MODEL),
                        g_next, emit_next=not last)
        x2d, h = (outs[0], None) if last else outs
    return x2d.reshape(batch, seq, D_MODEL)
```

```python
import functools
import math

import jax
import jax.numpy as jnp
from jax import lax
from jax.experimental import pallas as pl
from jax.experimental.pallas import tpu as pltpu

F32 = jnp.float32
BF16 = jnp.bfloat16

D_MODEL = 1024
HEADS = 4
HEAD_W = 128
MIX_W = HEADS * HEAD_W
DQK = 64
D_FF = 2816
D_IN = 7 * MIX_W + 2 * D_MODEL
EPS = 1e-6
ALIBI_MAX_BIAS = 8.0
NEG_BIG = -1e30
LOG2E = math.log2(math.e)

IN_TM, IN_TN = 512, 512
IN_ROW_PARTS = 4
HG_T, HG_C = 512, 64
HG_GROUP = 8
HG_MAX_FACTORED_DECAY = 80.0
ATT_TQ, ATT_TK = 512, 512
CH_TM, FF_TF = 512, 256
CH_ROW_PARTS = 4
VMEM_LIMIT = 56 * 1024 * 1024


def _sigmoid(x):
    return 1.0 / (1.0 + jnp.exp(-x))


def _rms(x, gain):
    return x * lax.rsqrt(jnp.mean(x * x, axis=-1, keepdims=True) + EPS) * gain


def _resident(shape, layer=None):
    if layer is None:
        return pl.BlockSpec(shape, lambda *_: (0,) * len(shape), pipeline_mode=pl.Buffered(1))
    return pl.BlockSpec((None,) + tuple(shape), lambda *_: (layer,) + (0,) * len(shape),
                        pipeline_mode=pl.Buffered(1))


PROJ_TILES = 9
PROJ_W = PROJ_TILES * 512
KB_TILE, GATE_TILE = 4, 5


def _in_proj_kernel(x_ref, g_ref, w_ref, lb_ref, o_ref, lf_ref, lfmin_ref, qt_ref, vt_ref, h_ref, *,
                    prenorm):
    if prenorm:
        h_ref[...] = _rms(x_ref[...], g_ref[...]).astype(BF16)
    else:
        h_ref[...] = x_ref[...]

    def silu(acc):
        return acc * _sigmoid(acc)

    lf_mins = []

    def forget_gate(acc, rs):
        log_lb, log_1m_lb, one_m_lb = lb_ref[0:1, :], lb_ref[1:2, :], lb_ref[2:3, :]
        e = jnp.exp(-jnp.abs(acc))
        log_sig = jnp.minimum(acc, 0.0) - jnp.log(1.0 + e)
        c = log_1m_lb + log_sig
        hi = jnp.maximum(log_lb, c)
        lf = hi + jnp.log(1.0 + jnp.exp(-jnp.abs(log_lb - c)))
        lf_ref[rs, :] = lf
        lf_mins.append(jnp.min(lf))
        return one_m_lb * (jnp.where(acc >= 0.0, e, 1.0) / (1.0 + e))

    def query_scale(acc):
        return acc * (LOG2E / math.sqrt(DQK))

    row_parts = [slice(r0, r0 + IN_TM // IN_ROW_PARTS) for r0 in range(0, IN_TM, IN_TM // IN_ROW_PARTS)]

    def column_tile(jt, rs):
        return jnp.dot(h_ref[rs, :], w_ref[:, jt * IN_TN:(jt + 1) * IN_TN], preferred_element_type=F32)

    plan = [(0, silu, 0), (3, silu, 3)] + [(7 + g, _sigmoid, GATE_TILE + g) for g in range(4)]
    tail = [(2, None, 2), (5, None, KB_TILE)]
    for rs in row_parts:
        o_ref[rs, IN_TN:2 * IN_TN] = forget_gate(column_tile(1, rs), rs).astype(BF16)
    lfmin_ref[0] = jnp.full((8, HEAD_W), functools.reduce(jnp.minimum, lf_mins), F32)
    for jt, epilogue, out_tile in plan:
        for rs in row_parts:
            o_ref[rs, out_tile * IN_TN:(out_tile + 1) * IN_TN] = epilogue(column_tile(jt, rs)).astype(BF16)
    for rs in row_parts:
        qt_ref[0, :, rs] = query_scale(column_tile(4, rs)).T.astype(BF16)
        vt_ref[0, :, rs] = column_tile(6, rs).T.astype(BF16)
    for jt, _, out_tile in tail:
        for rs in row_parts:
            o_ref[rs, out_tile * IN_TN:(out_tile + 1) * IN_TN] = column_tile(jt, rs).astype(BF16)


def _in_proj(x_or_h, gain, w_all, layer, lbp, prenorm):
    m = x_or_h.shape[0]
    nt = m // IN_TM
    return pl.pallas_call(
        functools.partial(_in_proj_kernel, prenorm=prenorm),
        grid=(nt,),
        in_specs=[pl.BlockSpec((IN_TM, D_MODEL), lambda i: (i, 0)),
                  _resident((1, D_MODEL)), _resident((D_MODEL, D_IN), layer), _resident((8, MIX_W))],
        out_specs=[pl.BlockSpec((IN_TM, PROJ_W), lambda i: (i, 0)),
                   pl.BlockSpec((IN_TM, MIX_W), lambda i: (i, 0)),
                   pl.BlockSpec((1, 8, HEAD_W), lambda i: (i, 0, 0)),
                   pl.BlockSpec((1, MIX_W, IN_TM), lambda i: (i, 0, 0)),
                   pl.BlockSpec((1, MIX_W, IN_TM), lambda i: (i, 0, 0))],
        out_shape=[jax.ShapeDtypeStruct((m, PROJ_W), BF16),
                   jax.ShapeDtypeStruct((m, MIX_W), F32),
                   jax.ShapeDtypeStruct((nt, 8, HEAD_W), F32),
                   jax.ShapeDtypeStruct((nt, MIX_W, IN_TM), BF16),
                   jax.ShapeDtypeStruct((nt, MIX_W, IN_TM), BF16)],
        scratch_shapes=[pltpu.VMEM((IN_TM, D_MODEL), BF16)],
        compiler_params=pltpu.CompilerParams(dimension_semantics=("parallel",),
                                             vmem_limit_bytes=VMEM_LIMIT),
        name="in_proj",
    )(x_or_h, gain, w_all, lbp)


def _row_bcast(b, rows, w):
    return jnp.concatenate([jnp.broadcast_to(b[r:r + 1, :], (w, HEAD_W)) for r in rows], axis=0)


def _hgrn_kernel(lfmin_ref, q_ref, lf_ref, k_ref, v_ref, g_ref, gain_ref, o_ref, st_ref):
    @pl.when(pl.program_id(1) == 0)
    def _():
        st_ref[...] = jnp.zeros_like(st_ref)

    c = HG_C
    ri = lax.broadcasted_iota(jnp.int32, (c, c), 0)
    ci = lax.broadcasted_iota(jnp.int32, (c, c), 1)
    tri = (ci <= ri).astype(BF16)
    level_masks = []
    for w in (32, 16, 8):
        tb, sb = ri // w, ci // w
        level_masks.append((tb == sb + 1) & (tb % 2 == 1))
    diag_masks = [(ci == (ri // 8) * 8 + s) & (ri % 8 >= s) for s in range(8)]
    same_block_mask = (ri // 8 == ci // 8) & (ci <= ri)
    ones_sq = jnp.ones((HEAD_W, HEAD_W), BF16)
    gain = gain_ref[...]

    def chunk_group(ig, carry, factored):
        nt_dims = (((1,), (1,)), ((), ()))
        items = [(pl.multiple_of((ig * HG_GROUP + ch) * c, c), slice(h * HEAD_W, (h + 1) * HEAD_W), h)
                 for ch in range(HG_GROUP) for h in range(HEADS)]
        n = len(items)
        q = [q_ref[pl.ds(r0, c), cs].astype(F32) for r0, cs, _ in items]
        k = [k_ref[pl.ds(r0, c), cs].astype(F32) for r0, cs, _ in items]
        v_bf = [v_ref[pl.ds(r0, c), cs] for r0, cs, _ in items]

        b = []
        for r0, cs, _ in items:
            lf = lf_ref[pl.ds(r0, c), cs]
            lf_hi = lf.astype(BF16)
            lf_lo = (lf - lf_hi.astype(F32)).astype(BF16)
            b.append(jnp.dot(tri, lf_hi, preferred_element_type=F32)
                     + jnp.dot(tri, lf_lo, preferred_element_type=F32))
        b_last = [bh[c - 1:c, :] for bh in b]

        upd = []
        for it in range(n):
            ks = (k[it] * jnp.exp(b_last[it] - b[it])).astype(BF16)
            upd.append(lax.dot_general(v_bf[it], ks, (((0,), (0,)), ((), ())), preferred_element_type=F32))
        a = [jnp.zeros((c, c), F32) for _ in items]
        for w, msk in zip((32, 16, 8), level_masks):
            nb = c // w
            for it in range(n):
                start = _row_bcast(b[it], [j * w for j in range(nb)], w)
                nxt = _row_bcast(b[it], [(j + 1) * w for j in range(nb - 1)] + [c - 1], w)
                qh = (q[it] * jnp.exp(b[it] - start)).astype(BF16)
                kh = (k[it] * jnp.exp(nxt - b[it])).astype(BF16)
                aw = lax.dot_general(qh, kh, nt_dims, preferred_element_type=F32)
                a[it] = jnp.where(msk, aw, a[it])
                if factored and w == 8:
                    kb = (k[it] * jnp.exp(start - b[it])).astype(BF16)
                    aw = lax.dot_general(qh, kb, nt_dims, preferred_element_type=F32)
                    a[it] = jnp.where(same_block_mask, aw, a[it])

        if not factored:
            for it in range(n):
                b3 = b[it].reshape(c // 8, 8, HEAD_W)
                q3 = q[it].reshape(c // 8, 8, HEAD_W)
                k3 = k[it].reshape(c // 8, 8, HEAD_W)
                terms = []
                for s in range(8):
                    dec = jnp.exp(jnp.minimum(b3 - b3[:, s:s + 1, :], 0.0))
                    terms.append((q3 * dec * k3[:, s:s + 1, :]).reshape(c, HEAD_W).astype(BF16))
                sums = jnp.dot(jnp.concatenate(terms, axis=0), ones_sq, preferred_element_type=F32)
                for s in range(8):
                    a[it] = jnp.where(diag_masks[s], sums[s * c:(s + 1) * c, 0:c], a[it])

        o = [None] * n
        for h in range(HEADS):
            st = st_ref[h]
            for it in range(h, n, HEADS):
                qi = (q[it] * jnp.exp(b[it])).astype(BF16)
                o[it] = lax.dot_general(qi, st.astype(BF16), nt_dims, preferred_element_type=F32)
                st = st * jnp.exp(b_last[it]) + upd[it]
            st_ref[h] = st
        for it, (r0, cs, _) in enumerate(items):
            oh = o[it] + jnp.dot(a[it].astype(BF16), v_bf[it], preferred_element_type=F32)
            g = g_ref[pl.ds(r0, c), cs].astype(F32)
            o_ref[pl.ds(r0, c), cs] = (_rms(oh, gain) * g).astype(BF16)
        return carry

    tile = pl.program_id(0) * pl.num_programs(1) + pl.program_id(1)
    mild = lfmin_ref[tile] * -8.0 < HG_MAX_FACTORED_DECAY
    n_groups = HG_T // (c * HG_GROUP)

    @pl.when(mild)
    def _():
        lax.fori_loop(0, n_groups, functools.partial(chunk_group, factored=True), 0)

    @pl.when(jnp.logical_not(mild))
    def _():
        lax.fori_loop(0, n_groups, functools.partial(chunk_group, factored=False), 0)


def _hgrn(proj, lf, lfmin, gain, batch, seq):
    m = proj.shape[0]
    nt = seq // HG_T
    assert HG_T == IN_TM

    def spec(col):
        return pl.BlockSpec((HG_T, MIX_W), lambda b, t: (b * nt + t, col))

    return pl.pallas_call(
        _hgrn_kernel,
        grid=(batch, nt),
        in_specs=[pl.BlockSpec(memory_space=pltpu.SMEM),
                  spec(0), spec(0), spec(1), spec(2), spec(3),
                  pl.BlockSpec((1, HEAD_W), lambda b, t: (0, 0))],
        out_specs=spec(0),
        out_shape=jax.ShapeDtypeStruct((m, MIX_W), BF16),
        scratch_shapes=[pltpu.VMEM((HEADS, HEAD_W, HEAD_W), F32)],
        compiler_params=pltpu.CompilerParams(dimension_semantics=("parallel", "arbitrary"),
                                             vmem_limit_bytes=VMEM_LIMIT),
        name="hgrn2",
    )(lfmin, proj, lf, proj, proj, proj, gain)


ATT_HPG = 2
ATT_MAX_SCORE_BOUND = 60.0


def _attn_kernel(lam_ref, slope_ref, qt_ref, k_ref, vt_ref, gain_ref, o_ref,
                 kaug_ref, q2t_ref, acc_ref, sa_ref, sb_ref, pa_ref, pb_ref, stat_ref, lsum_ref):
    tq, r, tk = ATT_TQ, 2 * ATT_TQ, ATT_TK
    grp = pl.program_id(1)
    nblk = k_ref.shape[0] // tk
    slab = 64
    M_ROW, L_ROW, ALPHA_ROW, BMAX_ROW, KNORM_ROW, MREF_ROW = (slice(k, k + 1) for k in range(6))

    def col_reduce(x, op):
        return op(x.reshape(slab // 8, 8, r), axis=0)

    di = lax.broadcasted_iota(jnp.int32, (HEAD_W, HEAD_W), 0)
    ci = lax.broadcasted_iota(jnp.int32, (HEAD_W, HEAD_W), 1)
    same_comp = ((di < DQK) == (ci < DQK)).astype(BF16)

    def head(hh, i):
        cols = slice(hh * HEAD_W, (hh + 1) * HEAD_W)
        kaug, q2t, acc, stat = kaug_ref.at[hh], q2t_ref.at[hh], acc_ref.at[hh], stat_ref.at[hh]
        sa, sb = sa_ref, sb_ref
        pa, pb = pa_ref.at[hh], pb_ref.at[hh]
        lsum = lsum_ref.at[hh]
        slope = slope_ref[grp * ATT_HPG + hh] * LOG2E

        def split3(c, first_lane, lane):
            c1 = c.astype(BF16).astype(F32)
            c2 = (c - c1).astype(BF16).astype(F32)
            c3 = c - c1 - c2
            return jnp.where(lane == first_lane, c1,
                             jnp.where(lane == first_lane + 1, c2,
                                       jnp.where(lane == first_lane + 2, c3, 0.0)))

        def local_bias():
            row = lax.broadcasted_iota(jnp.int32, (tk, HEAD_W), 0)
            lane = lax.broadcasted_iota(jnp.int32, (tk, HEAD_W), 1)
            return split3(slope * row.astype(F32), 0, lane)

        def fill(jb, knorm_sq, local):
            r0 = pl.multiple_of(jb * tk, tk)
            k_blk = k_ref[pl.ds(r0, tk), cols]
            ksq = (k_blk.astype(F32) * k_blk.astype(F32)).astype(BF16)
            sums = jnp.dot(ksq, same_comp, preferred_element_type=F32)
            knorm_sq = jnp.maximum(knorm_sq, jnp.max(sums, axis=0, keepdims=True))
            kaug[pl.ds(r0, tk), 0:HEAD_W] = k_blk
            lane = lax.broadcasted_iota(jnp.int32, (1, HEAD_W), 1)
            base = split3(jnp.full((1, HEAD_W), slope * (jb * tk).astype(F32), F32), 3, lane)
            kaug[pl.ds(r0, tk), HEAD_W:2 * HEAD_W] = (local + base).astype(BF16)
            return knorm_sq

        def set_key_norms(knorm_sq):
            knorm = jnp.sqrt(knorm_sq * 1.02)
            stat[KNORM_ROW, :] = jnp.concatenate([jnp.broadcast_to(knorm[:, 0:1], (1, tq)),
                                                  jnp.broadcast_to(knorm[:, DQK:DQK + 1], (1, tq))], axis=1)

        def scores(j):
            r0 = pl.multiple_of(j * tk, tk)
            return jnp.dot(kaug[pl.ds(r0, tk), :], q2t[...], preferred_element_type=F32)

        def store_scores(s, s_out):
            part = jnp.full((8, r), NEG_BIG, F32)
            for a in range(0, tk, slab):
                sl = s[a:a + slab, :]
                s_out[a:a + slab, :] = sl
                part = jnp.maximum(part, col_reduce(sl, jnp.max))
            return jnp.max(part, axis=0, keepdims=True)

        def softmax(s_of, p_out, bmax, m_old, l_old):
            m_new = jnp.maximum(m_old, bmax)
            alpha = jnp.exp2(m_old - m_new)
            part = jnp.zeros((8, r), F32)
            for a in range(0, tk, slab):
                p = jnp.exp2(s_of(slice(a, a + slab)) - m_new)
                part = part + col_reduce(p, jnp.sum)
                p_out[a:a + slab, :] = p.astype(BF16)
            l_new = alpha * l_old + jnp.sum(part, axis=0, keepdims=True)
            return alpha, m_new, l_new

        def add_values(jv, p, alpha):
            acc[...] = alpha * acc[...] + jnp.dot(vt_ref[jv, cols, :], p, preferred_element_type=F32)

        def load_queries():
            qt = qt_ref[i, cols, :].astype(F32)
            sub = lax.broadcasted_iota(jnp.int32, (HEAD_W, tq), 0)
            q2t[0:HEAD_W, 0:tq] = jnp.where(sub < DQK, qt, 0.0).astype(BF16)
            q2t[0:HEAD_W, tq:r] = jnp.where(sub >= DQK, qt, 0.0).astype(BF16)
            sub_r = lax.broadcasted_iota(jnp.int32, (HEAD_W, r), 0)
            q2t[HEAD_W:2 * HEAD_W, :] = jnp.where(sub_r < 6, 1.0, 0.0).astype(BF16)
            qsq = qt * qt
            qnorm = jnp.sqrt(jnp.concatenate([jnp.sum(qsq[0:DQK, :], axis=0, keepdims=True),
                                              jnp.sum(qsq[DQK:HEAD_W, :], axis=0, keepdims=True)], axis=1))
            bound = qnorm * stat[KNORM_ROW, :]
            ql = lax.broadcasted_iota(jnp.int32, (1, r), 1)
            q_pos = (i * tq + jnp.where(ql >= tq, ql - tq, ql)).astype(F32)
            stat[MREF_ROW, :] = bound + slope * q_pos + 1.0
            return bound

        def bounded_start():
            acc[...] = jnp.zeros((HEAD_W, r), F32)
            lsum[...] = jnp.zeros((8, r), F32)

        def bounded_block(j, p_out, masked):
            s = scores(j)
            mref = stat[MREF_ROW, :]
            if masked:
                ql = lax.broadcasted_iota(jnp.int32, (slab, r), 1)
                lead = i * tq - j * tk + jnp.where(ql >= tq, ql - tq, ql)
                row = lax.broadcasted_iota(jnp.int32, (slab, r), 0)
            part = lsum[...]
            for a in range(0, tk, slab):
                e = s[a:a + slab, :] - mref
                if masked:
                    e = jnp.where(row + a <= lead, e, NEG_BIG)
                p = jnp.exp2(e)
                part = part + col_reduce(p, jnp.sum)
                p_out[a:a + slab, :] = p.astype(BF16)
            lsum[...] = part

        def bounded_values(jv, p_in):
            acc[...] = acc[...] + jnp.dot(vt_ref[jv, cols, :], p_in[...], preferred_element_type=F32)

        def bounded_finish():
            write_output(jnp.sum(lsum[...], axis=0, keepdims=True))

        def write_output(l_fin):
            on = acc[...] * (1.0 / l_fin)
            ot = on[:, 0:tq] - lam_ref[0] * on[:, tq:r]
            ot = ot * lax.rsqrt(jnp.mean(ot * ot, axis=0, keepdims=True) + EPS)
            o_ref[pl.ds(pl.multiple_of(i * tq, tq), tq), cols] = (ot.T * gain_ref[...]).astype(BF16)

        def start():
            acc[...] = jnp.zeros((HEAD_W, r), F32)
            pb[...] = jnp.zeros((tk, r), BF16)
            stat[BMAX_ROW, :] = store_scores(scores(0), sa)
            stat[M_ROW, :] = jnp.full((1, r), NEG_BIG, F32)
            stat[L_ROW, :] = jnp.zeros((1, r), F32)
            stat[ALPHA_ROW, :] = jnp.ones((1, r), F32)

        def stage(j, flip):
            s_in, s_out, p_in, p_out = (sb, sa, pa, pb) if flip else (sa, sb, pb, pa)
            bmax_next = store_scores(scores(j + 1), s_out)
            add_values(jnp.maximum(j - 1, 0), p_in[...], stat[ALPHA_ROW, :])
            alpha, m_new, l_new = softmax(lambda rows: s_in[rows, :], p_out, stat[BMAX_ROW, :],
                                          stat[M_ROW, :], stat[L_ROW, :])
            stat[M_ROW, :] = m_new
            stat[L_ROW, :] = l_new
            stat[ALPHA_ROW, :] = alpha
            stat[BMAX_ROW, :] = bmax_next

        def finish(j_last, flip):
            s_in, p_in, p_free = (sb, pa, pb) if flip else (sa, pb, pa)
            ql = lax.broadcasted_iota(jnp.int32, (slab, r), 1)
            lead = i * tq - j_last * tk + jnp.where(ql >= tq, ql - tq, ql)
            row = lax.broadcasted_iota(jnp.int32, (slab, r), 0)

            part = jnp.full((8, r), NEG_BIG, F32)
            for a in range(0, tk, slab):
                sl = jnp.where(row + a <= lead, s_in[a:a + slab, :], NEG_BIG)
                s_in[a:a + slab, :] = sl
                part = jnp.maximum(part, col_reduce(sl, jnp.max))
            add_values(jnp.maximum(j_last - 1, 0), p_in[...], stat[ALPHA_ROW, :])
            alpha, _, l_fin = softmax(lambda rows: s_in[rows, :], p_free,
                                      jnp.max(part, axis=0, keepdims=True),
                                      stat[M_ROW, :], stat[L_ROW, :])
            add_values(j_last, p_free[...], alpha)
            write_output(l_fin)

        return dict(fill=fill, local_bias=local_bias, set_key_norms=set_key_norms, load_queries=load_queries,
                    start=start, stage=stage, finish=finish,
                    bounded_start=bounded_start, bounded_block=bounded_block,
                    bounded_values=bounded_values, bounded_finish=bounded_finish, pa=pa, pb=pb)

    local_biases = [head(hh, 0)["local_bias"]() for hh in range(ATT_HPG)]

    def fill_all(jb, carry):
        return tuple(head(hh, 0)["fill"](jb, carry[hh], local_biases[hh]) for hh in range(ATT_HPG))
    knorm_sq = lax.fori_loop(0, nblk, fill_all, tuple(jnp.zeros((1, HEAD_W), F32) for _ in range(ATT_HPG)))
    for hh in range(ATT_HPG):
        head(hh, 0)["set_key_norms"](knorm_sq[hh])

    def query_tile(i, carry):
        heads = [head(hh, i) for hh in range(ATT_HPG)]
        n_full = (i * tq) // tk
        bound = functools.reduce(jnp.maximum, [h["load_queries"]() for h in heads])
        bounded = jnp.max(bound) < ATT_MAX_SCORE_BOUND

        def each(name, *args):
            for h in heads:
                h[name](*args)

        @pl.when(bounded)
        def _():
            each("bounded_start")
            for h in heads:
                h["bounded_block"](n_full, h["pb"], True)

            def pair(t, owed):
                for h in heads:
                    h["bounded_block"](2 * t, h["pa"], False)
                for h in heads:
                    h["bounded_values"](owed, h["pb"])
                for h in heads:
                    h["bounded_block"](2 * t + 1, h["pb"], False)
                for h in heads:
                    h["bounded_values"](2 * t, h["pa"])
                return 2 * t + 1
            owed = lax.fori_loop(0, n_full // 2, pair, n_full)

            @pl.when(n_full % 2 == 0)
            def _():
                for h in heads:
                    h["bounded_values"](owed, h["pb"])

            @pl.when(n_full % 2 == 1)
            def _():
                for h in heads:
                    h["bounded_block"](n_full - 1, h["pa"], False)
                for h in heads:
                    h["bounded_values"](owed, h["pb"])
                for h in heads:
                    h["bounded_values"](n_full - 1, h["pa"])
            each("bounded_finish")

        @pl.when(jnp.logical_not(bounded))
        def _():
            for h in heads:
                h["start"]()

                def pair(t, carry, h=h):
                    h["stage"](2 * t, False)
                    h["stage"](2 * t + 1, True)
                    return carry
                lax.fori_loop(0, n_full // 2, pair, 0)

                @pl.when(n_full % 2 == 0)
                def _(h=h):
                    h["finish"](n_full, False)

                @pl.when(n_full % 2 == 1)
                def _(h=h):
                    h["stage"](n_full - 1, False)
                    h["finish"](n_full, True)
        return carry

    lax.fori_loop(0, qt_ref.shape[0], query_tile, 0)


def _attn(lam, proj, qt, vt, gain, batch, seq):
    m = proj.shape[0]
    slopes = jnp.exp2(-(ALIBI_MAX_BIAS / HEADS) * jnp.arange(1, HEADS + 1, dtype=F32))
    assert ATT_TQ == IN_TM and ATT_TK == IN_TM
    nq = seq // ATT_TQ
    nk = seq // ATT_TK
    gw = ATT_HPG * HEAD_W
    kcol = KB_TILE * (MIX_W // gw)
    hpg, r = ATT_HPG, 2 * ATT_TQ
    return pl.pallas_call(
        _attn_kernel,
        grid=(batch, HEADS // ATT_HPG),
        in_specs=[pl.BlockSpec(memory_space=pltpu.SMEM),
                  pl.BlockSpec(memory_space=pltpu.SMEM),
                  pl.BlockSpec((nq, gw, ATT_TQ), lambda b, g: (b, g, 0)),
                  pl.BlockSpec((seq, gw), lambda b, g: (b, kcol + g)),
                  pl.BlockSpec((nk, gw, ATT_TK), lambda b, g: (b, g, 0)),
                  pl.BlockSpec((1, HEAD_W), lambda b, g: (0, 0))],
        out_specs=pl.BlockSpec((seq, gw), lambda b, g: (b, g)),
        out_shape=jax.ShapeDtypeStruct((m, MIX_W), BF16),
        scratch_shapes=[pltpu.VMEM((hpg, seq, 2 * HEAD_W), BF16),
                        pltpu.VMEM((hpg, 2 * HEAD_W, r), BF16),
                        pltpu.VMEM((hpg, HEAD_W, r), F32),
                        pltpu.VMEM((ATT_TK, r), F32),
                        pltpu.VMEM((ATT_TK, r), F32),
                        pltpu.VMEM((hpg, ATT_TK, r), BF16),
                        pltpu.VMEM((hpg, ATT_TK, r), BF16),
                        pltpu.VMEM((hpg, 8, r), F32),
                        pltpu.VMEM((hpg, 8, r), F32)],
        compiler_params=pltpu.CompilerParams(dimension_semantics=("parallel", "parallel"),
                                             vmem_limit_bytes=VMEM_LIMIT),
        name="diffattn",
    )(lam, slopes, qt, proj, vt, gain)


def _channel_kernel(ya_ref, yb_ref, ga0_ref, ga1_ref, gb0_ref, gb1_ref, x_ref,
                    wua_ref, wub_ref, wo_ref, wg_ref, wu_ref, wd_ref,
                    gmix_ref, gpre_ref, gpost_ref, gnext_ref, x2_ref, *maybe_hn_ref):
    x1_parts, h_parts = [], []
    for r0 in range(0, CH_TM, CH_TM // CH_ROW_PARTS):
        rs = slice(r0, r0 + CH_TM // CH_ROW_PARTS)
        ua = jnp.dot(ya_ref[rs, :], wua_ref[...], preferred_element_type=F32)
        ub = jnp.dot(yb_ref[rs, :], wub_ref[...], preferred_element_type=F32)
        ga = jnp.concatenate([ga0_ref[rs, :], ga1_ref[rs, :]], axis=1).astype(F32)
        gb = jnp.concatenate([gb0_ref[rs, :], gb1_ref[rs, :]], axis=1).astype(F32)
        merged = (ga * ua + gb * ub).astype(BF16)
        mix = jnp.dot(merged, wo_ref[...], preferred_element_type=F32)
        x1_part = x_ref[rs, :] + _rms(mix, gmix_ref[...])
        x1_parts.append(x1_part)
        h_parts.append(_rms(x1_part, gpre_ref[...]).astype(BF16))
    x1 = jnp.concatenate(x1_parts, axis=0)
    h = jnp.concatenate(h_parts, axis=0)

    acc = jnp.zeros((CH_TM, D_MODEL), F32)
    for c in range(D_FF // FF_TF):
        cs = slice(c * FF_TF, (c + 1) * FF_TF)
        g = jnp.dot(h, wg_ref[:, cs], preferred_element_type=F32)
        u = jnp.dot(h, wu_ref[:, cs], preferred_element_type=F32)
        ff = (g * _sigmoid(g) * u).astype(BF16)
        acc = acc + jnp.dot(ff, wd_ref[cs, :], preferred_element_type=F32)
    x2 = x1 + _rms(acc, gpost_ref[...])
    x2_ref[...] = x2
    for hn_ref in maybe_hn_ref:
        hn_ref[...] = _rms(x2, gnext_ref[...]).astype(BF16)


def _channel(ya, yb, proj, x2d, wua, wub, wo, wg, wu, wd, layer, gmix, gpre, gpost, gnext, emit_next):
    m = ya.shape[0]
    tm = CH_TM

    def row(width, col=0):
        return pl.BlockSpec((tm, width), lambda i: (i, col))

    out_specs = [row(D_MODEL)] + ([row(D_MODEL)] if emit_next else [])
    out_shape = ([jax.ShapeDtypeStruct((m, D_MODEL), F32)]
                 + ([jax.ShapeDtypeStruct((m, D_MODEL), BF16)] if emit_next else []))
    return pl.pallas_call(
        _channel_kernel,
        grid=(m // tm,),
        in_specs=[row(MIX_W), row(MIX_W)] + [row(MIX_W, GATE_TILE + g) for g in range(4)] + [
                  row(D_MODEL),
                  _resident((MIX_W, D_MODEL), layer), _resident((MIX_W, D_MODEL), layer),
                  _resident((D_MODEL, D_MODEL), layer),
                  _resident((D_MODEL, D_FF), layer), _resident((D_MODEL, D_FF), layer),
                  _resident((D_FF, D_MODEL), layer),
                  _resident((1, D_MODEL)), _resident((1, D_MODEL)), _resident((1, D_MODEL)),
                  _resident((1, D_MODEL))],
        out_specs=out_specs,
        out_shape=out_shape,
        compiler_params=pltpu.CompilerParams(dimension_semantics=("parallel",),
                                             vmem_limit_bytes=VMEM_LIMIT),
        name="channel",
    )(ya, yb, proj, proj, proj, proj, x2d, wua, wub, wo, wg, wu, wd, gmix, gpre, gpost, gnext)


def kernel(x, lower_bounds, norm_mix_pre, norm_mix_post, norm_ffn_pre, norm_ffn_post, w_in, hg_out_norm,
           da_subln, lambda_q1, lambda_k1, lambda_q2, lambda_k2, w_up_a, w_up_b, w_out, w_ffn_gate,
           w_ffn_up, w_ffn_down):
    batch, seq, _ = x.shape
    depth = w_in.shape[0]
    m = batch * seq
    x2d = x.reshape(m, D_MODEL).astype(F32)

    lb_all = jnp.cumsum(jax.nn.softmax(lower_bounds.astype(F32), axis=0), axis=0)
    lb_all = lb_all - lb_all[0:1]

    w_in, w_up_a, w_up_b, w_out, w_ffn_gate, w_ffn_up, w_ffn_down = (
        w.astype(BF16) for w in (w_in, w_up_a, w_up_b, w_out, w_ffn_gate, w_ffn_up, w_ffn_down))

    h = None
    for l in range(depth):
        lb = lb_all[l]
        lbp = jnp.zeros((8, MIX_W), F32)
        lbp = lbp.at[0].set(jnp.maximum(jnp.log(lb), NEG_BIG)).at[1].set(jnp.log1p(-lb)).at[2].set(1.0 - lb)
        proj, lf, lfmin, qt, vt = _in_proj(x2d if h is None else h, norm_mix_pre[l].reshape(1, D_MODEL),
                                           w_in, l, lbp, prenorm=h is None)

        ya = _hgrn(proj, lf, lfmin[:, 0, 0], hg_out_norm[l].reshape(1, HEAD_W), batch, seq)

        lam_init = 0.8 - 0.6 * math.exp(-0.3 * l)
        lam = (jnp.exp(jnp.sum(lambda_q1[l].astype(F32) * lambda_k1[l].astype(F32)))
               - jnp.exp(jnp.sum(lambda_q2[l].astype(F32) * lambda_k2[l].astype(F32))) + lam_init)
        sub_gain = (da_subln[l].astype(F32) * (1.0 - lam_init)).reshape(1, HEAD_W)
        yb = _attn(lam.reshape(1), proj, qt, vt, sub_gain, batch, seq)

        last = l == depth - 1
        g_next = norm_mix_pre[(l + 1) % depth].reshape(1, D_MODEL)
        outs = _channel(ya, yb, proj, x2d, w_up_a, w_up_b, w_out, w_ffn_gate, w_ffn_up, w_ffn_down, l,
                        norm_mix_post[l].reshape(1, D_MODEL),
                        norm_ffn_pre[l].reshape(1, D_MODEL), norm_ffn_post[l].reshape(1, D_MODEL),
                        g_next, emit_next=not last)
        x2d, h = (outs[0], None) if last else outs
    return x2d.reshape(batch, seq, D_MODEL)
```

```python
import functools
import math

import jax
import jax.numpy as jnp
from jax import lax
from jax.experimental import pallas as pl
from jax.experimental.pallas import tpu as pltpu

F32 = jnp.float32
BF16 = jnp.bfloat16

D_MODEL = 1024
HEADS = 4
HEAD_W = 128
MIX_W = HEADS * HEAD_W
DQK = 64
D_FF = 2816
D_IN = 7 * MIX_W + 2 * D_MODEL
EPS = 1e-6
ALIBI_MAX_BIAS = 8.0
NEG_BIG = -1e30
LOG2E = math.log2(math.e)

IN_TM, IN_TN = 512, 512
IN_ROW_PARTS = 4
HG_T, HG_C = 512, 64
HG_GROUP = 8
HG_MAX_FACTORED_DECAY = 80.0
ATT_TQ, ATT_TK = 512, 512
CH_TM, FF_TF = 512, 256
CH_ROW_PARTS = 4
VMEM_LIMIT = 56 * 1024 * 1024


def _sigmoid(x):
    return 1.0 / (1.0 + jnp.exp(-x))


def _rms(x, gain):
    return x * lax.rsqrt(jnp.mean(x * x, axis=-1, keepdims=True) + EPS) * gain


def _resident(shape, layer=None):
    if layer is None:
        return pl.BlockSpec(shape, lambda *_: (0,) * len(shape), pipeline_mode=pl.Buffered(1))
    return pl.BlockSpec((None,) + tuple(shape), lambda *_: (layer,) + (0,) * len(shape),
                        pipeline_mode=pl.Buffered(1))


PROJ_TILES = 9
PROJ_W = PROJ_TILES * 512
KB_TILE, GATE_TILE = 4, 5


def _in_proj_kernel(x_ref, g_ref, w_ref, lb_ref, o_ref, lf_ref, lfmin_ref, qt_ref, vt_ref, h_ref, *,
                    prenorm):
    if prenorm:
        h_ref[...] = _rms(x_ref[...], g_ref[...]).astype(BF16)
    else:
        h_ref[...] = x_ref[...]

    def silu(acc):
        return acc * _sigmoid(acc)

    lf_mins = []

    def forget_gate(acc, rs):
        log_lb, log_1m_lb, one_m_lb = lb_ref[0:1, :], lb_ref[1:2, :], lb_ref[2:3, :]
        e = jnp.exp(-jnp.abs(acc))
        log_sig = jnp.minimum(acc, 0.0) - jnp.log(1.0 + e)
        c = log_1m_lb + log_sig
        hi = jnp.maximum(log_lb, c)
        lf = hi + jnp.log(1.0 + jnp.exp(-jnp.abs(log_lb - c)))
        lf_ref[rs, :] = lf
        lf_mins.append(jnp.min(lf))
        return one_m_lb * (jnp.where(acc >= 0.0, e, 1.0) / (1.0 + e))

    def query_scale(acc):
        return acc * (LOG2E / math.sqrt(DQK))

    row_parts = [slice(r0, r0 + IN_TM // IN_ROW_PARTS) for r0 in range(0, IN_TM, IN_TM // IN_ROW_PARTS)]

    def column_tile(jt, rs):
        return jnp.dot(h_ref[rs, :], w_ref[:, jt * IN_TN:(jt + 1) * IN_TN], preferred_element_type=F32)

    plan = [(0, silu, 0), (3, silu, 3)] + [(7 + g, _sigmoid, GATE_TILE + g) for g in range(4)]
    tail = [(2, None, 2), (5, None, KB_TILE)]
    for rs in row_parts:
        o_ref[rs, IN_TN:2 * IN_TN] = forget_gate(column_tile(1, rs), rs).astype(BF16)
    lfmin_ref[0] = jnp.full((8, HEAD_W), functools.reduce(jnp.minimum, lf_mins), F32)
    for jt, epilogue, out_tile in plan:
        for rs in row_parts:
            o_ref[rs, out_tile * IN_TN:(out_tile + 1) * IN_TN] = epilogue(column_tile(jt, rs)).astype(BF16)
    for rs in row_parts:
        qt_ref[0, :, rs] = query_scale(column_tile(4, rs)).T.astype(BF16)
        vt_ref[0, :, rs] = column_tile(6, rs).T.astype(BF16)
    for jt, _, out_tile in tail:
        for rs in row_parts:
            o_ref[rs, out_tile * IN_TN:(out_tile + 1) * IN_TN] = column_tile(jt, rs).astype(BF16)


def _in_proj(x_or_h, gain, w_all, layer, lbp, prenorm):
    m = x_or_h.shape[0]
    nt = m // IN_TM
    return pl.pallas_call(
        functools.partial(_in_proj_kernel, prenorm=prenorm),
        grid=(nt,),
        in_specs=[pl.BlockSpec((IN_TM, D_MODEL), lambda i: (i, 0)),
                  _resident((1, D_MODEL)), _resident((D_MODEL, D_IN), layer), _resident((8, MIX_W))],
        out_specs=[pl.BlockSpec((IN_TM, PROJ_W), lambda i: (i, 0)),
                   pl.BlockSpec((IN_TM, MIX_W), lambda i: (i, 0)),
                   pl.BlockSpec((1, 8, HEAD_W), lambda i: (i, 0, 0)),
                   pl.BlockSpec((1, MIX_W, IN_TM), lambda i: (i, 0, 0)),
                   pl.BlockSpec((1, MIX_W, IN_TM), lambda i: (i, 0, 0))],
        out_shape=[jax.ShapeDtypeStruct((m, PROJ_W), BF16),
                   jax.ShapeDtypeStruct((m, MIX_W), F32),
                   jax.ShapeDtypeStruct((nt, 8, HEAD_W), F32),
                   jax.ShapeDtypeStruct((nt, MIX_W, IN_TM), BF16),
                   jax.ShapeDtypeStruct((nt, MIX_W, IN_TM), BF16)],
        scratch_shapes=[pltpu.VMEM((IN_TM, D_MODEL), BF16)],
        compiler_params=pltpu.CompilerParams(dimension_semantics=("parallel",),
                                             vmem_limit_bytes=VMEM_LIMIT),
        name="in_proj",
    )(x_or_h, gain, w_all, lbp)


def _row_bcast(b, rows, w):
    return jnp.concatenate([jnp.broadcast_to(b[r:r + 1, :], (w, HEAD_W)) for r in rows], axis=0)


def _hgrn_kernel(lfmin_ref, q_ref, lf_ref, k_ref, v_ref, g_ref, gain_ref, o_ref, st_ref):
    @pl.when(pl.program_id(1) == 0)
    def _():
        st_ref[...] = jnp.zeros_like(st_ref)

    c = HG_C
    ri = lax.broadcasted_iota(jnp.int32, (c, c), 0)
    ci = lax.broadcasted_iota(jnp.int32, (c, c), 1)
    tri = (ci <= ri).astype(BF16)
    level_masks = []
    for w in (32, 16, 8):
        tb, sb = ri // w, ci // w
        level_masks.append((tb == sb + 1) & (tb % 2 == 1))
    diag_masks = [(ci == (ri // 8) * 8 + s) & (ri % 8 >= s) for s in range(8)]
    same_block_mask = (ri // 8 == ci // 8) & (ci <= ri)
    ones_sq = jnp.ones((HEAD_W, HEAD_W), BF16)
    gain = gain_ref[...]

    def chunk_group(ig, carry, factored):
        nt_dims = (((1,), (1,)), ((), ()))
        items = [(pl.multiple_of((ig * HG_GROUP + ch) * c, c), slice(h * HEAD_W, (h + 1) * HEAD_W), h)
                 for ch in range(HG_GROUP) for h in range(HEADS)]
        n = len(items)
        q = [q_ref[pl.ds(r0, c), cs].astype(F32) for r0, cs, _ in items]
        k = [k_ref[pl.ds(r0, c), cs].astype(F32) for r0, cs, _ in items]
        v_bf = [v_ref[pl.ds(r0, c), cs] for r0, cs, _ in items]

        b = []
        for r0, cs, _ in items:
            lf = lf_ref[pl.ds(r0, c), cs]
            lf_hi = lf.astype(BF16)
            lf_lo = (lf - lf_hi.astype(F32)).astype(BF16)
            b.append(jnp.dot(tri, lf_hi, preferred_element_type=F32)
                     + jnp.dot(tri, lf_lo, preferred_element_type=F32))
        b_last = [bh[c - 1:c, :] for bh in b]

        upd = []
        for it in range(n):
            ks = (k[it] * jnp.exp(b_last[it] - b[it])).astype(BF16)
            upd.append(lax.dot_general(v_bf[it], ks, (((0,), (0,)), ((), ())), preferred_element_type=F32))
        a = [jnp.zeros((c, c), F32) for _ in items]
        for w, msk in zip((32, 16, 8), level_masks):
            nb = c // w
            for it in range(n):
                start = _row_bcast(b[it], [j * w for j in range(nb)], w)
                nxt = _row_bcast(b[it], [(j + 1) * w for j in range(nb - 1)] + [c - 1], w)
                qh = (q[it] * jnp.exp(b[it] - start)).astype(BF16)
                kh = (k[it] * jnp.exp(nxt - b[it])).astype(BF16)
                aw = lax.dot_general(qh, kh, nt_dims, preferred_element_type=F32)
                a[it] = jnp.where(msk, aw, a[it])
                if factored and w == 8:
                    kb = (k[it] * jnp.exp(start - b[it])).astype(BF16)
                    aw = lax.dot_general(qh, kb, nt_dims, preferred_element_type=F32)
                    a[it] = jnp.where(same_block_mask, aw, a[it])

        if not factored:
            for it in range(n):
                b3 = b[it].reshape(c // 8, 8, HEAD_W)
                q3 = q[it].reshape(c // 8, 8, HEAD_W)
                k3 = k[it].reshape(c // 8, 8, HEAD_W)
                terms = []
                for s in range(8):
                    dec = jnp.exp(jnp.minimum(b3 - b3[:, s:s + 1, :], 0.0))
                    terms.append((q3 * dec * k3[:, s:s + 1, :]).reshape(c, HEAD_W).astype(BF16))
                sums = jnp.dot(jnp.concatenate(terms, axis=0), ones_sq, preferred_element_type=F32)
                for s in range(8):
                    a[it] = jnp.where(diag_masks[s], sums[s * c:(s + 1) * c, 0:c], a[it])

        o = [None] * n
        for h in range(HEADS):
            st = st_ref[h]
            for it in range(h, n, HEADS):
                qi = (q[it] * jnp.exp(b[it])).astype(BF16)
                o[it] = lax.dot_general(qi, st.astype(BF16), nt_dims, preferred_element_type=F32)
                st = st * jnp.exp(b_last[it]) + upd[it]
            st_ref[h] = st
        for it, (r0, cs, _) in enumerate(items):
            oh = o[it] + jnp.dot(a[it].astype(BF16), v_bf[it], preferred_element_type=F32)
            g = g_ref[pl.ds(r0, c), cs].astype(F32)
            o_ref[pl.ds(r0, c), cs] = (_rms(oh, gain) * g).astype(BF16)
        return carry

    tile = pl.program_id(0) * pl.num_programs(1) + pl.program_id(1)
    mild = lfmin_ref[tile] * -8.0 < HG_MAX_FACTORED_DECAY
    n_groups = HG_T // (c * HG_GROUP)

    @pl.when(mild)
    def _():
        lax.fori_loop(0, n_groups, functools.partial(chunk_group, factored=True), 0)

    @pl.when(jnp.logical_not(mild))
    def _():
        lax.fori_loop(0, n_groups, functools.partial(chunk_group, factored=False), 0)


def _hgrn(proj, lf, lfmin, gain, batch, seq):
    m = proj.shape[0]
    nt = seq // HG_T
    assert HG_T == IN_TM

    def spec(col):
        return pl.BlockSpec((HG_T, MIX_W), lambda b, t: (b * nt + t, col))

    return pl.pallas_call(
        _hgrn_kernel,
        grid=(batch, nt),
        in_specs=[pl.BlockSpec(memory_space=pltpu.SMEM),
                  spec(0), spec(0), spec(1), spec(2), spec(3),
                  pl.BlockSpec((1, HEAD_W), lambda b, t: (0, 0))],
        out_specs=spec(0),
        out_shape=jax.ShapeDtypeStruct((m, MIX_W), BF16),
        scratch_shapes=[pltpu.VMEM((HEADS, HEAD_W, HEAD_W), F32)],
        compiler_params=pltpu.CompilerParams(dimension_semantics=("parallel", "arbitrary"),
                                             vmem_limit_bytes=VMEM_LIMIT),
        name="hgrn2",
    )(lfmin, proj, lf, proj, proj, proj, gain)


ATT_HPG = 2
ATT_MAX_SCORE_BOUND = 60.0


def _attn_kernel(lam_ref, slope_ref, qt_ref, k_ref, vt_ref, gain_ref, o_ref,
                 kaug_ref, q2t_ref, acc_ref, sa_ref, sb_ref, pa_ref, pb_ref, stat_ref, lsum_ref):
    tq, r, tk = ATT_TQ, 2 * ATT_TQ, ATT_TK
    grp = pl.program_id(1)
    nblk = k_ref.shape[0] // tk
    slab = 64
    M_ROW, L_ROW, ALPHA_ROW, BMAX_ROW, KNORM_ROW, MREF_ROW = (slice(k, k + 1) for k in range(6))

    def col_reduce(x, op):
        return op(x.reshape(slab // 8, 8, r), axis=0)

    di = lax.broadcasted_iota(jnp.int32, (HEAD_W, HEAD_W), 0)
    ci = lax.broadcasted_iota(jnp.int32, (HEAD_W, HEAD_W), 1)
    same_comp = ((di < DQK) == (ci < DQK)).astype(BF16)

    def head(hh, i):
        cols = slice(hh * HEAD_W, (hh + 1) * HEAD_W)
        kaug, q2t, acc, stat = kaug_ref.at[hh], q2t_ref.at[hh], acc_ref.at[hh], stat_ref.at[hh]
        sa, sb = sa_ref, sb_ref
        pa, pb = pa_ref.at[hh], pb_ref.at[hh]
        lsum = lsum_ref.at[hh]
        slope = slope_ref[grp * ATT_HPG + hh] * LOG2E

        def split3(c, first_lane, lane):
            c1 = c.astype(BF16).astype(F32)
            c2 = (c - c1).astype(BF16).astype(F32)
            c3 = c - c1 - c2
            return jnp.where(lane == first_lane, c1,
                             jnp.where(lane == first_lane + 1, c2,
                                       jnp.where(lane == first_lane + 2, c3, 0.0)))

        def local_bias():
            row = lax.broadcasted_iota(jnp.int32, (tk, HEAD_W), 0)
            lane = lax.broadcasted_iota(jnp.int32, (tk, HEAD_W), 1)
            return split3(slope * row.astype(F32), 0, lane)

        def fill(jb, knorm_sq, local):
            r0 = pl.multiple_of(jb * tk, tk)
            k_blk = k_ref[pl.ds(r0, tk), cols]
            ksq = (k_blk.astype(F32) * k_blk.astype(F32)).astype(BF16)
            sums = jnp.dot(ksq, same_comp, preferred_element_type=F32)
            knorm_sq = jnp.maximum(knorm_sq, jnp.max(sums, axis=0, keepdims=True))
            kaug[pl.ds(r0, tk), 0:HEAD_W] = k_blk
            lane = lax.broadcasted_iota(jnp.int32, (1, HEAD_W), 1)
            base = split3(jnp.full((1, HEAD_W), slope * lax.convert_element_type(jb * tk, F32), F32), 3, lane)
            kaug[pl.ds(r0, tk), HEAD_W:2 * HEAD_W] = (local + base).astype(BF16)
            return knorm_sq

        def set_key_norms(knorm_sq):
            knorm = jnp.sqrt(knorm_sq * 1.02)
            stat[KNORM_ROW, :] = jnp.concatenate([jnp.broadcast_to(knorm[:, 0:1], (1, tq)),
                                                  jnp.broadcast_to(knorm[:, DQK:DQK + 1], (1, tq))], axis=1)

        def scores(j):
            r0 = pl.multiple_of(j * tk, tk)
            return jnp.dot(kaug[pl.ds(r0, tk), :], q2t[...], preferred_element_type=F32)

        def store_scores(s, s_out):
            part = jnp.full((8, r), NEG_BIG, F32)
            for a in range(0, tk, slab):
                sl = s[a:a + slab, :]
                s_out[a:a + slab, :] = sl
                part = jnp.maximum(part, col_reduce(sl, jnp.max))
            return jnp.max(part, axis=0, keepdims=True)

        def softmax(s_of, p_out, bmax, m_old, l_old):
            m_new = jnp.maximum(m_old, bmax)
            alpha = jnp.exp2(m_old - m_new)
            part = jnp.zeros((8, r), F32)
            for a in range(0, tk, slab):
                p = jnp.exp2(s_of(slice(a, a + slab)) - m_new)
                part = part + col_reduce(p, jnp.sum)
                p_out[a:a + slab, :] = p.astype(BF16)
            l_new = alpha * l_old + jnp.sum(part, axis=0, keepdims=True)
            return alpha, m_new, l_new

        def add_values(jv, p, alpha):
            acc[...] = alpha * acc[...] + jnp.dot(vt_ref[jv, cols, :], p, preferred_element_type=F32)

        def load_queries():
            qt = qt_ref[i, cols, :].astype(F32)
            sub = lax.broadcasted_iota(jnp.int32, (HEAD_W, tq), 0)
            q2t[0:HEAD_W, 0:tq] = jnp.where(sub < DQK, qt, 0.0).astype(BF16)
            q2t[0:HEAD_W, tq:r] = jnp.where(sub >= DQK, qt, 0.0).astype(BF16)
            sub_r = lax.broadcasted_iota(jnp.int32, (HEAD_W, r), 0)
            q2t[HEAD_W:2 * HEAD_W, :] = jnp.where(sub_r < 6, 1.0, 0.0).astype(BF16)
            qsq = qt * qt
            qnorm = jnp.sqrt(jnp.concatenate([jnp.sum(qsq[0:DQK, :], axis=0, keepdims=True),
                                              jnp.sum(qsq[DQK:HEAD_W, :], axis=0, keepdims=True)], axis=1))
            bound = qnorm * stat[KNORM_ROW, :]
            ql = lax.broadcasted_iota(jnp.int32, (1, r), 1)
            q_pos = (i * tq + jnp.where(ql >= tq, ql - tq, ql)).astype(F32)
            stat[MREF_ROW, :] = bound + slope * q_pos + 1.0
            return bound

        def bounded_start():
            acc[...] = jnp.zeros((HEAD_W, r), F32)
            lsum[...] = jnp.zeros((8, r), F32)

        def bounded_block(j, p_out, masked):
            s = scores(j)
            mref = stat[MREF_ROW, :]
            if masked:
                ql = lax.broadcasted_iota(jnp.int32, (slab, r), 1)
                lead = i * tq - j * tk + jnp.where(ql >= tq, ql - tq, ql)
                row = lax.broadcasted_iota(jnp.int32, (slab, r), 0)
            part = lsum[...]
            for a in range(0, tk, slab):
                e = s[a:a + slab, :] - mref
                if masked:
                    e = jnp.where(row + a <= lead, e, NEG_BIG)
                p = jnp.exp2(e)
                part = part + col_reduce(p, jnp.sum)
                p_out[a:a + slab, :] = p.astype(BF16)
            lsum[...] = part

        def bounded_values(jv, p_in):
            acc[...] = acc[...] + jnp.dot(vt_ref[jv, cols, :], p_in[...], preferred_element_type=F32)

        def bounded_finish():
            write_output(jnp.sum(lsum[...], axis=0, keepdims=True))

        def write_output(l_fin):
            on = acc[...] * (1.0 / l_fin)
            ot = on[:, 0:tq] - lam_ref[0] * on[:, tq:r]
            ot = ot * lax.rsqrt(jnp.mean(ot * ot, axis=0, keepdims=True) + EPS)
            o_ref[pl.ds(pl.multiple_of(i * tq, tq), tq), cols] = (ot.T * gain_ref[...]).astype(BF16)

        def start():
            acc[...] = jnp.zeros((HEAD_W, r), F32)
            pb[...] = jnp.zeros((tk, r), BF16)
            stat[BMAX_ROW, :] = store_scores(scores(0), sa)
            stat[M_ROW, :] = jnp.full((1, r), NEG_BIG, F32)
            stat[L_ROW, :] = jnp.zeros((1, r), F32)
            stat[ALPHA_ROW, :] = jnp.ones((1, r), F32)

        def stage(j, flip):
            s_in, s_out, p_in, p_out = (sb, sa, pa, pb) if flip else (sa, sb, pb, pa)
            bmax_next = store_scores(scores(j + 1), s_out)
            add_values(jnp.maximum(j - 1, 0), p_in[...], stat[ALPHA_ROW, :])
            alpha, m_new, l_new = softmax(lambda rows: s_in[rows, :], p_out, stat[BMAX_ROW, :],
                                          stat[M_ROW, :], stat[L_ROW, :])
            stat[M_ROW, :] = m_new
            stat[L_ROW, :] = l_new
            stat[ALPHA_ROW, :] = alpha
            stat[BMAX_ROW, :] = bmax_next

        def finish(j_last, flip):
            s_in, p_in, p_free = (sb, pa, pb) if flip else (sa, pb, pa)
            ql = lax.broadcasted_iota(jnp.int32, (slab, r), 1)
            lead = i * tq - j_last * tk + jnp.where(ql >= tq, ql - tq, ql)
            row = lax.broadcasted_iota(jnp.int32, (slab, r), 0)

            part = jnp.full((8, r), NEG_BIG, F32)
            for a in range(0, tk, slab):
                sl = jnp.where(row + a <= lead, s_in[a:a + slab, :], NEG_BIG)
                s_in[a:a + slab, :] = sl
                part = jnp.maximum(part, col_reduce(sl, jnp.max))
            add_values(jnp.maximum(j_last - 1, 0), p_in[...], stat[ALPHA_ROW, :])
            alpha, _, l_fin = softmax(lambda rows: s_in[rows, :], p_free,
                                      jnp.max(part, axis=0, keepdims=True),
                                      stat[M_ROW, :], stat[L_ROW, :])
            add_values(j_last, p_free[...], alpha)
            write_output(l_fin)

        return dict(fill=fill, local_bias=local_bias, set_key_norms=set_key_norms, load_queries=load_queries,
                    start=start, stage=stage, finish=finish,
                    bounded_start=bounded_start, bounded_block=bounded_block,
                    bounded_values=bounded_values, bounded_finish=bounded_finish, pa=pa, pb=pb)

    local_biases = [head(hh, 0)["local_bias"]() for hh in range(ATT_HPG)]

    def fill_all(jb, carry):
        return tuple(head(hh, 0)["fill"](jb, carry[hh], local_biases[hh]) for hh in range(ATT_HPG))
    knorm_sq = lax.fori_loop(0, nblk, fill_all, tuple(jnp.zeros((1, HEAD_W), F32) for _ in range(ATT_HPG)))
    for hh in range(ATT_HPG):
        head(hh, 0)["set_key_norms"](knorm_sq[hh])

    def query_tile(i, carry):
        heads = [head(hh, i) for hh in range(ATT_HPG)]
        n_full = (i * tq) // tk
        bound = functools.reduce(jnp.maximum, [h["load_queries"]() for h in heads])
        bounded = jnp.max(bound) < ATT_MAX_SCORE_BOUND

        def each(name, *args):
            for h in heads:
                h[name](*args)

        @pl.when(bounded)
        def _():
            each("bounded_start")
            for h in heads:
                h["bounded_block"](n_full, h["pb"], True)

            def pair(t, owed):
                for h in heads:
                    h["bounded_block"](2 * t, h["pa"], False)
                for h in heads:
                    h["bounded_values"](owed, h["pb"])
                for h in heads:
                    h["bounded_block"](2 * t + 1, h["pb"], False)
                for h in heads:
                    h["bounded_values"](2 * t, h["pa"])
                return 2 * t + 1
            owed = lax.fori_loop(0, n_full // 2, pair, n_full)

            @pl.when(n_full % 2 == 0)
            def _():
                for h in heads:
                    h["bounded_values"](owed, h["pb"])

            @pl.when(n_full % 2 == 1)
            def _():
                for h in heads:
                    h["bounded_block"](n_full - 1, h["pa"], False)
                for h in heads:
                    h["bounded_values"](owed, h["pb"])
                for h in heads:
                    h["bounded_values"](n_full - 1, h["pa"])
            each("bounded_finish")

        @pl.when(jnp.logical_not(bounded))
        def _():
            for h in heads:
                h["start"]()

                def pair(t, carry, h=h):
                    h["stage"](2 * t, False)
                    h["stage"](2 * t + 1, True)
                    return carry
                lax.fori_loop(0, n_full // 2, pair, 0)

                @pl.when(n_full % 2 == 0)
                def _(h=h):
                    h["finish"](n_full, False)

                @pl.when(n_full % 2 == 1)
                def _(h=h):
                    h["stage"](n_full - 1, False)
                    h["finish"](n_full, True)
        return carry

    lax.fori_loop(0, qt_ref.shape[0], query_tile, 0)


def _attn(lam, proj, qt, vt, gain, batch, seq):
    m = proj.shape[0]
    slopes = jnp.exp2(-(ALIBI_MAX_BIAS / HEADS) * jnp.arange(1, HEADS + 1, dtype=F32))
    assert ATT_TQ == IN_TM and ATT_TK == IN_TM
    nq = seq // ATT_TQ
    nk = seq // ATT_TK
    gw = ATT_HPG * HEAD_W
    kcol = KB_TILE * (MIX_W // gw)
    hpg, r = ATT_HPG, 2 * ATT_TQ
    return pl.pallas_call(
        _attn_kernel,
        grid=(batch, HEADS // ATT_HPG),
        in_specs=[pl.BlockSpec(memory_space=pltpu.SMEM),
                  pl.BlockSpec(memory_space=pltpu.SMEM),
                  pl.BlockSpec((nq, gw, ATT_TQ), lambda b, g: (b, g, 0)),
                  pl.BlockSpec((seq, gw), lambda b, g: (b, kcol + g)),
                  pl.BlockSpec((nk, gw, ATT_TK), lambda b, g: (b, g, 0)),
                  pl.BlockSpec((1, HEAD_W), lambda b, g: (0, 0))],
        out_specs=pl.BlockSpec((seq, gw), lambda b, g: (b, g)),
        out_shape=jax.ShapeDtypeStruct((m, MIX_W), BF16),
        scratch_shapes=[pltpu.VMEM((hpg, seq, 2 * HEAD_W), BF16),
                        pltpu.VMEM((hpg, 2 * HEAD_W, r), BF16),
                        pltpu.VMEM((hpg, HEAD_W, r), F32),
                        pltpu.VMEM((ATT_TK, r), F32),
                        pltpu.VMEM((ATT_TK, r), F32),
                        pltpu.VMEM((hpg, ATT_TK, r), BF16),
                        pltpu.VMEM((hpg, ATT_TK, r), BF16),
                        pltpu.VMEM((hpg, 8, r), F32),
                        pltpu.VMEM((hpg, 8, r), F32)],
        compiler_params=pltpu.CompilerParams(dimension_semantics=("parallel", "parallel"),
                                             vmem_limit_bytes=VMEM_LIMIT),
        name="diffattn",
    )(lam, slopes, qt, proj, vt, gain)


def _channel_kernel(ya_ref, yb_ref, ga0_ref, ga1_ref, gb0_ref, gb1_ref, x_ref,
                    wua_ref, wub_ref, wo_ref, wg_ref, wu_ref, wd_ref,
                    gmix_ref, gpre_ref, gpost_ref, gnext_ref, x2_ref, *maybe_hn_ref):
    x1_parts, h_parts = [], []
    for r0 in range(0, CH_TM, CH_TM // CH_ROW_PARTS):
        rs = slice(r0, r0 + CH_TM // CH_ROW_PARTS)
        ua = jnp.dot(ya_ref[rs, :], wua_ref[...], preferred_element_type=F32)
        ub = jnp.dot(yb_ref[rs, :], wub_ref[...], preferred_element_type=F32)
        ga = jnp.concatenate([ga0_ref[rs, :], ga1_ref[rs, :]], axis=1).astype(F32)
        gb = jnp.concatenate([gb0_ref[rs, :], gb1_ref[rs, :]], axis=1).astype(F32)
        merged = (ga * ua + gb * ub).astype(BF16)
        mix = jnp.dot(merged, wo_ref[...], preferred_element_type=F32)
        x1_part = x_ref[rs, :] + _rms(mix, gmix_ref[...])
        x1_parts.append(x1_part)
        h_parts.append(_rms(x1_part, gpre_ref[...]).astype(BF16))
    x1 = jnp.concatenate(x1_parts, axis=0)
    h = jnp.concatenate(h_parts, axis=0)

    acc = jnp.zeros((CH_TM, D_MODEL), F32)
    for c in range(D_FF // FF_TF):
        cs = slice(c * FF_TF, (c + 1) * FF_TF)
        g = jnp.dot(h, wg_ref[:, cs], preferred_element_type=F32)
        u = jnp.dot(h, wu_ref[:, cs], preferred_element_type=F32)
        ff = (g * _sigmoid(g) * u).astype(BF16)
        acc = acc + jnp.dot(ff, wd_ref[cs, :], preferred_element_type=F32)
    x2 = x1 + _rms(acc, gpost_ref[...])
    x2_ref[...] = x2
    for hn_ref in maybe_hn_ref:
        hn_ref[...] = _rms(x2, gnext_ref[...]).astype(BF16)


def _channel(ya, yb, proj, x2d, wua, wub, wo, wg, wu, wd, layer, gmix, gpre, gpost, gnext, emit_next):
    m = ya.shape[0]
    tm = CH_TM

    def row(width, col=0):
        return pl.BlockSpec((tm, width), lambda i: (i, col))

    out_specs = [row(D_MODEL)] + ([row(D_MODEL)] if emit_next else [])
    out_shape = ([jax.ShapeDtypeStruct((m, D_MODEL), F32)]
                 + ([jax.ShapeDtypeStruct((m, D_MODEL), BF16)] if emit_next else []))
    return pl.pallas_call(
        _channel_kernel,
        grid=(m // tm,),
        in_specs=[row(MIX_W), row(MIX_W)] + [row(MIX_W, GATE_TILE + g) for g in range(4)] + [
                  row(D_MODEL),
                  _resident((MIX_W, D_MODEL), layer), _resident((MIX_W, D_MODEL), layer),
                  _resident((D_MODEL, D_MODEL), layer),
                  _resident((D_MODEL, D_FF), layer), _resident((D_MODEL, D_FF), layer),
                  _resident((D_FF, D_MODEL), layer),
                  _resident((1, D_MODEL)), _resident((1, D_MODEL)), _resident((1, D_MODEL)),
                  _resident((1, D_MODEL))],
        out_specs=out_specs,
        out_shape=out_shape,
        compiler_params=pltpu.CompilerParams(dimension_semantics=("parallel",),
                                             vmem_limit_bytes=VMEM_LIMIT),
        name="channel",
    )(ya, yb, proj, proj, proj, proj, x2d, wua, wub, wo, wg, wu, wd, gmix, gpre, gpost, gnext)


def kernel(x, lower_bounds, norm_mix_pre, norm_mix_post, norm_ffn_pre, norm_ffn_post, w_in, hg_out_norm,
           da_subln, lambda_q1, lambda_k1, lambda_q2, lambda_k2, w_up_a, w_up_b, w_out, w_ffn_gate,
           w_ffn_up, w_ffn_down):
    batch, seq, _ = x.shape
    depth = w_in.shape[0]
    m = batch * seq
    x2d = x.reshape(m, D_MODEL).astype(F32)

    lb_all = jnp.cumsum(jax.nn.softmax(lower_bounds.astype(F32), axis=0), axis=0)
    lb_all = lb_all - lb_all[0:1]

    w_in, w_up_a, w_up_b, w_out, w_ffn_gate, w_ffn_up, w_ffn_down = (
        w.astype(BF16) for w in (w_in, w_up_a, w_up_b, w_out, w_ffn_gate, w_ffn_up, w_ffn_down))

    h = None
    for l in range(depth):
        lb = lb_all[l]
        lbp = jnp.zeros((8, MIX_W), F32)
        lbp = lbp.at[0].set(jnp.maximum(jnp.log(lb), NEG_BIG)).at[1].set(jnp.log1p(-lb)).at[2].set(1.0 - lb)
        proj, lf, lfmin, qt, vt = _in_proj(x2d if h is None else h, norm_mix_pre[l].reshape(1, D_MODEL),
                                           w_in, l, lbp, prenorm=h is None)

        ya = _hgrn(proj, lf, lfmin[:, 0, 0], hg_out_norm[l].reshape(1, HEAD_W), batch, seq)

        lam_init = 0.8 - 0.6 * math.exp(-0.3 * l)
        lam = (jnp.exp(jnp.sum(lambda_q1[l].astype(F32) * lambda_k1[l].astype(F32)))
               - jnp.exp(jnp.sum(lambda_q2[l].astype(F32) * lambda_k2[l].astype(F32))) + lam_init)
        sub_gain = (da_subln[l].astype(F32) * (1.0 - lam_init)).reshape(1, HEAD_W)
        yb = _attn(lam.reshape(1), proj, qt, vt, sub_gain, batch, seq)

        last = l == depth - 1
        g_next = norm_mix_pre[(l + 1) % depth].reshape(1, D_MODEL)
        outs = _channel(ya, yb, proj, x2d, w_up_a, w_up_b, w_out, w_ffn_gate, w_ffn_up, w_ffn_down, l,
                        norm_mix_post[l].reshape(1, D_MODEL),
                        norm_ffn_pre[l].reshape(1, D_MODEL), norm_ffn_post[l].reshape(1, D_MODEL),
                        g_next, emit_next=not last)
        x2d, h = (outs[0], None) if last else outs
    return x2d.reshape(batch, seq, D_MODEL)
```

```python
import functools
import math

import jax
import jax.numpy as jnp
from jax import lax
from jax.experimental import pallas as pl
from jax.experimental.pallas import tpu as pltpu

F32 = jnp.float32
BF16 = jnp.bfloat16

D_MODEL = 1024
HEADS = 4
HEAD_W = 128
MIX_W = HEADS * HEAD_W
DQK = 64
D_FF = 2816
D_IN = 7 * MIX_W + 2 * D_MODEL
EPS = 1e-6
ALIBI_MAX_BIAS = 8.0
NEG_BIG = -1e30
LOG2E = math.log2(math.e)

IN_TM, IN_TN = 512, 512
IN_ROW_PARTS = 4
HG_T, HG_C = 512, 64
HG_GROUP = 8
HG_MAX_FACTORED_DECAY = 80.0
ATT_TQ, ATT_TK = 512, 512
CH_TM, FF_TF = 512, 256
CH_ROW_PARTS = 4
VMEM_LIMIT = 56 * 1024 * 1024


def _sigmoid(x):
    return 1.0 / (1.0 + jnp.exp(-x))


def _rms(x, gain):
    return x * lax.rsqrt(jnp.mean(x * x, axis=-1, keepdims=True) + EPS) * gain


def _resident(shape, layer=None):
    if layer is None:
        return pl.BlockSpec(shape, lambda *_: (0,) * len(shape), pipeline_mode=pl.Buffered(1))
    return pl.BlockSpec((None,) + tuple(shape), lambda *_: (layer,) + (0,) * len(shape),
                        pipeline_mode=pl.Buffered(1))


PROJ_TILES = 9
PROJ_W = PROJ_TILES * 512
KB_TILE, GATE_TILE = 4, 5


def _in_proj_kernel(x_ref, g_ref, w_ref, lb_ref, o_ref, lf_ref, lfmin_ref, qt_ref, vt_ref, h_ref, *,
                    prenorm):
    if prenorm:
        h_ref[...] = _rms(x_ref[...], g_ref[...]).astype(BF16)
    else:
        h_ref[...] = x_ref[...]

    def silu(acc):
        return acc * _sigmoid(acc)

    lf_mins = []

    def forget_gate(acc, rs):
        log_lb, log_1m_lb, one_m_lb = lb_ref[0:1, :], lb_ref[1:2, :], lb_ref[2:3, :]
        e = jnp.exp(-jnp.abs(acc))
        log_sig = jnp.minimum(acc, 0.0) - jnp.log(1.0 + e)
        c = log_1m_lb + log_sig
        hi = jnp.maximum(log_lb, c)
        lf = hi + jnp.log(1.0 + jnp.exp(-jnp.abs(log_lb - c)))
        lf_ref[rs, :] = lf
        lf_mins.append(jnp.min(lf))
        return one_m_lb * (jnp.where(acc >= 0.0, e, 1.0) / (1.0 + e))

    def query_scale(acc):
        return acc * (LOG2E / math.sqrt(DQK))

    row_parts = [slice(r0, r0 + IN_TM // IN_ROW_PARTS) for r0 in range(0, IN_TM, IN_TM // IN_ROW_PARTS)]

    w_tiles = {}

    def column_tile(jt, rs):
        if jt not in w_tiles:
            w_tiles[jt] = w_ref[:, jt * IN_TN:(jt + 1) * IN_TN].astype(BF16)
        return jnp.dot(h_ref[rs, :], w_tiles[jt], preferred_element_type=F32)

    plan = [(0, silu, 0), (3, silu, 3)] + [(7 + g, _sigmoid, GATE_TILE + g) for g in range(4)]
    tail = [(2, None, 2), (5, None, KB_TILE)]
    for rs in row_parts:
        o_ref[rs, IN_TN:2 * IN_TN] = forget_gate(column_tile(1, rs), rs).astype(BF16)
    lfmin_ref[0] = jnp.full((8, HEAD_W), functools.reduce(jnp.minimum, lf_mins), F32)
    for jt, epilogue, out_tile in plan:
        for rs in row_parts:
            o_ref[rs, out_tile * IN_TN:(out_tile + 1) * IN_TN] = epilogue(column_tile(jt, rs)).astype(BF16)
    for rs in row_parts:
        qt_ref[0, :, rs] = query_scale(column_tile(4, rs)).T.astype(BF16)
        vt_ref[0, :, rs] = column_tile(6, rs).T.astype(BF16)
    for jt, _, out_tile in tail:
        for rs in row_parts:
            o_ref[rs, out_tile * IN_TN:(out_tile + 1) * IN_TN] = column_tile(jt, rs).astype(BF16)


def _in_proj(x_or_h, gain, w_all, layer, lbp, prenorm):
    m = x_or_h.shape[0]
    nt = m // IN_TM
    return pl.pallas_call(
        functools.partial(_in_proj_kernel, prenorm=prenorm),
        grid=(nt,),
        in_specs=[pl.BlockSpec((IN_TM, D_MODEL), lambda i: (i, 0)),
                  _resident((1, D_MODEL)), _resident((D_MODEL, D_IN), layer), _resident((8, MIX_W))],
        out_specs=[pl.BlockSpec((IN_TM, PROJ_W), lambda i: (i, 0)),
                   pl.BlockSpec((IN_TM, MIX_W), lambda i: (i, 0)),
                   pl.BlockSpec((1, 8, HEAD_W), lambda i: (i, 0, 0)),
                   pl.BlockSpec((1, MIX_W, IN_TM), lambda i: (i, 0, 0)),
                   pl.BlockSpec((1, MIX_W, IN_TM), lambda i: (i, 0, 0))],
        out_shape=[jax.ShapeDtypeStruct((m, PROJ_W), BF16),
                   jax.ShapeDtypeStruct((m, MIX_W), F32),
                   jax.ShapeDtypeStruct((nt, 8, HEAD_W), F32),
                   jax.ShapeDtypeStruct((nt, MIX_W, IN_TM), BF16),
                   jax.ShapeDtypeStruct((nt, MIX_W, IN_TM), BF16)],
        scratch_shapes=[pltpu.VMEM((IN_TM, D_MODEL), BF16)],
        compiler_params=pltpu.CompilerParams(dimension_semantics=("parallel",),
                                             vmem_limit_bytes=VMEM_LIMIT),
        name="in_proj",
    )(x_or_h, gain, w_all, lbp)


def _row_bcast(b, rows, w):
    return jnp.concatenate([jnp.broadcast_to(b[r:r + 1, :], (w, HEAD_W)) for r in rows], axis=0)


def _hgrn_kernel(lfmin_ref, q_ref, lf_ref, k_ref, v_ref, g_ref, gain_ref, o_ref, st_ref):
    @pl.when(pl.program_id(1) == 0)
    def _():
        st_ref[...] = jnp.zeros_like(st_ref)

    c = HG_C
    ri = lax.broadcasted_iota(jnp.int32, (c, c), 0)
    ci = lax.broadcasted_iota(jnp.int32, (c, c), 1)
    tri = (ci <= ri).astype(BF16)
    level_masks = []
    for w in (32, 16, 8):
        tb, sb = ri // w, ci // w
        level_masks.append((tb == sb + 1) & (tb % 2 == 1))
    diag_masks = [(ci == (ri // 8) * 8 + s) & (ri % 8 >= s) for s in range(8)]
    same_block_mask = (ri // 8 == ci // 8) & (ci <= ri)
    ones_sq = jnp.ones((HEAD_W, HEAD_W), BF16)
    gain = gain_ref[...]

    def chunk_group(ig, carry, factored):
        nt_dims = (((1,), (1,)), ((), ()))
        items = [(pl.multiple_of((ig * HG_GROUP + ch) * c, c), slice(h * HEAD_W, (h + 1) * HEAD_W), h)
                 for ch in range(HG_GROUP) for h in range(HEADS)]
        n = len(items)
        q = [q_ref[pl.ds(r0, c), cs].astype(F32) for r0, cs, _ in items]
        k = [k_ref[pl.ds(r0, c), cs].astype(F32) for r0, cs, _ in items]
        v_bf = [v_ref[pl.ds(r0, c), cs] for r0, cs, _ in items]

        b = []
        for r0, cs, _ in items:
            lf = lf_ref[pl.ds(r0, c), cs]
            lf_hi = lf.astype(BF16)
            lf_lo = (lf - lf_hi.astype(F32)).astype(BF16)
            b.append(jnp.dot(tri, lf_hi, preferred_element_type=F32)
                     + jnp.dot(tri, lf_lo, preferred_element_type=F32))
        b_last = [bh[c - 1:c, :] for bh in b]

        upd = []
        for it in range(n):
            ks = (k[it] * jnp.exp(b_last[it] - b[it])).astype(BF16)
            upd.append(lax.dot_general(v_bf[it], ks, (((0,), (0,)), ((), ())), preferred_element_type=F32))
        a = [jnp.zeros((c, c), F32) for _ in items]
        for w, msk in zip((32, 16, 8), level_masks):
            nb = c // w
            for it in range(n):
                start = _row_bcast(b[it], [j * w for j in range(nb)], w)
                nxt = _row_bcast(b[it], [(j + 1) * w for j in range(nb - 1)] + [c - 1], w)
                qh = (q[it] * jnp.exp(b[it] - start)).astype(BF16)
                kh = (k[it] * jnp.exp(nxt - b[it])).astype(BF16)
                aw = lax.dot_general(qh, kh, nt_dims, preferred_element_type=F32)
                a[it] = jnp.where(msk, aw, a[it])
                if factored and w == 8:
                    kb = (k[it] * jnp.exp(start - b[it])).astype(BF16)
                    aw = lax.dot_general(qh, kb, nt_dims, preferred_element_type=F32)
                    a[it] = jnp.where(same_block_mask, aw, a[it])

        if not factored:
            for it in range(n):
                b3 = b[it].reshape(c // 8, 8, HEAD_W)
                q3 = q[it].reshape(c // 8, 8, HEAD_W)
                k3 = k[it].reshape(c // 8, 8, HEAD_W)
                terms = []
                for s in range(8):
                    dec = jnp.exp(jnp.minimum(b3 - b3[:, s:s + 1, :], 0.0))
                    terms.append((q3 * dec * k3[:, s:s + 1, :]).reshape(c, HEAD_W).astype(BF16))
                sums = jnp.dot(jnp.concatenate(terms, axis=0), ones_sq, preferred_element_type=F32)
                for s in range(8):
                    a[it] = jnp.where(diag_masks[s], sums[s * c:(s + 1) * c, 0:c], a[it])

        o = [None] * n
        for h in range(HEADS):
            st = st_ref[h]
            for it in range(h, n, HEADS):
                qi = (q[it] * jnp.exp(b[it])).astype(BF16)
                o[it] = lax.dot_general(qi, st.astype(BF16), nt_dims, preferred_element_type=F32)
                st = st * jnp.exp(b_last[it]) + upd[it]
            st_ref[h] = st
        for it, (r0, cs, _) in enumerate(items):
            oh = o[it] + jnp.dot(a[it].astype(BF16), v_bf[it], preferred_element_type=F32)
            g = g_ref[pl.ds(r0, c), cs].astype(F32)
            o_ref[pl.ds(r0, c), cs] = (_rms(oh, gain) * g).astype(BF16)
        return carry

    tile = pl.program_id(0) * pl.num_programs(1) + pl.program_id(1)
    mild = lfmin_ref[tile] * -8.0 < HG_MAX_FACTORED_DECAY
    n_groups = HG_T // (c * HG_GROUP)

    @pl.when(mild)
    def _():
        lax.fori_loop(0, n_groups, functools.partial(chunk_group, factored=True), 0)

    @pl.when(jnp.logical_not(mild))
    def _():
        lax.fori_loop(0, n_groups, functools.partial(chunk_group, factored=False), 0)


def _hgrn(proj, lf, lfmin, gain, batch, seq):
    m = proj.shape[0]
    nt = seq // HG_T
    assert HG_T == IN_TM

    def spec(col):
        return pl.BlockSpec((HG_T, MIX_W), lambda b, t: (b * nt + t, col))

    return pl.pallas_call(
        _hgrn_kernel,
        grid=(batch, nt),
        in_specs=[pl.BlockSpec(memory_space=pltpu.SMEM),
                  spec(0), spec(0), spec(1), spec(2), spec(3),
                  pl.BlockSpec((1, HEAD_W), lambda b, t: (0, 0))],
        out_specs=spec(0),
        out_shape=jax.ShapeDtypeStruct((m, MIX_W), BF16),
        scratch_shapes=[pltpu.VMEM((HEADS, HEAD_W, HEAD_W), F32)],
        compiler_params=pltpu.CompilerParams(dimension_semantics=("parallel", "arbitrary"),
                                             vmem_limit_bytes=VMEM_LIMIT),
        name="hgrn2",
    )(lfmin, proj, lf, proj, proj, proj, gain)


ATT_HPG = 2
ATT_MAX_SCORE_BOUND = 60.0


def _attn_kernel(lam_ref, slope_ref, qt_ref, k_ref, vt_ref, gain_ref, o_ref,
                 kaug_ref, q2t_ref, acc_ref, sa_ref, sb_ref, pa_ref, pb_ref, stat_ref, lsum_ref):
    tq, r, tk = ATT_TQ, 2 * ATT_TQ, ATT_TK
    grp = pl.program_id(1)
    nblk = k_ref.shape[0] // tk
    slab = 64
    M_ROW, L_ROW, ALPHA_ROW, BMAX_ROW, KNORM_ROW, MREF_ROW = (slice(k, k + 1) for k in range(6))

    def col_reduce(x, op):
        return op(x.reshape(slab // 8, 8, r), axis=0)

    di = lax.broadcasted_iota(jnp.int32, (HEAD_W, HEAD_W), 0)
    ci = lax.broadcasted_iota(jnp.int32, (HEAD_W, HEAD_W), 1)
    same_comp = ((di < DQK) == (ci < DQK)).astype(BF16)

    def head(hh, i):
        cols = slice(hh * HEAD_W, (hh + 1) * HEAD_W)
        kaug, q2t, acc, stat = kaug_ref.at[hh], q2t_ref.at[hh], acc_ref.at[hh], stat_ref.at[hh]
        sa, sb = sa_ref, sb_ref
        pa, pb = pa_ref.at[hh], pb_ref.at[hh]
        lsum = lsum_ref.at[hh]
        slope = slope_ref[grp * ATT_HPG + hh] * LOG2E

        def split3(c, first_lane, lane):
            c1 = c.astype(BF16).astype(F32)
            c2 = (c - c1).astype(BF16).astype(F32)
            c3 = c - c1 - c2
            return jnp.where(lane == first_lane, c1,
                             jnp.where(lane == first_lane + 1, c2,
                                       jnp.where(lane == first_lane + 2, c3, 0.0)))

        def local_bias():
            row = lax.broadcasted_iota(jnp.int32, (tk, HEAD_W), 0)
            lane = lax.broadcasted_iota(jnp.int32, (tk, HEAD_W), 1)
            return split3(slope * row.astype(F32), 0, lane)

        def fill(jb, knorm_sq, local):
            r0 = pl.multiple_of(jb * tk, tk)
            k_blk = k_ref[pl.ds(r0, tk), cols]
            ksq = (k_blk.astype(F32) * k_blk.astype(F32)).astype(BF16)
            sums = jnp.dot(ksq, same_comp, preferred_element_type=F32)
            knorm_sq = jnp.maximum(knorm_sq, jnp.max(sums, axis=0, keepdims=True))
            kaug[pl.ds(r0, tk), 0:HEAD_W] = k_blk
            lane = lax.broadcasted_iota(jnp.int32, (1, HEAD_W), 1)
            base = split3(jnp.full((1, HEAD_W), slope * lax.convert_element_type(jb * tk, F32), F32), 3, lane)
            kaug[pl.ds(r0, tk), HEAD_W:2 * HEAD_W] = (local + base).astype(BF16)
            return knorm_sq

        def set_key_norms(knorm_sq):
            knorm = jnp.sqrt(knorm_sq * 1.02)
            stat[KNORM_ROW, :] = jnp.concatenate([jnp.broadcast_to(knorm[:, 0:1], (1, tq)),
                                                  jnp.broadcast_to(knorm[:, DQK:DQK + 1], (1, tq))], axis=1)

        def scores(j):
            r0 = pl.multiple_of(j * tk, tk)
            return jnp.dot(kaug[pl.ds(r0, tk), :], q2t[...], preferred_element_type=F32)

        def store_scores(s, s_out):
            part = jnp.full((8, r), NEG_BIG, F32)
            for a in range(0, tk, slab):
                sl = s[a:a + slab, :]
                s_out[a:a + slab, :] = sl
                part = jnp.maximum(part, col_reduce(sl, jnp.max))
            return jnp.max(part, axis=0, keepdims=True)

        def softmax(s_of, p_out, bmax, m_old, l_old):
            m_new = jnp.maximum(m_old, bmax)
            alpha = jnp.exp2(m_old - m_new)
            part = jnp.zeros((8, r), F32)
            for a in range(0, tk, slab):
                p = jnp.exp2(s_of(slice(a, a + slab)) - m_new)
                part = part + col_reduce(p, jnp.sum)
                p_out[a:a + slab, :] = p.astype(BF16)
            l_new = alpha * l_old + jnp.sum(part, axis=0, keepdims=True)
            return alpha, m_new, l_new

        def add_values(jv, p, alpha):
            acc[...] = alpha * acc[...] + jnp.dot(vt_ref[jv, cols, :], p, preferred_element_type=F32)

        def load_queries():
            qt = qt_ref[i, cols, :].astype(F32)
            sub = lax.broadcasted_iota(jnp.int32, (HEAD_W, tq), 0)
            q2t[0:HEAD_W, 0:tq] = jnp.where(sub < DQK, qt, 0.0).astype(BF16)
            q2t[0:HEAD_W, tq:r] = jnp.where(sub >= DQK, qt, 0.0).astype(BF16)
            sub_r = lax.broadcasted_iota(jnp.int32, (HEAD_W, r), 0)
            q2t[HEAD_W:2 * HEAD_W, :] = jnp.where(sub_r < 6, 1.0, 0.0).astype(BF16)
            qsq = qt * qt
            qnorm = jnp.sqrt(jnp.concatenate([jnp.sum(qsq[0:DQK, :], axis=0, keepdims=True),
                                              jnp.sum(qsq[DQK:HEAD_W, :], axis=0, keepdims=True)], axis=1))
            bound = qnorm * stat[KNORM_ROW, :]
            ql = lax.broadcasted_iota(jnp.int32, (1, r), 1)
            q_pos = (i * tq + jnp.where(ql >= tq, ql - tq, ql)).astype(F32)
            stat[MREF_ROW, :] = bound + slope * q_pos + 1.0
            return bound

        def bounded_start():
            acc[...] = jnp.zeros((HEAD_W, r), F32)
            lsum[...] = jnp.zeros((8, r), F32)

        def bounded_block(j, p_out, masked):
            s = scores(j)
            mref = stat[MREF_ROW, :]
            if masked:
                ql = lax.broadcasted_iota(jnp.int32, (slab, r), 1)
                lead = i * tq - j * tk + jnp.where(ql >= tq, ql - tq, ql)
                row = lax.broadcasted_iota(jnp.int32, (slab, r), 0)
            part = lsum[...]
            for a in range(0, tk, slab):
                e = s[a:a + slab, :] - mref
                if masked:
                    e = jnp.where(row + a <= lead, e, NEG_BIG)
                p = jnp.exp2(e)
                part = part + col_reduce(p, jnp.sum)
                p_out[a:a + slab, :] = p.astype(BF16)
            lsum[...] = part

        def bounded_values(jv, p_in):
            acc[...] = acc[...] + jnp.dot(vt_ref[jv, cols, :], p_in[...], preferred_element_type=F32)

        def bounded_finish():
            write_output(jnp.sum(lsum[...], axis=0, keepdims=True))

        def write_output(l_fin):
            on = acc[...] * (1.0 / l_fin)
            ot = on[:, 0:tq] - lam_ref[0] * on[:, tq:r]
            ot = ot * lax.rsqrt(jnp.mean(ot * ot, axis=0, keepdims=True) + EPS)
            o_ref[pl.ds(pl.multiple_of(i * tq, tq), tq), cols] = (ot.T * gain_ref[...]).astype(BF16)

        def start():
            acc[...] = jnp.zeros((HEAD_W, r), F32)
            pb[...] = jnp.zeros((tk, r), BF16)
            stat[BMAX_ROW, :] = store_scores(scores(0), sa)
            stat[M_ROW, :] = jnp.full((1, r), NEG_BIG, F32)
            stat[L_ROW, :] = jnp.zeros((1, r), F32)
            stat[ALPHA_ROW, :] = jnp.ones((1, r), F32)

        def stage(j, flip):
            s_in, s_out, p_in, p_out = (sb, sa, pa, pb) if flip else (sa, sb, pb, pa)
            bmax_next = store_scores(scores(j + 1), s_out)
            add_values(jnp.maximum(j - 1, 0), p_in[...], stat[ALPHA_ROW, :])
            alpha, m_new, l_new = softmax(lambda rows: s_in[rows, :], p_out, stat[BMAX_ROW, :],
                                          stat[M_ROW, :], stat[L_ROW, :])
            stat[M_ROW, :] = m_new
            stat[L_ROW, :] = l_new
            stat[ALPHA_ROW, :] = alpha
            stat[BMAX_ROW, :] = bmax_next

        def finish(j_last, flip):
            s_in, p_in, p_free = (sb, pa, pb) if flip else (sa, pb, pa)
            ql = lax.broadcasted_iota(jnp.int32, (slab, r), 1)
            lead = i * tq - j_last * tk + jnp.where(ql >= tq, ql - tq, ql)
            row = lax.broadcasted_iota(jnp.int32, (slab, r), 0)

            part = jnp.full((8, r), NEG_BIG, F32)
            for a in range(0, tk, slab):
                sl = jnp.where(row + a <= lead, s_in[a:a + slab, :], NEG_BIG)
                s_in[a:a + slab, :] = sl
                part = jnp.maximum(part, col_reduce(sl, jnp.max))
            add_values(jnp.maximum(j_last - 1, 0), p_in[...], stat[ALPHA_ROW, :])
            alpha, _, l_fin = softmax(lambda rows: s_in[rows, :], p_free,
                                      jnp.max(part, axis=0, keepdims=True),
                                      stat[M_ROW, :], stat[L_ROW, :])
            add_values(j_last, p_free[...], alpha)
            write_output(l_fin)

        return dict(fill=fill, local_bias=local_bias, set_key_norms=set_key_norms, load_queries=load_queries,
                    start=start, stage=stage, finish=finish,
                    bounded_start=bounded_start, bounded_block=bounded_block,
                    bounded_values=bounded_values, bounded_finish=bounded_finish, pa=pa, pb=pb)

    local_biases = [head(hh, 0)["local_bias"]() for hh in range(ATT_HPG)]

    def fill_all(jb, carry):
        return tuple(head(hh, 0)["fill"](jb, carry[hh], local_biases[hh]) for hh in range(ATT_HPG))
    knorm_sq = lax.fori_loop(0, nblk, fill_all, tuple(jnp.zeros((1, HEAD_W), F32) for _ in range(ATT_HPG)))
    for hh in range(ATT_HPG):
        head(hh, 0)["set_key_norms"](knorm_sq[hh])

    def query_tile(i, carry):
        heads = [head(hh, i) for hh in range(ATT_HPG)]
        n_full = (i * tq) // tk
        bound = functools.reduce(jnp.maximum, [h["load_queries"]() for h in heads])
        bounded = jnp.max(bound) < ATT_MAX_SCORE_BOUND

        def each(name, *args):
            for h in heads:
                h[name](*args)

        @pl.when(bounded)
        def _():
            each("bounded_start")
            for h in heads:
                h["bounded_block"](n_full, h["pb"], True)

            def pair(t, owed):
                for h in heads:
                    h["bounded_block"](2 * t, h["pa"], False)
                for h in heads:
                    h["bounded_values"](owed, h["pb"])
                for h in heads:
                    h["bounded_block"](2 * t + 1, h["pb"], False)
                for h in heads:
                    h["bounded_values"](2 * t, h["pa"])
                return 2 * t + 1
            owed = lax.fori_loop(0, n_full // 2, pair, n_full)

            @pl.when(n_full % 2 == 0)
            def _():
                for h in heads:
                    h["bounded_values"](owed, h["pb"])

            @pl.when(n_full % 2 == 1)
            def _():
                for h in heads:
                    h["bounded_block"](n_full - 1, h["pa"], False)
                for h in heads:
                    h["bounded_values"](owed, h["pb"])
                for h in heads:
                    h["bounded_values"](n_full - 1, h["pa"])
            each("bounded_finish")

        @pl.when(jnp.logical_not(bounded))
        def _():
            for h in heads:
                h["start"]()

                def pair(t, carry, h=h):
                    h["stage"](2 * t, False)
                    h["stage"](2 * t + 1, True)
                    return carry
                lax.fori_loop(0, n_full // 2, pair, 0)

                @pl.when(n_full % 2 == 0)
                def _(h=h):
                    h["finish"](n_full, False)

                @pl.when(n_full % 2 == 1)
                def _(h=h):
                    h["stage"](n_full - 1, False)
                    h["finish"](n_full, True)
        return carry

    lax.fori_loop(0, qt_ref.shape[0], query_tile, 0)


def _attn(lam, proj, qt, vt, gain, batch, seq):
    m = proj.shape[0]
    slopes = jnp.exp2(-(ALIBI_MAX_BIAS / HEADS) * jnp.arange(1, HEADS + 1, dtype=F32))
    assert ATT_TQ == IN_TM and ATT_TK == IN_TM
    nq = seq // ATT_TQ
    nk = seq // ATT_TK
    gw = ATT_HPG * HEAD_W
    kcol = KB_TILE * (MIX_W // gw)
    hpg, r = ATT_HPG, 2 * ATT_TQ
    return pl.pallas_call(
        _attn_kernel,
        grid=(batch, HEADS // ATT_HPG),
        in_specs=[pl.BlockSpec(memory_space=pltpu.SMEM),
                  pl.BlockSpec(memory_space=pltpu.SMEM),
                  pl.BlockSpec((nq, gw, ATT_TQ), lambda b, g: (b, g, 0)),
                  pl.BlockSpec((seq, gw), lambda b, g: (b, kcol + g)),
                  pl.BlockSpec((nk, gw, ATT_TK), lambda b, g: (b, g, 0)),
                  pl.BlockSpec((1, HEAD_W), lambda b, g: (0, 0))],
        out_specs=pl.BlockSpec((seq, gw), lambda b, g: (b, g)),
        out_shape=jax.ShapeDtypeStruct((m, MIX_W), BF16),
        scratch_shapes=[pltpu.VMEM((hpg, seq, 2 * HEAD_W), BF16),
                        pltpu.VMEM((hpg, 2 * HEAD_W, r), BF16),
                        pltpu.VMEM((hpg, HEAD_W, r), F32),
                        pltpu.VMEM((ATT_TK, r), F32),
                        pltpu.VMEM((ATT_TK, r), F32),
                        pltpu.VMEM((hpg, ATT_TK, r), BF16),
                        pltpu.VMEM((hpg, ATT_TK, r), BF16),
                        pltpu.VMEM((hpg, 8, r), F32),
                        pltpu.VMEM((hpg, 8, r), F32)],
        compiler_params=pltpu.CompilerParams(dimension_semantics=("parallel", "parallel"),
                                             vmem_limit_bytes=VMEM_LIMIT),
        name="diffattn",
    )(lam, slopes, qt, proj, vt, gain)


def _channel_kernel(ya_ref, yb_ref, ga0_ref, ga1_ref, gb0_ref, gb1_ref, x_ref,
                    wua_ref, wub_ref, wo_ref, wg_ref, wu_ref, wd_ref,
                    gmix_ref, gpre_ref, gpost_ref, gnext_ref, x2_ref, *maybe_hn_ref):
    x1_parts, h_parts = [], []
    for r0 in range(0, CH_TM, CH_TM // CH_ROW_PARTS):
        rs = slice(r0, r0 + CH_TM // CH_ROW_PARTS)
        ua = jnp.dot(ya_ref[rs, :], wua_ref[...], preferred_element_type=F32)
        ub = jnp.dot(yb_ref[rs, :], wub_ref[...], preferred_element_type=F32)
        ga = jnp.concatenate([ga0_ref[rs, :], ga1_ref[rs, :]], axis=1).astype(F32)
        gb = jnp.concatenate([gb0_ref[rs, :], gb1_ref[rs, :]], axis=1).astype(F32)
        merged = (ga * ua + gb * ub).astype(BF16)
        mix = jnp.dot(merged, wo_ref[...], preferred_element_type=F32)
        x1_part = x_ref[rs, :] + _rms(mix, gmix_ref[...])
        x1_parts.append(x1_part)
        h_parts.append(_rms(x1_part, gpre_ref[...]).astype(BF16))
    x1 = jnp.concatenate(x1_parts, axis=0)
    h = jnp.concatenate(h_parts, axis=0)

    acc = jnp.zeros((CH_TM, D_MODEL), F32)
    for c in range(D_FF // FF_TF):
        cs = slice(c * FF_TF, (c + 1) * FF_TF)
        g = jnp.dot(h, wg_ref[:, cs], preferred_element_type=F32)
        u = jnp.dot(h, wu_ref[:, cs], preferred_element_type=F32)
        ff = (g * _sigmoid(g) * u).astype(BF16)
        acc = acc + jnp.dot(ff, wd_ref[cs, :], preferred_element_type=F32)
    x2 = x1 + _rms(acc, gpost_ref[...])
    x2_ref[...] = x2
    for hn_ref in maybe_hn_ref:
        hn_ref[...] = _rms(x2, gnext_ref[...]).astype(BF16)


def _channel(ya, yb, proj, x2d, wua, wub, wo, wg, wu, wd, layer, gmix, gpre, gpost, gnext, emit_next):
    m = ya.shape[0]
    tm = CH_TM

    def row(width, col=0):
        return pl.BlockSpec((tm, width), lambda i: (i, col))

    out_specs = [row(D_MODEL)] + ([row(D_MODEL)] if emit_next else [])
    out_shape = ([jax.ShapeDtypeStruct((m, D_MODEL), F32)]
                 + ([jax.ShapeDtypeStruct((m, D_MODEL), BF16)] if emit_next else []))
    return pl.pallas_call(
        _channel_kernel,
        grid=(m // tm,),
        in_specs=[row(MIX_W), row(MIX_W)] + [row(MIX_W, GATE_TILE + g) for g in range(4)] + [
                  row(D_MODEL),
                  _resident((MIX_W, D_MODEL), layer), _resident((MIX_W, D_MODEL), layer),
                  _resident((D_MODEL, D_MODEL), layer),
                  _resident((D_MODEL, D_FF), layer), _resident((D_MODEL, D_FF), layer),
                  _resident((D_FF, D_MODEL), layer),
                  _resident((1, D_MODEL)), _resident((1, D_MODEL)), _resident((1, D_MODEL)),
                  _resident((1, D_MODEL))],
        out_specs=out_specs,
        out_shape=out_shape,
        compiler_params=pltpu.CompilerParams(dimension_semantics=("parallel",),
                                             vmem_limit_bytes=VMEM_LIMIT),
        name="channel",
    )(ya, yb, proj, proj, proj, proj, x2d, wua, wub, wo, wg, wu, wd, gmix, gpre, gpost, gnext)


def kernel(x, lower_bounds, norm_mix_pre, norm_mix_post, norm_ffn_pre, norm_ffn_post, w_in, hg_out_norm,
           da_subln, lambda_q1, lambda_k1, lambda_q2, lambda_k2, w_up_a, w_up_b, w_out, w_ffn_gate,
           w_ffn_up, w_ffn_down):
    batch, seq, _ = x.shape
    depth = w_in.shape[0]
    m = batch * seq
    x2d = x.reshape(m, D_MODEL).astype(F32)

    lb_all = jnp.cumsum(jax.nn.softmax(lower_bounds.astype(F32), axis=0), axis=0)
    lb_all = lb_all - lb_all[0:1]

    w_up_a, w_up_b, w_out, w_ffn_gate, w_ffn_up, w_ffn_down = (
        w.astype(BF16) for w in (w_up_a, w_up_b, w_out, w_ffn_gate, w_ffn_up, w_ffn_down))

    h = None
    for l in range(depth):
        lb = lb_all[l]
        lbp = jnp.zeros((8, MIX_W), F32)
        lbp = lbp.at[0].set(jnp.maximum(jnp.log(lb), NEG_BIG)).at[1].set(jnp.log1p(-lb)).at[2].set(1.0 - lb)
        proj, lf, lfmin, qt, vt = _in_proj(x2d if h is None else h, norm_mix_pre[l].reshape(1, D_MODEL),
                                           w_in, l, lbp, prenorm=h is None)

        ya = _hgrn(proj, lf, lfmin[:, 0, 0], hg_out_norm[l].reshape(1, HEAD_W), batch, seq)

        lam_init = 0.8 - 0.6 * math.exp(-0.3 * l)
        lam = (jnp.exp(jnp.sum(lambda_q1[l].astype(F32) * lambda_k1[l].astype(F32)))
               - jnp.exp(jnp.sum(lambda_q2[l].astype(F32) * lambda_k2[l].astype(F32))) + lam_init)
        sub_gain = (da_subln[l].astype(F32) * (1.0 - lam_init)).reshape(1, HEAD_W)
        yb = _attn(lam.reshape(1), proj, qt, vt, sub_gain, batch, seq)

        last = l == depth - 1
        g_next = norm_mix_pre[(l + 1) % depth].reshape(1, D_MODEL)
        outs = _channel(ya, yb, proj, x2d, w_up_a, w_up_b, w_out, w_ffn_gate, w_ffn_up, w_ffn_down, l,
                        norm_mix_post[l].reshape(1, D_MODEL),
                        norm_ffn_pre[l].reshape(1, D_MODEL), norm_ffn_post[l].reshape(1, D_MODEL),
                        g_next, emit_next=not last)
        x2d, h = (outs[0], None) if last else outs
    return x2d.reshape(batch, seq, D_MODEL)
```
